```python
import math
import jax, jax.numpy as jnp
from jax import lax
import numpy as np

D_MODEL = 1024
BATCH = 8
SEQ = 8192
DEPTH = 2
DEC_BATCH = 16
DEC_SEQ = 64
PAST_LEN = 1024

CHUNK = 64
Q_BLOCK = 128
EPS = 1e-6
NEG_BIG = -1e30

A_HEADS = 8
A_DQK = 64
A_DV = 2 * A_DQK
A_ROT = A_DQK // 4
ROPE_THETA = 500000.0
A_QK_W = A_HEADS * 2 * A_DQK
A_WIDTH = A_HEADS * A_DV

B_HEADS = 8
B_DK = 128
B_DV = 128
B_QK_W = B_HEADS * B_DK
B_WIDTH = B_HEADS * B_DV

C_HEADS = 16
C_DH = 64
C_WIDTH = C_HEADS * C_DH
C_DECAY_RANK = 64
C_A_RANK = 64
C_VRES_RANK = 32
C_GN_EPS = 64e-5
C_SIZES = (C_WIDTH, C_DECAY_RANK, C_WIDTH, C_WIDTH, C_A_RANK)
C_SHIFT_W = 3 * C_WIDTH + C_DECAY_RANK + C_A_RANK

IN_SIZES = (A_QK_W, A_QK_W, A_WIDTH, A_WIDTH,
            B_QK_W, B_QK_W, B_WIDTH, B_WIDTH,
            C_SHIFT_W, C_WIDTH,
            D_MODEL, D_MODEL, D_MODEL)
IN_COLS = 2 * A_QK_W + 2 * A_WIDTH + 2 * B_QK_W + 2 * B_WIDTH + C_SHIFT_W + C_WIDTH + 3 * D_MODEL

kernel_name = "hybrid_stream_diffattn_hgrn2_rwkv7_step"


def rms_norm(x, g):
    xf = x.astype(jnp.float32)
    y = xf * lax.rsqrt(jnp.mean(xf * xf, axis=-1, keepdims=True) + EPS)
    return y.astype(x.dtype) * g


def split_cols(p, sizes):
    idx = [int(s) for s in np.cumsum(sizes)[:-1]]
    return jnp.split(p, idx, axis=-1)


def partial_rope(x, pos):
    half = A_ROT // 2
    inv_freq = ROPE_THETA ** (-(jnp.arange(half, dtype=jnp.float32) * (2.0 / A_ROT)))
    ang = pos.astype(jnp.float32)[:, None] * inv_freq[None, :]
    shp = (pos.shape[0],) + (1,) * (x.ndim - 3) + (half,)
    cos = jnp.cos(ang).reshape(shp)
    sin = jnp.sin(ang).reshape(shp)
    x1 = x[..., :half].astype(jnp.float32)
    x2 = x[..., half:A_ROT].astype(jnp.float32)
    rot = jnp.concatenate([x1 * cos - x2 * sin, x2 * cos + x1 * sin], axis=-1).astype(x.dtype)
    return jnp.concatenate([rot, x[..., A_ROT:]], axis=-1)


def diff_attn_block(q, k, v, q_pos, k_pos, lam):
    s = jnp.einsum("bqhnd,bkhnd->bhnqk", q, k, preferred_element_type=jnp.float32) * (A_DQK ** -0.5)
    visible = (k_pos[None, :] // CHUNK) <= (q_pos[:, None] // CHUNK)
    s = jnp.where(visible, s, NEG_BIG)
    p = jax.nn.softmax(s, axis=-1)
    w = p[:, :, 0] - lam * p[:, :, 1]
    return jnp.einsum("bhqk,bkhd->bqhd", w.astype(v.dtype), v)


def mixer_diff_attn(a_q, a_k, a_v, pos, past_k, past_v, qn_g, kn_g, lam_p, subln_g, l):
    Bsz, T, _ = a_q.shape
    q = partial_rope(rms_norm(a_q.reshape(Bsz, T, A_HEADS, 2, A_DQK), qn_g), pos)
    k = partial_rope(rms_norm(a_k.reshape(Bsz, T, A_HEADS, 2, A_DQK), kn_g), pos)
    v = a_v.reshape(Bsz, T, A_HEADS, A_DV)
    if past_k is None:
        k_all, v_all, k_pos = k, v, pos
    else:
        p_len = past_k.shape[1]
        k_all = jnp.concatenate([past_k.reshape(Bsz, p_len, A_HEADS, 2, A_DQK).astype(k.dtype), k], axis=1)
        v_all = jnp.concatenate([past_v.astype(v.dtype), v], axis=1)
        k_pos = jnp.concatenate([jnp.arange(p_len, dtype=jnp.int32), pos])
    lam_init = 0.8 - 0.6 * math.exp(-0.3 * l)
    lp = lam_p.astype(jnp.float32)
    lam = jnp.exp(jnp.sum(lp[0] * lp[1])) - jnp.exp(jnp.sum(lp[2] * lp[3])) + lam_init
    if T > Q_BLOCK and T % Q_BLOCK == 0:
        nb = T // Q_BLOCK
        qb = q.reshape(Bsz, nb, Q_BLOCK, A_HEADS, 2, A_DQK).transpose(1, 0, 2, 3, 4, 5)
        pb = pos.reshape(nb, Q_BLOCK)
        o = lax.map(lambda blk: diff_attn_block(blk[0], k_all, v_all, blk[1], k_pos, lam), (qb, pb))
        o = o.transpose(1, 0, 2, 3, 4).reshape(Bsz, T, A_HEADS, A_DV)
    else:
        o = diff_attn_block(q, k_all, v_all, pos, k_pos, lam)
    o = rms_norm(o, subln_g) * (1.0 - lam_init)
    return o.reshape(Bsz, T, A_WIDTH), k.reshape(Bsz, T, A_HEADS, 2 * A_DQK), v


def mixer_hgrn2(b_q, b_f, b_i, s0, lb, norm_g):
    f32 = jnp.float32
    Bsz, T, _ = b_q.shape
    z = b_f.astype(f32)
    lb = lb.astype(f32)
    log_f = jnp.log(lb + (1.0 - lb) * jax.nn.sigmoid(z))
    k_in = (1.0 - lb) * jax.nn.sigmoid(-z)
    q = jax.nn.silu(b_q.astype(f32))
    i = b_i.astype(f32)
    pad = (-T) % CHUNK
    nc = (T + pad) // CHUNK

    def blocks(t, d):
        t = jnp.pad(t.reshape(Bsz, T, B_HEADS, d), ((0, 0), (0, pad), (0, 0), (0, 0)))
        return t.reshape(Bsz, nc, CHUNK, B_HEADS, d).transpose(1, 0, 3, 2, 4)

    causal = jnp.tril(jnp.ones((CHUNK, CHUNK), dtype=bool))

    def step(S, inp):
        qc, lfc, kc, ic = inp
        cum = jnp.cumsum(lfc, axis=2)
        rel = cum[:, :, :, None, :] - cum[:, :, None, :, :]
        dec = jnp.where(causal[None, None, :, :, None], jnp.exp(jnp.minimum(rel, 0.0)), 0.0)
        scores = jnp.einsum("bhtk,bhtsk,bhsk->bhts", qc, dec, kc)
        o = (jnp.einsum("bhts,bhsv->bhtv", scores, ic)
             + jnp.einsum("bhtk,bhkv->bhtv", qc * jnp.exp(cum), S))
        tail = jnp.exp(cum[:, :, -1:, :] - cum)
        S = jnp.exp(cum[:, :, -1, :])[..., None] * S + jnp.einsum("bhsk,bhsv->bhkv", kc * tail, ic)
        return S, o

    s_fin, o = lax.scan(step, s0.astype(f32),
                        (blocks(q, B_DK), blocks(log_f, B_DK), blocks(k_in, B_DK), blocks(i, B_DV)))
    o = o.transpose(1, 0, 3, 2, 4).reshape(Bsz, nc * CHUNK, B_HEADS, B_DV)[:, :T]
    o = rms_norm(o, norm_g)
    return o.reshape(Bsz, T, B_WIDTH).astype(b_q.dtype), s_fin


def rwkv7_scan(s0, r, w, k, v, a, b):
    def step(S, inp):
        r_t, w_t, k_t, v_t, a_t, b_t = inp
        sa = jnp.einsum("bhij,bhj->bhi", S, a_t)
        S = S * w_t[:, :, None, :] + sa[..., None] * b_t[:, :, None, :] + v_t[..., None] * k_t[:, :, None, :]
        return S, jnp.einsum("bhij,bhj->bhi", S, r_t)
    xs = tuple(jnp.moveaxis(t, 1, 0) for t in (r, w, k, v, a, b))
    s_fin, y = lax.scan(step, s0, xs)
    return s_fin, jnp.moveaxis(y, 0, 1)


def mixer_rwkv7(c_p, shift_prev, s0, h, v_first, l, P):
    f32 = jnp.float32
    Bsz, T, _ = c_p.shape
    prev = jnp.concatenate([shift_prev[:, None, :].astype(c_p.dtype), c_p[:, :-1]], axis=1)
    cs = c_p + (prev - c_p) * P["c_shift_mu"][l]
    r, w_lo, k, v, a_lo = split_cols(cs.astype(f32), C_SIZES)
    w_log = -jax.nn.softplus(-(P["c_w0"][l] + jnp.tanh(w_lo) @ P["c_w2"][l])) - 0.5
    decay = jnp.exp(-jnp.exp(w_log))
    a = jax.nn.sigmoid(P["c_a0"][l] + a_lo @ P["c_a2"][l])
    if l > 0:
        v_mix = jax.nn.sigmoid(P["c_v0"][l - 1] + (h @ P["c_vres_w1"][l - 1]) @ P["c_vres_w2"][l - 1])
        v = v + (v_first - v) * v_mix.astype(f32)
    heads = lambda t: t.reshape(Bsz, T, C_HEADS, C_DH)
    hp = lambda t: t.reshape(C_HEADS, C_DH)
    r, k, vh, decay, a = heads(r), heads(k), heads(v), heads(decay), heads(a)
    kk = k * hp(P["c_k_k"][l])
    kk = kk / jnp.maximum(jnp.sqrt(jnp.sum(kk * kk, axis=-1, keepdims=True)), 1e-12)
    k = k * (1.0 + (a - 1.0) * hp(P["c_k_a"][l]))
    s_fin, y = rwkv7_scan(s0.astype(f32), r, decay, k, vh, -kk, kk * a)
    mu = jnp.mean(y, axis=-1, keepdims=True)
    var = jnp.mean(jnp.square(y - mu), axis=-1, keepdims=True)
    y = (y - mu) * lax.rsqrt(var + C_GN_EPS) * hp(P["c_ln_w"][l]) + hp(P["c_ln_b"][l])
    y = y + jnp.sum(r * k * P["c_r_k"][l], axis=-1, keepdims=True) * vh
    return y.reshape(Bsz, T, C_WIDTH).astype(h.dtype), s_fin, c_p[:, -1], v


def trunk_layer(l, x, pos, P, lb, past_k, past_v, s_hgrn, s_rwkv, shift_prev, v_first):
    h = rms_norm(x, P["norm_g"][l])
    proj = jnp.einsum("btd,dc->btc", h, P["w_in"][l])
    a_q, a_k, a_v, a_g, b_q, b_f, b_i, b_g, c_p, c_g, m_a, m_b, m_c = split_cols(proj, IN_SIZES)
    o_a, k_rows, v_rows = mixer_diff_attn(a_q, a_k, a_v, pos, past_k, past_v,
                                          P["a_qnorm_g"][l], P["a_knorm_g"][l],
                                          P["a_lambda"][l], P["a_subln_g"][l], l)
    o_b, s_hgrn_new = mixer_hgrn2(b_q, b_f, b_i, s_hgrn, lb[l], P["b_norm_g"][l])
    o_c, s_rwkv_new, shift_new, v_c = mixer_rwkv7(c_p, shift_prev, s_rwkv, h, v_first, l, P)

    def branch(o, gate, w):
        return jnp.einsum("btc,cd->btd", o * jax.nn.silu(gate), w)

    merged = (jax.nn.sigmoid(m_a) * branch(o_a, a_g, P["w_out_a"][l])
              + jax.nn.sigmoid(m_b) * branch(o_b, b_g, P["w_out_b"][l])
              + jax.nn.sigmoid(m_c) * branch(o_c, c_g, P["w_out_c"][l]))
    y = x + jnp.einsum("btd,de->bte", merged, P["w_o"][l])
    return y, (k_rows, v_rows, s_hgrn_new, s_rwkv_new, shift_new), v_c


def run_trunk(x, pos, P, lb, past):
    Bsz = x.shape[0]
    outs = ([], [], [], [], [])
    v_first = None
    for l in range(DEPTH):
        if past is None:
            pk, pv = None, None
            s_h = jnp.zeros((Bsz, B_HEADS, B_DK, B_DV), jnp.float32)
            s_r = jnp.zeros((Bsz, C_HEADS, C_DH, C_DH), jnp.float32)
            s_sh = jnp.zeros((Bsz, C_SHIFT_W), x.dtype)
        else:
            pk, pv, s_h, s_r, s_sh = (t[l] for t in past)
        x, entries, v_c = trunk_layer(l, x, pos, P, lb, pk, pv, s_h, s_r, s_sh, v_first)
        if l == 0:
            v_first = v_c
        for lst, e in zip(outs, entries):
            lst.append(e)
    return x, [jnp.stack(lst) for lst in outs]


def setup_inputs(seed: int = 0) -> dict:
    key = jax.random.key(seed)
    ks = iter(jax.random.split(key, 40))
    nrm = lambda shape, scale: jax.random.normal(next(ks), shape, jnp.float32) * scale
    gain = lambda shape: 1.0 + 0.05 * jax.random.normal(next(ks), shape, jnp.float32)
    unif = lambda shape, lo, hi: jax.random.uniform(next(ks), shape, jnp.float32, lo, hi)
    L = DEPTH
    return {
        "x_prompt": nrm((BATCH, SEQ, D_MODEL), 1.0),
        "x_sample": nrm((DEC_BATCH, DEC_SEQ, D_MODEL), 1.0),
        "cache_attn_k": nrm((L, DEC_BATCH, PAST_LEN, A_HEADS, 2 * A_DQK), 1.0),
        "cache_attn_v": nrm((L, DEC_BATCH, PAST_LEN, A_HEADS, A_DV), 1.0),
        "state_hgrn": nrm((L, DEC_BATCH, B_HEADS, B_DK, B_DV), 0.5),
        "state_rwkv": nrm((L, DEC_BATCH, C_HEADS, C_DH, C_DH), 0.3),
        "state_rwkv_shift": nrm((L, DEC_BATCH, C_SHIFT_W), 1.0),
        "norm_g": gain((L, D_MODEL)),
        "w_in": nrm((L, D_MODEL, IN_COLS), D_MODEL ** -0.5),
        "a_qnorm_g": gain((L, A_DQK)),
        "a_knorm_g": gain((L, A_DQK)),
        "a_lambda": nrm((L, 4, A_DQK), 0.1),
        "a_subln_g": gain((L, A_DV)),
        "b_lower": nrm((L, B_QK_W), 1.0),
        "b_norm_g": gain((L, B_DV)),
        "c_shift_mu": unif((L, C_SHIFT_W), 0.0, 1.0),
        "c_w0": unif((L, C_WIDTH), -4.0, 1.0),
        "c_w2": nrm((L, C_DECAY_RANK, C_WIDTH), 0.1),
        "c_a0": nrm((L, C_WIDTH), 0.5),
        "c_a2": nrm((L, C_A_RANK, C_WIDTH), 0.1),
        "c_k_k": 0.85 + nrm((L, C_WIDTH), 0.05),
        "c_k_a": gain((L, C_WIDTH)),
        "c_r_k": nrm((L, C_HEADS, C_DH), 0.3),
        "c_ln_w": gain((L, C_WIDTH)),
        "c_ln_b": nrm((L, C_WIDTH), 0.02),
        "c_vres_w1": nrm((L - 1, D_MODEL, C_VRES_RANK), D_MODEL ** -0.5),
        "c_vres_w2": nrm((L - 1, C_VRES_RANK, C_WIDTH), 0.1),
        "c_v0": nrm((L - 1, C_WIDTH), 0.5),
        "w_out_a": nrm((L, A_WIDTH, D_MODEL), A_WIDTH ** -0.5),
        "w_out_b": nrm((L, B_WIDTH, D_MODEL), B_WIDTH ** -0.5),
        "w_out_c": nrm((L, C_WIDTH, D_MODEL), C_WIDTH ** -0.5),
        "w_o": nrm((L, D_MODEL, D_MODEL), D_MODEL ** -0.5),
    }


def reference(x_prompt, x_sample, cache_attn_k, cache_attn_v, state_hgrn, state_rwkv, state_rwkv_shift,
              norm_g, w_in, a_qnorm_g, a_knorm_g, a_lambda, a_subln_g, b_lower, b_norm_g,
              c_shift_mu, c_w0, c_w2, c_a0, c_a2, c_k_k, c_k_a, c_r_k, c_ln_w, c_ln_b,
              c_vres_w1, c_vres_w2, c_v0, w_out_a, w_out_b, w_out_c, w_o):
    P = {"norm_g": norm_g, "w_in": w_in, "a_qnorm_g": a_qnorm_g, "a_knorm_g": a_knorm_g,
         "a_lambda": a_lambda, "a_subln_g": a_subln_g, "b_norm_g": b_norm_g,
         "c_shift_mu": c_shift_mu, "c_w0": c_w0, "c_w2": c_w2, "c_a0": c_a0, "c_a2": c_a2,
         "c_k_k": c_k_k, "c_k_a": c_k_a, "c_r_k": c_r_k, "c_ln_w": c_ln_w, "c_ln_b": c_ln_b,
         "c_vres_w1": c_vres_w1, "c_vres_w2": c_vres_w2, "c_v0": c_v0,
         "w_out_a": w_out_a, "w_out_b": w_out_b, "w_out_c": w_out_c, "w_o": w_o}
    sm = jax.nn.softmax(b_lower.astype(jnp.float32), axis=0)
    lb = jnp.cumsum(sm, axis=0) - sm[0:1]
    pos_p = jnp.arange(x_prompt.shape[1], dtype=jnp.int32)
    past_len = cache_attn_k.shape[2]
    pos_s = past_len + jnp.arange(x_sample.shape[1], dtype=jnp.int32)
    y_prompt, (k_p, v_p, hg_p, rw_p, sh_p) = run_trunk(x_prompt, pos_p, P, lb, None)
    y_sample, (k_s, v_s, hg_s, rw_s, sh_s) = run_trunk(
        x_sample, pos_s, P, lb, (cache_attn_k, cache_attn_v, state_hgrn, state_rwkv, state_rwkv_shift))
    return (y_prompt, y_sample, k_p, v_p, hg_p, rw_p, sh_p, k_s, v_s, hg_s, rw_s, sh_s)
```

```python
import functools
import math

import jax
import jax.numpy as jnp
from jax import lax
from jax.experimental import pallas as pl
from jax.experimental.pallas import tpu as pltpu

F32 = jnp.float32
BF16 = jnp.bfloat16

D_MODEL = 1024
CHUNK = 64
EPS = 1e-6
NEG_BIG = -1e30
A_HEADS = 8
A_DQK = 64
A_ROT = 16
ROPE_THETA = 500000.0
B_HEADS = 8
C_HEADS = 16
C_DH = 64
C_WIDTH = 1024
C_DECAY_RANK = 64
C_A_RANK = 64
C_VRES_RANK = 32
C_GN_EPS = 64e-5
LANES = 128
SUBLANES = 8
GROUP_W = 1024
N_GROUPS = 15
COL_LO = N_GROUPS * GROUP_W
COL_VRES = COL_LO + LANES
N_PROJ = COL_VRES + LANES
PROJ_TN = 512
N_PROJ_PAD = -(-N_PROJ // PROJ_TN) * PROJ_TN
(G_AQ, G_AK, G_AV, G_AG, G_BQ, G_BF, G_BI, G_BG, G_CR, G_CK, G_CV, G_CG, G_MA, G_MB, G_MC) = range(N_GROUPS)
VMEM_LIMIT = 56 * 1024 * 1024


def _dot(a, b):
    return jnp.dot(a, b, preferred_element_type=F32)


def _dot_nt(a, b):
    return lax.dot_general(a, b, (((1,), (1,)), ((), ())), preferred_element_type=F32)


def _dot_tn(a, b):
    return lax.dot_general(a, b, (((0,), (0,)), ((), ())), preferred_element_type=F32)


def _split2(x):
    hi = x.astype(BF16)
    lo = (x - hi.astype(F32)).astype(BF16)
    return hi, lo


def _split3(x):
    hi = x.astype(BF16)
    r = x - hi.astype(F32)
    mid = r.astype(BF16)
    lo = (r - mid.astype(F32)).astype(BF16)
    return hi, mid, lo


def _sigmoid(x):
    return 1.0 / (1.0 + jnp.exp(-x))


def _silu(x):
    return x * _sigmoid(x)


def _seg_mask(n, seg):
    r = lax.broadcasted_iota(jnp.int32, (n, n), 0) // seg
    c = lax.broadcasted_iota(jnp.int32, (n, n), 1) // seg
    return r == c


def _seg_ones(n, seg):
    return _seg_mask(n, seg).astype(BF16)


def _segsum(x, ones_bf16):
    hi, lo = _split2(x)
    return _dot(hi, ones_bf16) + _dot(lo, ones_bf16)


def _tri_incl(n):
    r = lax.broadcasted_iota(jnp.int32, (n, n), 0)
    c = lax.broadcasted_iota(jnp.int32, (n, n), 1)
    return (c <= r).astype(BF16)


def _cumsum_rows(x, tri_bf16):
    hi, mid, lo = _split3(x)
    return _dot(tri_bf16, hi) + _dot(tri_bf16, mid) + _dot(tri_bf16, lo)


def _params(sem, vmem=None):
    return pltpu.CompilerParams(dimension_semantics=sem, vmem_limit_bytes=vmem or VMEM_LIMIT)


def _proj_kernel(x_ref, g_ref, w_ref, o_ref, h_scr):
    @pl.when(pl.program_id(1) == 0)
    def _():
        x = x_ref[...]
        ms = jnp.mean(x * x, axis=-1, keepdims=True)
        h_scr[...] = (x * lax.rsqrt(ms + EPS) * g_ref[...]).astype(BF16)

    o_ref[...] = _dot(h_scr[...], w_ref[...])


def _proj(x2d, g, w_bf16):
    m = x2d.shape[0]
    tm = min(1024, m)
    n = w_bf16.shape[1]
    return pl.pallas_call(
        _proj_kernel,
        grid=(m // tm, n // PROJ_TN),
        in_specs=[pl.BlockSpec((tm, D_MODEL), lambda i, j: (i, 0)),
                  pl.BlockSpec((1, D_MODEL), lambda i, j: (0, 0)),
                  pl.BlockSpec((D_MODEL, PROJ_TN), lambda i, j: (0, j))],
        out_specs=pl.BlockSpec((tm, PROJ_TN), lambda i, j: (i, j)),
        out_shape=jax.ShapeDtypeStruct((m, n), F32),
        scratch_shapes=[pltpu.VMEM((tm, D_MODEL), BF16)],
        compiler_params=_params(("parallel", "arbitrary")),
        name="proj",
    )(x2d, g, w_bf16)


def _qkprep_kernel(q_ref, k_ref, v_ref, qg_ref, kg_ref, c_ref, s1_ref, s2_ref,
                   q16_ref, k32_ref, k16_ref, v16_ref):
    ones = _seg_ones(LANES, A_DQK)
    cosv, sin1, sin2 = c_ref[...], s1_ref[...], s2_ref[...]

    def prep(x, gain):
        ss = _segsum(x * x, ones)
        y = x * lax.rsqrt(ss * (1.0 / A_DQK) + EPS) * gain
        return y * cosv + pltpu.roll(y, LANES - A_ROT // 2, 1) * sin1 + pltpu.roll(y, A_ROT // 2, 1) * sin2

    for c in range(GROUP_W // LANES):
        sl = slice(c * LANES, (c + 1) * LANES)
        q = prep(q_ref[:, sl], qg_ref[...])
        q16_ref[:, sl] = (q * (A_DQK ** -0.5)).astype(BF16)
        k = prep(k_ref[:, sl], kg_ref[...])
        k32_ref[:, sl] = k
        k16_ref[:, sl] = k.astype(BF16)
    v16_ref[...] = v_ref[...].astype(BF16)


def _qkprep(proj, qg128, kg128, cos_t, sin1_t, sin2_t, t_len):
    m = proj.shape[0]
    tm = min(512, t_len)
    nt = t_len // tm
    row = lambda g: pl.BlockSpec((tm, GROUP_W), lambda i, g=g: (i, g))
    tab = pl.BlockSpec((tm, LANES), lambda i: (i % nt, 0))
    vec = pl.BlockSpec((1, LANES), lambda i: (0, 0))
    out = pl.BlockSpec((tm, GROUP_W), lambda i: (i, 0))
    return pl.pallas_call(
        _qkprep_kernel,
        grid=(m // tm,),
        in_specs=[row(G_AQ), row(G_AK), row(G_AV), vec, vec, tab, tab, tab],
        out_specs=[out, out, out, out],
        out_shape=[jax.ShapeDtypeStruct((m, GROUP_W), BF16), jax.ShapeDtypeStruct((m, GROUP_W), F32),
                   jax.ShapeDtypeStruct((m, GROUP_W), BF16), jax.ShapeDtypeStruct((m, GROUP_W), BF16)],
        compiler_params=_params(("parallel",)),
        name="qkprep",
    )(proj, proj, proj, qg128, kg128, cos_t, sin1_t, sin2_t)


def _attn_kernel(qi_tab, kj_tab, lam_ref, q_ref, k_ref, v_ref, g_ref, o_ref,
                 m1, l1, a1, m2, l2, a2, *, tq, tk, q_off, nk, out_scale):
    p = pl.program_id(2)
    qi = qi_tab[p]
    kj = kj_tab[p]
    q_first = q_off + qi * tq
    last_vis = ((q_first + tq - 1) // CHUNK) * CHUNK + CHUNK - 1
    last_kj = jnp.minimum(nk - 1, last_vis // tk)
    full = (kj + 1) * tk - 1 <= (q_first // CHUNK) * CHUNK + CHUNK - 1

    @pl.when(kj == 0)
    def _():
        for m, l, a in ((m1, l1, a1), (m2, l2, a2)):
            m[...] = jnp.full(m.shape, NEG_BIG, F32)
            l[...] = jnp.zeros(l.shape, F32)
            a[...] = jnp.zeros(a.shape, F32)

    def step(masked):
        q = q_ref[0]
        k = k_ref[0]
        v = v_ref[0]
        lane = lax.broadcasted_iota(jnp.int32, q.shape, 1)
        zero = jnp.zeros_like(q)
        if masked:
            qpos = q_first + lax.broadcasted_iota(jnp.int32, (tq, tk), 0)
            kpos = kj * tk + lax.broadcasted_iota(jnp.int32, (tq, tk), 1)
            vis = (kpos // CHUNK) <= (qpos // CHUNK)
        for half, (m, l, a) in enumerate(((m1, l1, a1), (m2, l2, a2))):
            qh = jnp.where(lane >= A_DQK if half else lane < A_DQK, q, zero)
            s = _dot_nt(qh, k)
            if masked:
                s = jnp.where(vis, s, NEG_BIG)
            m_prev = m[...]
            m_new = jnp.maximum(m_prev, jnp.max(s, axis=1, keepdims=True))
            pr = jnp.exp(s - m_new[:, :1])
            alpha = jnp.exp(m_prev - m_new)
            l[...] = alpha * l[...] + jnp.sum(pr, axis=1, keepdims=True)
            a[...] = alpha * a[...] + _dot(pr.astype(BF16), v)
            m[...] = m_new

    @pl.when(full)
    def _():
        step(False)

    @pl.when(jnp.logical_not(full))
    def _():
        step(True)

    @pl.when(kj == last_kj)
    def _():
        lam = lam_ref[0]
        o = a1[...] / l1[...] - lam * (a2[...] / l2[...])
        ms = jnp.mean(o * o, axis=-1, keepdims=True)
        o_ref[0] = o * lax.rsqrt(ms + EPS) * g_ref[...] * out_scale


def _attn(q16, k16, v16, lam, subln_g, q_off, out_scale):
    b, tq_len, _ = q16.shape
    tk_len = k16.shape[1]
    tq = min(512, tq_len)
    tk = tq if tk_len > 2048 else tk_len
    assert tq_len % tq == 0 and tk_len % tk == 0
    nq, nk = tq_len // tq, tk_len // tk
    pairs = []
    for qi in range(nq):
        last_vis = ((q_off + (qi + 1) * tq - 1) // CHUNK) * CHUNK + CHUNK - 1
        for kj in range(min(nk - 1, last_vis // tk) + 1):
            pairs.append((qi, kj))
    qi_tab = jnp.asarray([p[0] for p in pairs], jnp.int32)
    kj_tab = jnp.asarray([p[1] for p in pairs], jnp.int32)
    kern = functools.partial(_attn_kernel, tq=tq, tk=tk, q_off=q_off, nk=nk, out_scale=out_scale)
    acc = pltpu.VMEM((tq, LANES), F32)
    grid_spec = pltpu.PrefetchScalarGridSpec(
        num_scalar_prefetch=2,
        grid=(b, A_HEADS, len(pairs)),
        in_specs=[pl.BlockSpec(memory_space=pltpu.SMEM),
                  pl.BlockSpec((1, tq, LANES), lambda bi, h, p, qt, kt: (bi, qt[p], h)),
                  pl.BlockSpec((1, tk, LANES), lambda bi, h, p, qt, kt: (bi, kt[p], h)),
                  pl.BlockSpec((1, tk, LANES), lambda bi, h, p, qt, kt: (bi, kt[p], h)),
                  pl.BlockSpec((1, LANES), lambda bi, h, p, qt, kt: (0, 0))],
        out_specs=pl.BlockSpec((1, tq, LANES), lambda bi, h, p, qt, kt: (bi, qt[p], h)),
        scratch_shapes=[acc, acc, acc, acc, acc, acc])
    return pl.pallas_call(
        kern,
        grid_spec=grid_spec,
        out_shape=jax.ShapeDtypeStruct((b, tq_len, GROUP_W), F32),
        compiler_params=_params(("parallel", "parallel", "arbitrary")),
        name="attn",
    )(qi_tab, kj_tab, lam, q16, k16, v16, subln_g)


def _hgrn_kernel(q_ref, f_ref, i_ref, lb_ref, g_ref, s0_ref, o_ref, sfin_ref,
                 st_scr, cum_scr, qk_scr, oi_scr):
    c = pl.program_id(1)

    @pl.when(c == 0)
    def _():
        st_scr[...] = s0_ref[0]

    z = f_ref[...]
    lb = lb_ref[...]
    log_f = jnp.log(lb + (1.0 - lb) * _sigmoid(z))
    k_in = (1.0 - lb) * _sigmoid(-z)
    q = _silu(q_ref[...])
    cum = _cumsum_rows(log_f, _tri_incl(CHUNK))
    cum_scr[...] = cum
    qk_scr[...] = q
    row = lax.broadcasted_iota(jnp.int32, (CHUNK, 1), 0)

    for h in range(B_HEADS):
        sl = slice(h * LANES, (h + 1) * LANES)
        cum_h = cum[:, sl]
        kin_h = k_in[:, sl]
        i_h = i_ref[:, sl]

        def body(g, carry, sl=sl, cum_h=cum_h, kin_h=kin_h, i_h=i_h):
            g8 = pl.multiple_of(g * SUBLANES, SUBLANES)
            c_tile = cum_scr[pl.ds(g8, SUBLANES), sl]
            q_tile = qk_scr[pl.ds(g8, SUBLANES), sl]
            rows = []
            for r in range(SUBLANES):
                dec = jnp.exp(jnp.minimum(c_tile[r:r + 1] - cum_h, 0.0))
                col = jnp.sum(dec * (kin_h * q_tile[r:r + 1]), axis=1, keepdims=True)
                col = jnp.where(row <= g8 + r, col, 0.0)
                rows.append(jnp.sum(col * i_h, axis=0, keepdims=True))
            oi_scr[pl.ds(g8, SUBLANES), sl] = jnp.concatenate(rows, axis=0)
            return carry

        lax.fori_loop(0, CHUNK // SUBLANES, body, 0)

        st = st_scr[h]
        qe = (q[:, sl] * jnp.exp(cum_h)).astype(BF16)
        o = oi_scr[:, sl] + _dot_nt(qe, st.astype(BF16))
        cum_last = cum_h[CHUNK - 1:CHUNK, :]
        kt = (kin_h * jnp.exp(cum_last - cum_h)).astype(BF16)
        st_scr[h] = st * jnp.exp(cum_last) + _dot_tn(i_h.astype(BF16), kt)
        ms = jnp.mean(o * o, axis=-1, keepdims=True)
        o_ref[:, sl] = o * lax.rsqrt(ms + EPS) * g_ref[...]

    sfin_ref[0] = st_scr[...]


def _hgrn(proj, lb, norm_g128, s0_t, bsz, nc):
    m = proj.shape[0]
    row = lambda g: pl.BlockSpec((CHUNK, GROUP_W), lambda b, c, g=g: (b * nc + c, g))
    st_spec = pl.BlockSpec((1, B_HEADS, LANES, LANES), lambda b, c: (b, 0, 0, 0))
    return pl.pallas_call(
        _hgrn_kernel,
        grid=(bsz, nc),
        in_specs=[row(G_BQ), row(G_BF), row(G_BI),
                  pl.BlockSpec((1, GROUP_W), lambda b, c: (0, 0)),
                  pl.BlockSpec((1, LANES), lambda b, c: (0, 0)),
                  st_spec],
        out_specs=[pl.BlockSpec((CHUNK, GROUP_W), lambda b, c: (b * nc + c, 0)), st_spec],
        out_shape=[jax.ShapeDtypeStruct((m, GROUP_W), F32),
                   jax.ShapeDtypeStruct((bsz, B_HEADS, LANES, LANES), F32)],
        scratch_shapes=[pltpu.VMEM((B_HEADS, LANES, LANES), F32),
                        pltpu.VMEM((CHUNK, GROUP_W), F32),
                        pltpu.VMEM((CHUNK, GROUP_W), F32),
                        pltpu.VMEM((CHUNK, GROUP_W), F32)],
        compiler_params=_params(("parallel", "arbitrary")),
        name="hgrn",
    )(proj, proj, proj, lb, norm_g128, s0_t)


def _rwkprep_kernel(*refs, has_vres):
    (cr_ref, ck_ref, cv_ref, clo_ref, spr_ref, spk_ref, spv_ref, splo_ref,
     mur_ref, muk_ref, muv_ref, mulo_ref, w0_ref, w2h_ref, w2l_ref, a0_ref, a2h_ref, a2l_ref,
     kk_ref, ka_ref, rk_ref) = refs[:21]
    pos = 21
    if has_vres:
        vres_ref, vf_ref, v0_ref, vw2h_ref, vw2l_ref = refs[pos:pos + 5]
        pos += 5
    (at_ref, rt_ref, bh_ref, kh_ref, vc_ref, bonus_ref, gam_ref, nab_ref, arb_ref, g_ref, y0_ref,
     shr_ref, shk_ref, shv_ref, shlo_ref) = refs[pos:pos + 15]
    pr_scr, pk_scr, pv_scr, plo_scr = refs[pos + 15:]
    c = pl.program_id(1)

    @pl.when(c == 0)
    def _():
        pr_scr[...] = spr_ref[0]
        pk_scr[...] = spk_ref[0]
        pv_scr[...] = spv_ref[0]
        plo_scr[...] = splo_ref[0]

    def shifted(x_ref, prev_scr, mu_ref, last_ref):
        x = x_ref[...]
        row = lax.broadcasted_iota(jnp.int32, x.shape, 0)
        prev = jnp.where(row == 0, prev_scr[...], pltpu.roll(x, 1, 0))
        last = x[CHUNK - 1:CHUNK, :]
        prev_scr[...] = last
        last_ref[0] = last
        return x + (prev - x) * mu_ref[...]

    r = shifted(cr_ref, pr_scr, mur_ref, shr_ref)
    k0 = shifted(ck_ref, pk_scr, muk_ref, shk_ref)
    v = shifted(cv_ref, pv_scr, muv_ref, shv_ref)
    lo = shifted(clo_ref, plo_scr, mulo_ref, shlo_ref)

    def lowrank(x, wh_ref, wl_ref):
        xh, xl = _split2(x)
        return _dot(xh, wh_ref[...]) + _dot(xl, wh_ref[...]) + _dot(xh, wl_ref[...])

    w_in = w0_ref[...] + lowrank(jnp.tanh(lo), w2h_ref, w2l_ref)
    nw = -w_in
    softplus = jnp.maximum(nw, 0.0) + jnp.log(1.0 + jnp.exp(-jnp.abs(nw)))
    log_decay = -jnp.exp(-softplus - 0.5)
    a_sig = _sigmoid(a0_ref[...] + lowrank(lo, a2h_ref, a2l_ref))
    if has_vres:
        v_mix = _sigmoid(v0_ref[...] + lowrank(vres_ref[...], vw2h_ref, vw2l_ref))
        v = v + (vf_ref[...] - v) * v_mix
    vc_ref[...] = v

    ones = _seg_ones(LANES, C_DH)
    cum = _cumsum_rows(log_decay, _tri_incl(CHUNK))
    cum_last = cum[CHUNK - 1:CHUNK, :]
    gam_ref[0] = jnp.exp(cum_last)
    e_prev = jnp.exp(cum - log_decay)
    e_cum = jnp.exp(cum)
    e_inv = jnp.exp(-cum)
    e_tail = jnp.exp(cum_last - cum)

    lane = lax.broadcasted_iota(jnp.int32, (CHUNK, LANES), 1)
    par1 = lane >= C_DH
    tcol = lax.broadcasted_iota(jnp.int32, (CHUNK, LANES), 0)
    scol = lane % C_DH
    strict = jnp.concatenate([scol < tcol, scol < tcol], axis=1)
    incl = jnp.concatenate([scol <= tcol, scol <= tcol], axis=1)
    low_mask = jnp.concatenate([strict, incl], axis=0)

    for p in range(C_WIDTH // LANES):
        sl = slice(p * LANES, (p + 1) * LANES)
        kk = k0[:, sl] * kk_ref[:, sl]
        n2 = _segsum(kk * kk, ones)
        kk = kk / jnp.maximum(jnp.sqrt(n2), 1e-12)
        a_p = a_sig[:, sl]
        k = k0[:, sl] * (1.0 + (a_p - 1.0) * ka_ref[:, sl])
        r_p = r[:, sl]
        v_p = v[:, sl]
        b_vec = kk * a_p
        bonus_ref[:, sl] = _segsum(r_p * k * rk_ref[:, sl], ones) * v_p
        a_t = -kk * e_prev[:, sl]
        r_t = r_p * e_cum[:, sl]
        b_t = b_vec * e_inv[:, sl]
        k_t = k * e_inv[:, sl]
        at_ref[:, sl] = a_t
        rt_ref[:, sl] = r_t
        bh_ref[:, sl] = b_vec * e_tail[:, sl]
        kh_ref[:, sl] = k * e_tail[:, sl]
        zero = jnp.zeros_like(b_t)
        lhs = jnp.concatenate([a_t, r_t], axis=0).astype(BF16)
        rhs = jnp.concatenate([jnp.where(par1, zero, b_t), jnp.where(par1, b_t, zero),
                               jnp.where(par1, zero, k_t), jnp.where(par1, k_t, zero)], axis=0).astype(BF16)
        prod = jnp.where(low_mask, _dot_nt(lhs, rhs), 0.0)
        nab_ref[:, sl] = prod[:CHUNK, :LANES]
        arb_ref[:, sl] = prod[CHUNK:, :LANES]
        vbd = jnp.concatenate([jnp.where(par1, zero, v_p), jnp.where(par1, v_p, zero)], axis=0).astype(BF16)
        gy = _dot(prod[:, LANES:].astype(BF16), vbd)
        g_ref[:, sl] = gy[:CHUNK]
        y0_ref[:, sl] = gy[CHUNK:]


def _rwkprep(proj, shift_parts, params, vres, bsz, nc):
    m = proj.shape[0]
    has_vres = vres is not None
    row = lambda g: pl.BlockSpec((CHUNK, GROUP_W), lambda b, c, g=g: (b * nc + c, g))
    lo_spec = pl.BlockSpec((CHUNK, LANES), lambda b, c: (b * nc + c, COL_LO // LANES))
    st = lambda w: pl.BlockSpec((1, 1, w), lambda b, c: (b, 0, 0))
    vec = lambda w: pl.BlockSpec((1, w), lambda b, c: (0, 0))
    mat = lambda: pl.BlockSpec((LANES, GROUP_W), lambda b, c: (0, 0))
    out = pl.BlockSpec((CHUNK, GROUP_W), lambda b, c: (b * nc + c, 0))
    in_specs = [row(G_CR), row(G_CK), row(G_CV), lo_spec, st(GROUP_W), st(GROUP_W), st(GROUP_W), st(LANES),
                vec(GROUP_W), vec(GROUP_W), vec(GROUP_W), vec(LANES),
                vec(GROUP_W), mat(), mat(), vec(GROUP_W), mat(), mat(),
                vec(GROUP_W), vec(GROUP_W), vec(GROUP_W)]
    args = [proj, proj, proj, proj, *shift_parts,
            params["mu_r"], params["mu_k"], params["mu_v"], params["mu_lo"],
            params["w0"], params["w2h"], params["w2l"], params["a0"], params["a2h"], params["a2l"],
            params["k_k"], params["k_a"], params["r_k"]]
    if has_vres:
        in_specs += [pl.BlockSpec((CHUNK, LANES), lambda b, c: (b * nc + c, COL_VRES // LANES)),
                     out, vec(GROUP_W), mat(), mat()]
        args += [proj, vres["v_first"], vres["v0"], vres["w2h"], vres["w2l"]]
    big = jax.ShapeDtypeStruct((m, GROUP_W), F32)
    out_shape = [big] * 6 + [jax.ShapeDtypeStruct((bsz * nc, 1, GROUP_W), F32)] + [big] * 4 + [
        jax.ShapeDtypeStruct((bsz, 1, GROUP_W), F32)] * 3 + [jax.ShapeDtypeStruct((bsz, 1, LANES), F32)]
    out_specs = [out] * 6 + [pl.BlockSpec((1, 1, GROUP_W), lambda b, c: (b * nc + c, 0, 0))] + [out] * 4 + [
        st(GROUP_W)] * 3 + [st(LANES)]
    return pl.pallas_call(
        functools.partial(_rwkprep_kernel, has_vres=has_vres),
        grid=(bsz, nc),
        in_specs=in_specs,
        out_specs=out_specs,
        out_shape=out_shape,
        scratch_shapes=[pltpu.VMEM((1, GROUP_W), F32)] * 3 + [pltpu.VMEM((1, LANES), F32)],
        compiler_params=_params(("parallel", "arbitrary")),
        name="rwkprep",
    )(*args)


def _rwksolve_kernel(n_ref, r_ref, o_ref):
    def group_body(tg, carry):
        t0 = pl.multiple_of(tg * SUBLANES, SUBLANES)
        for r in range(SUBLANES):
            t = t0 + r

            def s_body(sg, acc, t=t):
                s0 = pl.multiple_of(sg * SUBLANES, SUBLANES)
                coef = n_ref[t, pl.ds(s0, SUBLANES), :]
                for q in range(SUBLANES):
                    acc = acc + coef[q:q + 1] * o_ref[s0 + q]
                return acc

            acc = lax.fori_loop(0, tg, s_body, r_ref[t])
            coef = n_ref[t, pl.ds(t0, SUBLANES), :]
            for q in range(r):
                acc = acc + coef[q:q + 1] * o_ref[t0 + q]
            o_ref[t] = acc
        return carry

    lax.fori_loop(0, CHUNK // SUBLANES, group_body, 0)


def _rwksolve(n_bl, r_bl):
    n_inst = n_bl.shape[-1]
    return pl.pallas_call(
        _rwksolve_kernel,
        grid=(n_inst // LANES,),
        in_specs=[pl.BlockSpec((CHUNK, CHUNK, LANES), lambda i: (0, 0, i)),
                  pl.BlockSpec((CHUNK, LANES, LANES), lambda i: (0, 0, i))],
        out_specs=pl.BlockSpec((CHUNK, LANES, LANES), lambda i: (0, 0, i)),
        out_shape=jax.ShapeDtypeStruct(r_bl.shape, F32),
        compiler_params=_params(("parallel",)),
        name="rwksolve",
    )(n_bl, r_bl)


def _rwkseq_kernel(w_ref, u0_ref, rt_ref, arb_ref, y0_ref, v_ref, bh_ref, kh_ref, gam_ref, bonus_ref,
                   lnw_ref, lnb_ref, s0_ref, o_ref, sfin_ref, st_scr):
    c = pl.program_id(1)

    @pl.when(c == 0)
    def _():
        st_scr[...] = s0_ref[0]

    ones = _seg_ones(LANES, C_DH)
    lane = lax.broadcasted_iota(jnp.int32, (CHUNK, LANES), 1)
    par1 = lane >= C_DH
    bd = _seg_mask(LANES, C_DH)
    for p in range(C_WIDTH // LANES):
        sl = slice(p * LANES, (p + 1) * LANES)
        st = st_scr[p]
        st16 = st.astype(BF16)
        u = _dot_nt(w_ref[:, sl].astype(BF16), st16) + u0_ref[:, sl]
        zero = jnp.zeros_like(u)
        ubd = jnp.concatenate([jnp.where(par1, zero, u), jnp.where(par1, u, zero)], axis=0).astype(BF16)
        y = (_dot_nt(rt_ref[:, sl].astype(BF16), st16) + _dot(arb_ref[:, sl].astype(BF16), ubd)
             + y0_ref[:, sl])
        uv = jnp.concatenate([u, v_ref[:, sl]], axis=0).astype(BF16)
        bk = jnp.concatenate([bh_ref[:, sl], kh_ref[:, sl]], axis=0).astype(BF16)
        st_scr[p] = st * gam_ref[0, :, sl] + jnp.where(bd, _dot_tn(uv, bk), 0.0)
        mu = _segsum(y, ones) * (1.0 / C_DH)
        d = y - mu
        var = _segsum(d * d, ones) * (1.0 / C_DH)
        o_ref[:, sl] = d * lax.rsqrt(var + C_GN_EPS) * lnw_ref[:, sl] + lnb_ref[:, sl] + bonus_ref[:, sl]

    sfin_ref[0] = st_scr[...]


def _rwkseq(w, u0, rt, arb, y0, v, bh, kh, gam, bonus, ln_w, ln_b, s0_bd, bsz, nc):
    m = w.shape[0]
    row = pl.BlockSpec((CHUNK, GROUP_W), lambda b, c: (b * nc + c, 0))
    vec = pl.BlockSpec((1, GROUP_W), lambda b, c: (0, 0))
    st_spec = pl.BlockSpec((1, C_WIDTH // LANES, LANES, LANES), lambda b, c: (b, 0, 0, 0))
    return pl.pallas_call(
        _rwkseq_kernel,
        grid=(bsz, nc),
        in_specs=[row] * 8 + [pl.BlockSpec((1, 1, GROUP_W), lambda b, c: (b * nc + c, 0, 0)), row, vec, vec,
                  st_spec],
        out_specs=[row, st_spec],
        out_shape=[jax.ShapeDtypeStruct((m, GROUP_W), F32),
                   jax.ShapeDtypeStruct((bsz, C_WIDTH // LANES, LANES, LANES), F32)],
        scratch_shapes=[pltpu.VMEM((C_WIDTH // LANES, LANES, LANES), F32)],
        compiler_params=_params(("parallel", "arbitrary")),
        name="rwkseq",
    )(w, u0, rt, arb, y0, v, bh, kh, gam, bonus, ln_w, ln_b, s0_bd)


def _merge_kernel(x_ref, oa_ref, ob_ref, oc_ref, ag_ref, bg_ref, cg_ref, ma_ref, mb_ref, mc_ref,
                  wa_ref, wb_ref, wc_ref, wo_ref, y_ref):
    def branch(o_ref, gate_ref, w_ref):
        return _dot((o_ref[...] * _silu(gate_ref[...])).astype(BF16), w_ref[...])

    merged = (_sigmoid(ma_ref[...]) * branch(oa_ref, ag_ref, wa_ref)
              + _sigmoid(mb_ref[...]) * branch(ob_ref, bg_ref, wb_ref)
              + _sigmoid(mc_ref[...]) * branch(oc_ref, cg_ref, wc_ref))
    y_ref[...] = x_ref[...] + _dot(merged.astype(BF16), wo_ref[...])


def _merge(x2d, o_a, o_b, o_c, proj, wa, wb, wc, wo):
    m = x2d.shape[0]
    tm = min(256, m)
    row = pl.BlockSpec((tm, GROUP_W), lambda i: (i, 0))
    grp = lambda g: pl.BlockSpec((tm, GROUP_W), lambda i, g=g: (i, g))
    wsp = pl.BlockSpec((GROUP_W, D_MODEL), lambda i: (0, 0))
    return pl.pallas_call(
        _merge_kernel,
        grid=(m // tm,),
        in_specs=[row, row, row, row, grp(G_AG), grp(G_BG), grp(G_CG), grp(G_MA), grp(G_MB), grp(G_MC),
                  wsp, wsp, wsp, wsp],
        out_specs=row,
        out_shape=jax.ShapeDtypeStruct((m, D_MODEL), F32),
        compiler_params=_params(("parallel",)),
        name="merge",
    )(x2d, o_a, o_b, o_c, proj, proj, proj, proj, proj, proj, wa, wb, wc, wo)


_C_OFF = 8 * GROUP_W
_CP_R = (_C_OFF, _C_OFF + C_WIDTH)
_CP_WLO = (_CP_R[1], _CP_R[1] + C_DECAY_RANK)
_CP_K = (_CP_WLO[1], _CP_WLO[1] + C_WIDTH)
_CP_V = (_CP_K[1], _CP_K[1] + C_WIDTH)
_CP_ALO = (_CP_V[1], _CP_V[1] + C_A_RANK)
_REST = _CP_ALO[1]


def _regroup_w_in(w, vres_w1):
    cols = [w[:, :_C_OFF], w[:, _CP_R[0]:_CP_R[1]], w[:, _CP_K[0]:_CP_K[1]], w[:, _CP_V[0]:_CP_V[1]],
            w[:, _REST:], w[:, _CP_WLO[0]:_CP_WLO[1]], w[:, _CP_ALO[0]:_CP_ALO[1]]]
    vres = jnp.zeros((D_MODEL, LANES), F32)
    if vres_w1 is not None:
        vres = vres.at[:, :C_VRES_RANK].set(vres_w1)
    cols.append(vres)
    cols.append(jnp.zeros((D_MODEL, N_PROJ_PAD - N_PROJ), F32))
    return jnp.concatenate(cols, axis=1).astype(BF16)


def _split_shift(s):
    o = _C_OFF
    part = lambda a: s[:, a[0] - o:a[1] - o]
    lo = jnp.concatenate([part(_CP_WLO), part(_CP_ALO)], axis=1)
    return [x[:, None, :] for x in (part(_CP_R), part(_CP_K), part(_CP_V), lo)]


def _join_shift(r, k, v, lo):
    r, k, v, lo = (x[:, 0, :] for x in (r, k, v, lo))
    return jnp.concatenate([r, lo[:, :C_DECAY_RANK], k, v, lo[:, C_DECAY_RANK:]], axis=1)


def _pad_rows(w, row0):
    out = jnp.zeros((LANES, w.shape[1]), F32).at[row0:row0 + w.shape[0]].set(w)
    hi = out.astype(BF16)
    return hi, (out - hi.astype(F32)).astype(BF16)


def _rope_tables(pos):
    half = A_ROT // 2
    inv_freq = ROPE_THETA ** (-(jnp.arange(half, dtype=F32) * (2.0 / A_ROT)))
    ang = pos.astype(F32)[:, None] * inv_freq[None, :]
    cos, sin = jnp.cos(ang), jnp.sin(ang)
    t = pos.shape[0]
    one = jnp.ones((t, A_DQK - A_ROT), F32)
    zero = jnp.zeros((t, A_DQK - A_ROT), F32)
    z8 = jnp.zeros((t, half), F32)
    c64 = jnp.concatenate([cos, cos, one], axis=1)
    s1 = jnp.concatenate([-sin, z8, zero], axis=1)
    s2 = jnp.concatenate([z8, sin, zero], axis=1)
    tile = lambda x: jnp.concatenate([x, x], axis=1)
    return tile(c64), tile(s1), tile(s2)


def _to_lanes(x, rows):
    return x.reshape(rows, CHUNK, C_HEADS, C_DH).transpose(1, 3, 0, 2).reshape(CHUNK, C_DH, rows * C_HEADS)


def _from_lanes(x, rows):
    return x.reshape(CHUNK, C_DH, rows, C_HEADS).transpose(2, 0, 3, 1).reshape(rows * CHUNK, C_WIDTH)


def _layer(l, x2d, bsz, t_len, q_off, P, lb, past, v_first):
    nc = t_len // CHUNK
    m = bsz * t_len
    w_in = _regroup_w_in(P["w_in"][l], P["c_vres_w1"][l - 1] if l > 0 else None)
    proj = _proj(x2d, P["norm_g"][l][None, :], w_in)

    pos = q_off + jnp.arange(t_len, dtype=jnp.int32)
    cos_t, sin1_t, sin2_t = _rope_tables(pos)
    tile2 = lambda g: jnp.concatenate([g, g])[None, :]
    q16, k32, k16, v16 = _qkprep(proj, tile2(P["a_qnorm_g"][l]), tile2(P["a_knorm_g"][l]),
                                 cos_t, sin1_t, sin2_t, t_len)
    v32 = proj[:, G_AV * GROUP_W:(G_AV + 1) * GROUP_W]
    q16 = q16.reshape(bsz, t_len, GROUP_W)
    k16 = k16.reshape(bsz, t_len, GROUP_W)
    v16 = v16.reshape(bsz, t_len, GROUP_W)
    if past is not None:
        pk, pv = past[0][l], past[1][l]
        p_len = pk.shape[1]
        k16 = jnp.concatenate([pk.reshape(bsz, p_len, GROUP_W).astype(BF16), k16], axis=1)
        v16 = jnp.concatenate([pv.reshape(bsz, p_len, GROUP_W).astype(BF16), v16], axis=1)
    lam_init = 0.8 - 0.6 * math.exp(-0.3 * l)
    lp = P["a_lambda"][l].astype(F32)
    lam = jnp.exp(jnp.sum(lp[0] * lp[1])) - jnp.exp(jnp.sum(lp[2] * lp[3])) + lam_init
    o_a = _attn(q16, k16, v16, lam.reshape(1), P["a_subln_g"][l][None, :], q_off, 1.0 - lam_init)
    o_a = o_a.reshape(m, GROUP_W)

    if past is None:
        s_h = jnp.zeros((bsz, B_HEADS, LANES, LANES), F32)
    else:
        s_h = jnp.swapaxes(past[2][l].astype(F32), -1, -2)
    o_b, s_h_new = _hgrn(proj, lb[l][None, :], P["b_norm_g"][l][None, :], s_h, bsz, nc)
    s_h_new = jnp.swapaxes(s_h_new, -1, -2)

    if past is None:
        shift_prev = jnp.zeros((bsz, 3 * C_WIDTH + C_DECAY_RANK + C_A_RANK), F32)
        s_r = jnp.zeros((bsz, C_HEADS, C_DH, C_DH), F32)
    else:
        shift_prev, s_r = past[4][l], past[3][l].astype(F32)
    mu = _split_shift(P["c_shift_mu"][l][None, :])
    w2h, w2l = _pad_rows(P["c_w2"][l], 0)
    a2h, a2l = _pad_rows(P["c_a2"][l], C_DECAY_RANK)
    cparams = {"mu_r": mu[0][0], "mu_k": mu[1][0], "mu_v": mu[2][0], "mu_lo": mu[3][0],
               "w0": P["c_w0"][l][None, :], "w2h": w2h, "w2l": w2l,
               "a0": P["c_a0"][l][None, :], "a2h": a2h, "a2l": a2l,
               "k_k": P["c_k_k"][l][None, :], "k_a": P["c_k_a"][l][None, :],
               "r_k": P["c_r_k"][l].reshape(1, C_WIDTH)}
    vres = None
    if l > 0:
        vh, vl = _pad_rows(P["c_vres_w2"][l - 1], 0)
        vres = {"v_first": v_first, "v0": P["c_v0"][l - 1][None, :], "w2h": vh, "w2l": vl}
    (a_t, r_t, b_h, k_h, v_c, bonus, gam, nab, arb, g_mat, y0, sh_r, sh_k, sh_v, sh_lo) = _rwkprep(
        proj, _split_shift(shift_prev), cparams, vres, bsz, nc)
    rows = bsz * nc
    n_inst = rows * C_HEADS
    pad = (-n_inst) % LANES
    n_bl = _to_lanes(nab, rows)
    r_bl = jnp.concatenate([_to_lanes(a_t, rows), _to_lanes(g_mat, rows)], axis=1)
    if pad:
        n_bl = jnp.pad(n_bl, ((0, 0), (0, 0), (0, pad)))
        r_bl = jnp.pad(r_bl, ((0, 0), (0, 0), (0, pad)))
    x_bl = _rwksolve(n_bl, r_bl)[:, :, :n_inst]
    w_mat = _from_lanes(x_bl[:, :C_DH], rows)
    u0 = _from_lanes(x_bl[:, C_DH:], rows)
    eye2 = jnp.eye(2, dtype=F32)
    s_bd = jnp.einsum("bpqvk,qr->bpqvrk", s_r.reshape(bsz, C_HEADS // 2, 2, C_DH, C_DH), eye2)
    s_bd = s_bd.reshape(bsz, C_HEADS // 2, LANES, LANES)
    o_c, s_bd_new = _rwkseq(w_mat, u0, r_t, arb, y0, v_c, b_h, k_h, gam, bonus,
                            P["c_ln_w"][l][None, :], P["c_ln_b"][l][None, :], s_bd, bsz, nc)
    s_new6 = s_bd_new.reshape(bsz, C_HEADS // 2, 2, C_DH, 2, C_DH)
    s_r_new = jnp.stack([s_new6[:, :, 0, :, 0, :], s_new6[:, :, 1, :, 1, :]], axis=2)
    s_r_new = s_r_new.reshape(bsz, C_HEADS, C_DH, C_DH)
    shift_new = _join_shift(sh_r, sh_k, sh_v, sh_lo)

    bf = lambda w: w.astype(BF16)
    y = _merge(x2d, o_a, o_b, o_c, proj, bf(P["w_out_a"][l]), bf(P["w_out_b"][l]), bf(P["w_out_c"][l]),
               bf(P["w_o"][l]))
    k_rows = k32.reshape(bsz, t_len, A_HEADS, 2 * A_DQK)
    v_rows = v32.reshape(bsz, t_len, A_HEADS, 2 * A_DQK)
    return y, (k_rows, v_rows, s_h_new, s_r_new, shift_new), v_c


def _run_trunk(x, q_off, P, lb, past):
    bsz, t_len, _ = x.shape
    depth = P["w_in"].shape[0]
    x2d = x.reshape(bsz * t_len, D_MODEL)
    outs = ([], [], [], [], [])
    v_first = None
    for l in range(depth):
        x2d, entries, v_c = _layer(l, x2d, bsz, t_len, q_off, P, lb, past, v_first)
        if l == 0:
            v_first = v_c
        for lst, e in zip(outs, entries):
            lst.append(e)
    return x2d.reshape(bsz, t_len, D_MODEL), [jnp.stack(lst) for lst in outs]


def kernel(x_prompt, x_sample, cache_attn_k, cache_attn_v, state_hgrn, state_rwkv, state_rwkv_shift,
           norm_g, w_in, a_qnorm_g, a_knorm_g, a_lambda, a_subln_g, b_lower, b_norm_g,
           c_shift_mu, c_w0, c_w2, c_a0, c_a2, c_k_k, c_k_a, c_r_k, c_ln_w, c_ln_b,
           c_vres_w1, c_vres_w2, c_v0, w_out_a, w_out_b, w_out_c, w_o):
    P = {"norm_g": norm_g, "w_in": w_in, "a_qnorm_g": a_qnorm_g, "a_knorm_g": a_knorm_g,
         "a_lambda": a_lambda, "a_subln_g": a_subln_g, "b_norm_g": b_norm_g,
         "c_shift_mu": c_shift_mu, "c_w0": c_w0, "c_w2": c_w2, "c_a0": c_a0, "c_a2": c_a2,
         "c_k_k": c_k_k, "c_k_a": c_k_a, "c_r_k": c_r_k, "c_ln_w": c_ln_w, "c_ln_b": c_ln_b,
         "c_vres_w1": c_vres_w1, "c_vres_w2": c_vres_w2, "c_v0": c_v0,
         "w_out_a": w_out_a, "w_out_b": w_out_b, "w_out_c": w_out_c, "w_o": w_o}
    sm = jax.nn.softmax(b_lower.astype(F32), axis=0)
    lb = jnp.cumsum(sm, axis=0) - sm[0:1]
    past_len = cache_attn_k.shape[2]
    y_p, (k_p, v_p, hg_p, rw_p, sh_p) = _run_trunk(x_prompt, 0, P, lb, None)
    y_s, (k_s, v_s, hg_s, rw_s, sh_s) = _run_trunk(
        x_sample, past_len, P, lb, (cache_attn_k, cache_attn_v, state_hgrn, state_rwkv, state_rwkv_shift))
    return (y_p, y_s, k_p, v_p, hg_p, rw_p, sh_p, k_s, v_s, hg_s, rw_s, sh_s)
```

```python
import functools
import math

import jax
import jax.numpy as jnp
from jax import lax
from jax.experimental import pallas as pl
from jax.experimental.pallas import tpu as pltpu

F32 = jnp.float32
BF16 = jnp.bfloat16

D_MODEL = 1024
CHUNK = 64
EPS = 1e-6
NEG_BIG = -1e30
LOG2_E = 1.4426950408889634
A_HEADS = 8
A_DQK = 64
A_ROT = 16
ROPE_THETA = 500000.0
B_HEADS = 8
C_HEADS = 16
C_DH = 64
C_WIDTH = 1024
C_DECAY_RANK = 64
C_A_RANK = 64
C_VRES_RANK = 32
C_GN_EPS = 64e-5
HGRN_SAFE_SPAN = 60.0
LANES = 128
SUBLANES = 8
GROUP_W = 1024
N_GROUPS = 15
COL_LO = N_GROUPS * GROUP_W
COL_VRES = COL_LO + LANES
N_PROJ = COL_VRES + LANES
PROJ_TN = 512
N_PROJ_PAD = -(-N_PROJ // PROJ_TN) * PROJ_TN
(G_AQ, G_AK, G_AV, G_AG, G_BQ, G_BF, G_BI, G_BG, G_CR, G_CK, G_CV, G_CG, G_MA, G_MB, G_MC) = range(N_GROUPS)
VMEM_LIMIT = 56 * 1024 * 1024


def _dot(a, b):
    return jnp.dot(a, b, preferred_element_type=F32)


def _dot_nt(a, b):
    return lax.dot_general(a, b, (((1,), (1,)), ((), ())), preferred_element_type=F32)


def _dot_tn(a, b):
    return lax.dot_general(a, b, (((0,), (0,)), ((), ())), preferred_element_type=F32)


def _split2(x):
    hi = x.astype(BF16)
    lo = (x - hi.astype(F32)).astype(BF16)
    return hi, lo


def _split3(x):
    hi = x.astype(BF16)
    r = x - hi.astype(F32)
    mid = r.astype(BF16)
    lo = (r - mid.astype(F32)).astype(BF16)
    return hi, mid, lo


def _sigmoid(x):
    return 1.0 / (1.0 + jnp.exp(-x))


def _silu(x):
    return x * _sigmoid(x)


def _seg_mask(n, seg):
    r = lax.broadcasted_iota(jnp.int32, (n, n), 0) // seg
    c = lax.broadcasted_iota(jnp.int32, (n, n), 1) // seg
    return r == c


def _seg_ones(n, seg):
    return _seg_mask(n, seg).astype(BF16)


def _segsum(x, ones_bf16):
    hi, lo = _split2(x)
    return _dot(hi, ones_bf16) + _dot(lo, ones_bf16)


def _tri_incl(n):
    r = lax.broadcasted_iota(jnp.int32, (n, n), 0)
    c = lax.broadcasted_iota(jnp.int32, (n, n), 1)
    return (c <= r).astype(BF16)


def _cumsum_rows(x, tri_bf16):
    hi, mid, lo = _split3(x)
    return _dot(tri_bf16, hi) + _dot(tri_bf16, mid) + _dot(tri_bf16, lo)


def _params(sem, vmem=None):
    return pltpu.CompilerParams(dimension_semantics=sem, vmem_limit_bytes=vmem or VMEM_LIMIT)


def _proj_kernel(x_ref, g_ref, w_ref, o_ref, h_scr):
    @pl.when(pl.program_id(1) == 0)
    def _():
        x = x_ref[...]
        ms = jnp.mean(x * x, axis=-1, keepdims=True)
        h_scr[...] = (x * lax.rsqrt(ms + EPS) * g_ref[...]).astype(BF16)

    o_ref[...] = _dot(h_scr[...], w_ref[...])


def _proj(x2d, g, w_bf16):
    m = x2d.shape[0]
    tm = min(2048, m)
    n = w_bf16.shape[1]
    return pl.pallas_call(
        _proj_kernel,
        grid=(m // tm, n // PROJ_TN),
        in_specs=[pl.BlockSpec((tm, D_MODEL), lambda i, j: (i, 0)),
                  pl.BlockSpec((1, D_MODEL), lambda i, j: (0, 0)),
                  pl.BlockSpec((D_MODEL, PROJ_TN), lambda i, j: (0, j))],
        out_specs=pl.BlockSpec((tm, PROJ_TN), lambda i, j: (i, j)),
        out_shape=jax.ShapeDtypeStruct((m, n), F32),
        scratch_shapes=[pltpu.VMEM((tm, D_MODEL), BF16)],
        compiler_params=_params(("parallel", "arbitrary")),
        name="proj",
    )(x2d, g, w_bf16)


def _qkprep_kernel(q_ref, k_ref, v_ref, qg_ref, kg_ref, c_ref, s1_ref, s2_ref,
                   q16_ref, k32_ref, k16_ref, v32_ref, v16_ref):
    ones = _seg_ones(LANES, A_DQK)
    cosv, sin1, sin2 = c_ref[...], s1_ref[...], s2_ref[...]

    def prep(x, gain):
        ss = _segsum(x * x, ones)
        y = x * lax.rsqrt(ss * (1.0 / A_DQK) + EPS) * gain
        return y * cosv + pltpu.roll(y, LANES - A_ROT // 2, 1) * sin1 + pltpu.roll(y, A_ROT // 2, 1) * sin2

    for c in range(GROUP_W // LANES):
        sl = slice(c * LANES, (c + 1) * LANES)
        q = prep(q_ref[:, sl], qg_ref[...])
        q16_ref[:, sl] = (q * (A_DQK ** -0.5 * LOG2_E)).astype(BF16)
        k = prep(k_ref[:, sl], kg_ref[...])
        k32_ref[:, sl] = k
        k16_ref[:, sl] = k.astype(BF16)
    v = v_ref[...]
    v32_ref[...] = v
    v16_ref[...] = v.astype(BF16)


def _qkprep(proj, qg128, kg128, cos_t, sin1_t, sin2_t, t_len):
    m = proj.shape[0]
    tm = min(512, t_len)
    nt = t_len // tm
    row = lambda g: pl.BlockSpec((tm, GROUP_W), lambda i, g=g: (i, g))
    tab = pl.BlockSpec((tm, LANES), lambda i: (i % nt, 0))
    vec = pl.BlockSpec((1, LANES), lambda i: (0, 0))
    out = pl.BlockSpec((tm, GROUP_W), lambda i: (i, 0))
    return pl.pallas_call(
        _qkprep_kernel,
        grid=(m // tm,),
        in_specs=[row(G_AQ), row(G_AK), row(G_AV), vec, vec, tab, tab, tab],
        out_specs=[out, out, out, out, out],
        out_shape=[jax.ShapeDtypeStruct((m, GROUP_W), BF16), jax.ShapeDtypeStruct((m, GROUP_W), F32),
                   jax.ShapeDtypeStruct((m, GROUP_W), BF16), jax.ShapeDtypeStruct((m, GROUP_W), F32),
                   jax.ShapeDtypeStruct((m, GROUP_W), BF16)],
        compiler_params=_params(("parallel",)),
        name="qkprep",
    )(proj, proj, proj, qg128, kg128, cos_t, sin1_t, sin2_t)


def _attn_kernel(lam_ref, q_ref, kt_ref, v_ref, g_ref, o_ref, m1, l1, a1, m2, l2, a2,
                 *, tq, tk, q_off, nk, out_scale):
    qi = pl.program_id(2)
    q_first = q_off + qi * tq
    first_chunk_end = (q_first // CHUNK) * CHUNK + CHUNK
    last_vis = ((q_first + tq - 1) // CHUNK) * CHUNK + CHUNK - 1
    n_blocks = jnp.minimum(nk, last_vis // tk + 1)
    n_full = jnp.minimum(n_blocks, first_chunk_end // tk)

    for m, l, a in ((m1, l1, a1), (m2, l2, a2)):
        m[...] = jnp.full(m.shape, NEG_BIG, F32)
        l[...] = jnp.zeros(l.shape, F32)
        a[...] = jnp.zeros(a.shape, F32)

    q = q_ref[0]
    lane = lax.broadcasted_iota(jnp.int32, q.shape, 1)
    zero = jnp.zeros_like(q)
    q_halves = (jnp.where(lane < A_DQK, q, zero), jnp.where(lane >= A_DQK, q, zero))

    def step(kj, masked):
        if nk == 1:
            k0, kt, v = 0, kt_ref[0], v_ref[0]
        else:
            k0 = pl.multiple_of(kj * tk, tk)
            kt = kt_ref[0, :, pl.ds(k0, tk)]
            v = v_ref[0, pl.ds(k0, tk), :]
        if masked:
            qpos = q_first + lax.broadcasted_iota(jnp.int32, (tq, tk), 0)
            kpos = k0 + lax.broadcasted_iota(jnp.int32, (tq, tk), 1)
            vis = (kpos // CHUNK) <= (qpos // CHUNK)
        for qh, (m, l, a) in zip(q_halves, ((m1, l1, a1), (m2, l2, a2))):
            s = _dot(qh, kt)
            if masked:
                s = jnp.where(vis, s, NEG_BIG)
            m_prev = m[...]
            m_new = jnp.maximum(m_prev, jnp.max(s, axis=1, keepdims=True))
            pr = jnp.exp2(s - m_new[:, :1])
            alpha = jnp.exp2(m_prev - m_new)
            l[...] = alpha * l[...] + jnp.sum(pr, axis=1, keepdims=True)
            a[...] = alpha * a[...] + _dot(pr.astype(BF16), v)
            m[...] = m_new

    def full_body(kj, carry):
        step(kj, False)
        return carry

    def masked_body(kj, carry):
        step(kj, True)
        return carry

    lax.fori_loop(0, n_full, full_body, 0)
    lax.fori_loop(n_full, n_blocks, masked_body, 0)

    lam = lam_ref[0]
    o = a1[...] / l1[...] - lam * (a2[...] / l2[...])
    ms = jnp.mean(o * o, axis=-1, keepdims=True)
    o_ref[0] = o * lax.rsqrt(ms + EPS) * g_ref[...] * out_scale


def _attn(q16, k16t, v16, lam, subln_g, q_off, out_scale):
    b, tq_len, _ = q16.shape
    tk_len = v16.shape[1]
    tq = min(512, tq_len)
    tk = tq if tk_len > 2048 else tk_len
    assert tq_len % tq == 0 and tk_len % tk == 0
    nq, nk = tq_len // tq, tk_len // tk
    kern = functools.partial(_attn_kernel, tq=tq, tk=tk, q_off=q_off, nk=nk, out_scale=out_scale)
    acc = pltpu.VMEM((tq, LANES), F32)
    return pl.pallas_call(
        kern,
        grid=(b, A_HEADS, nq),
        in_specs=[pl.BlockSpec(memory_space=pltpu.SMEM),
                  pl.BlockSpec((1, tq, LANES), lambda bi, h, qi: (bi, qi, h)),
                  pl.BlockSpec((1, LANES, tk_len), lambda bi, h, qi: (bi, h, 0)),
                  pl.BlockSpec((1, tk_len, LANES), lambda bi, h, qi: (bi, 0, h)),
                  pl.BlockSpec((1, LANES), lambda bi, h, qi: (0, 0))],
        out_specs=pl.BlockSpec((1, tq, LANES), lambda bi, h, qi: (bi, qi, h)),
        out_shape=jax.ShapeDtypeStruct((b, tq_len, GROUP_W), F32),
        scratch_shapes=[acc, acc, acc, acc, acc, acc],
        compiler_params=_params(("parallel", "parallel", "parallel")),
        name="attn",
    )(lam, q16, k16t, v16, subln_g)


def _hgrn_kernel(q_ref, f_ref, i_ref, lb_ref, g_ref, s0_ref, o_ref, sfin_ref,
                 st_scr, cum_scr, qk_scr, oi_scr):
    c = pl.program_id(1)

    @pl.when(c == 0)
    def _():
        st_scr[...] = s0_ref[0]

    z = f_ref[...]
    lb = lb_ref[...]
    log_f = jnp.log(lb + (1.0 - lb) * _sigmoid(z))
    k_in = (1.0 - lb) * _sigmoid(-z)
    q = _silu(q_ref[...])
    cum = _cumsum_rows(log_f, _tri_incl(CHUNK))
    row = lax.broadcasted_iota(jnp.int32, (CHUNK, 1), 0)

    rel = cum - cum[CHUNK // 2 - 1:CHUNK // 2, :]
    safe = jnp.max(jnp.abs(rel)) <= HGRN_SAFE_SPAN

    @pl.when(safe)
    def _():
        qe = (q * jnp.exp(rel)).astype(BF16)
        ke = (k_in * jnp.exp(-rel)).astype(BF16)
        causal = lax.broadcasted_iota(jnp.int32, (CHUNK, CHUNK), 1) <= lax.broadcasted_iota(
            jnp.int32, (CHUNK, CHUNK), 0)
        for h in range(B_HEADS):
            sl = slice(h * LANES, (h + 1) * LANES)
            scores = jnp.where(causal, _dot_nt(qe[:, sl], ke[:, sl]), 0.0)
            oi_scr[:, sl] = _dot(scores.astype(BF16), i_ref[:, sl].astype(BF16))

    @pl.when(jnp.logical_not(safe))
    def _():
        cum_scr[...] = cum
        qk_scr[...] = q
        for h in range(B_HEADS):
            sl = slice(h * LANES, (h + 1) * LANES)
            cum_h = cum[:, sl]
            kin_h = k_in[:, sl]
            i_h = i_ref[:, sl]

            def body(g, carry, sl=sl, cum_h=cum_h, kin_h=kin_h, i_h=i_h):
                g8 = pl.multiple_of(g * SUBLANES, SUBLANES)
                c_tile = cum_scr[pl.ds(g8, SUBLANES), sl]
                q_tile = qk_scr[pl.ds(g8, SUBLANES), sl]
                rows = []
                for r in range(SUBLANES):
                    dec = jnp.exp(jnp.minimum(c_tile[r:r + 1] - cum_h, 0.0))
                    col = jnp.sum(dec * (kin_h * q_tile[r:r + 1]), axis=1, keepdims=True)
                    col = jnp.where(row <= g8 + r, col, 0.0)
                    rows.append(jnp.sum(col * i_h, axis=0, keepdims=True))
                oi_scr[pl.ds(g8, SUBLANES), sl] = jnp.concatenate(rows, axis=0)
                return carry

            lax.fori_loop(0, CHUNK // SUBLANES, body, 0)

    for h in range(B_HEADS):
        sl = slice(h * LANES, (h + 1) * LANES)
        cum_h = cum[:, sl]
        kin_h = k_in[:, sl]
        i_h = i_ref[:, sl]
        st = st_scr[h]
        qe = (q[:, sl] * jnp.exp(cum_h)).astype(BF16)
        o = oi_scr[:, sl] + _dot_nt(qe, st.astype(BF16))
        cum_last = cum_h[CHUNK - 1:CHUNK, :]
        kt = (kin_h * jnp.exp(cum_last - cum_h)).astype(BF16)
        st_scr[h] = st * jnp.exp(cum_last) + _dot_tn(i_h.astype(BF16), kt)
        ms = jnp.mean(o * o, axis=-1, keepdims=True)
        o_ref[:, sl] = o * lax.rsqrt(ms + EPS) * g_ref[...]

    sfin_ref[0] = st_scr[...]


def _hgrn(proj, lb, norm_g128, s0_t, bsz, nc):
    m = proj.shape[0]
    row = lambda g: pl.BlockSpec((CHUNK, GROUP_W), lambda b, c, g=g: (b * nc + c, g))
    st_spec = pl.BlockSpec((1, B_HEADS, LANES, LANES), lambda b, c: (b, 0, 0, 0))
    return pl.pallas_call(
        _hgrn_kernel,
        grid=(bsz, nc),
        in_specs=[row(G_BQ), row(G_BF), row(G_BI),
                  pl.BlockSpec((1, GROUP_W), lambda b, c: (0, 0)),
                  pl.BlockSpec((1, LANES), lambda b, c: (0, 0)),
                  st_spec],
        out_specs=[pl.BlockSpec((CHUNK, GROUP_W), lambda b, c: (b * nc + c, 0)), st_spec],
        out_shape=[jax.ShapeDtypeStruct((m, GROUP_W), F32),
                   jax.ShapeDtypeStruct((bsz, B_HEADS, LANES, LANES), F32)],
        scratch_shapes=[pltpu.VMEM((B_HEADS, LANES, LANES), F32),
                        pltpu.VMEM((CHUNK, GROUP_W), F32),
                        pltpu.VMEM((CHUNK, GROUP_W), F32),
                        pltpu.VMEM((CHUNK, GROUP_W), F32)],
        compiler_params=_params(("parallel", "arbitrary")),
        name="hgrn",
    )(proj, proj, proj, lb, norm_g128, s0_t)


def _rwkprep_kernel(*refs, has_vres):
    (cr_ref, ck_ref, cv_ref, clo_ref, spr_ref, spk_ref, spv_ref, splo_ref,
     mur_ref, muk_ref, muv_ref, mulo_ref, w0_ref, w2h_ref, w2l_ref, a0_ref, a2h_ref, a2l_ref,
     kk_ref, ka_ref, rk_ref) = refs[:21]
    pos = 21
    if has_vres:
        vres_ref, vf_ref, v0_ref, vw2h_ref, vw2l_ref = refs[pos:pos + 5]
        pos += 5
    (at_ref, rt_ref, bh_ref, kh_ref, vc_ref, bonus_ref, gam_ref, nab_ref, arb_ref, g_ref, y0_ref,
     shr_ref, shk_ref, shv_ref, shlo_ref) = refs[pos:pos + 15]
    pr_scr, pk_scr, pv_scr, plo_scr = refs[pos + 15:]
    c = pl.program_id(1)

    @pl.when(c == 0)
    def _():
        pr_scr[...] = spr_ref[0]
        pk_scr[...] = spk_ref[0]
        pv_scr[...] = spv_ref[0]
        plo_scr[...] = splo_ref[0]

    def shifted(x_ref, prev_scr, mu_ref, last_ref):
        x = x_ref[...]
        row = lax.broadcasted_iota(jnp.int32, x.shape, 0)
        prev = jnp.where(row == 0, prev_scr[...], pltpu.roll(x, 1, 0))
        last = x[CHUNK - 1:CHUNK, :]
        prev_scr[...] = last
        last_ref[0] = last
        return x + (prev - x) * mu_ref[...]

    r = shifted(cr_ref, pr_scr, mur_ref, shr_ref)
    k0 = shifted(ck_ref, pk_scr, muk_ref, shk_ref)
    v = shifted(cv_ref, pv_scr, muv_ref, shv_ref)
    lo = shifted(clo_ref, plo_scr, mulo_ref, shlo_ref)

    def lowrank(x, wh_ref, wl_ref):
        xh, xl = _split2(x)
        return _dot(xh, wh_ref[...]) + _dot(xl, wh_ref[...]) + _dot(xh, wl_ref[...])

    w_in = w0_ref[...] + lowrank(jnp.tanh(lo), w2h_ref, w2l_ref)
    nw = -w_in
    softplus = jnp.maximum(nw, 0.0) + jnp.log(1.0 + jnp.exp(-jnp.abs(nw)))
    log_decay = -jnp.exp(-softplus - 0.5)
    a_sig = _sigmoid(a0_ref[...] + lowrank(lo, a2h_ref, a2l_ref))
    if has_vres:
        v_mix = _sigmoid(v0_ref[...] + lowrank(vres_ref[...], vw2h_ref, vw2l_ref))
        v = v + (vf_ref[...] - v) * v_mix
    vc_ref[...] = v

    ones = _seg_ones(LANES, C_DH)
    cum = _cumsum_rows(log_decay, _tri_incl(CHUNK))
    cum_last = cum[CHUNK - 1:CHUNK, :]
    gam_ref[0] = jnp.exp(cum_last)
    e_prev = jnp.exp(cum - log_decay)
    e_cum = jnp.exp(cum)
    e_inv = jnp.exp(-cum)
    e_tail = jnp.exp(cum_last - cum)

    lane = lax.broadcasted_iota(jnp.int32, (CHUNK, LANES), 1)
    par1 = lane >= C_DH
    tcol = lax.broadcasted_iota(jnp.int32, (CHUNK, LANES), 0)
    scol = lane % C_DH
    strict = jnp.concatenate([scol < tcol, scol < tcol], axis=1)
    incl = jnp.concatenate([scol <= tcol, scol <= tcol], axis=1)
    low_mask = jnp.concatenate([strict, incl], axis=0)

    for p in range(C_WIDTH // LANES):
        sl = slice(p * LANES, (p + 1) * LANES)
        kk = k0[:, sl] * kk_ref[:, sl]
        n2 = _segsum(kk * kk, ones)
        kk = kk / jnp.maximum(jnp.sqrt(n2), 1e-12)
        a_p = a_sig[:, sl]
        k = k0[:, sl] * (1.0 + (a_p - 1.0) * ka_ref[:, sl])
        r_p = r[:, sl]
        v_p = v[:, sl]
        b_vec = kk * a_p
        bonus_ref[:, sl] = _segsum(r_p * k * rk_ref[:, sl], ones) * v_p
        a_t = -kk * e_prev[:, sl]
        r_t = r_p * e_cum[:, sl]
        b_t = b_vec * e_inv[:, sl]
        k_t = k * e_inv[:, sl]
        at_ref[:, sl] = a_t
        rt_ref[:, sl] = r_t
        bh_ref[:, sl] = b_vec * e_tail[:, sl]
        kh_ref[:, sl] = k * e_tail[:, sl]
        zero = jnp.zeros_like(b_t)
        lhs = jnp.concatenate([a_t, r_t], axis=0).astype(BF16)
        rhs = jnp.concatenate([jnp.where(par1, zero, b_t), jnp.where(par1, b_t, zero),
                               jnp.where(par1, zero, k_t), jnp.where(par1, k_t, zero)], axis=0).astype(BF16)
        prod = jnp.where(low_mask, _dot_nt(lhs, rhs), 0.0)
        nab_ref[:, sl] = prod[:CHUNK, :LANES]
        arb_ref[:, sl] = prod[CHUNK:, :LANES]
        vbd = jnp.concatenate([jnp.where(par1, zero, v_p), jnp.where(par1, v_p, zero)], axis=0).astype(BF16)
        gy = _dot(prod[:, LANES:].astype(BF16), vbd)
        g_ref[:, sl] = gy[:CHUNK]
        y0_ref[:, sl] = gy[CHUNK:]


def _rwkprep(proj, shift_parts, params, vres, bsz, nc):
    m = proj.shape[0]
    has_vres = vres is not None
    row = lambda g: pl.BlockSpec((CHUNK, GROUP_W), lambda b, c, g=g: (b * nc + c, g))
    lo_spec = pl.BlockSpec((CHUNK, LANES), lambda b, c: (b * nc + c, COL_LO // LANES))
    st = lambda w: pl.BlockSpec((1, 1, w), lambda b, c: (b, 0, 0))
    vec = lambda w: pl.BlockSpec((1, w), lambda b, c: (0, 0))
    mat = lambda: pl.BlockSpec((LANES, GROUP_W), lambda b, c: (0, 0))
    out = pl.BlockSpec((CHUNK, GROUP_W), lambda b, c: (b * nc + c, 0))
    in_specs = [row(G_CR), row(G_CK), row(G_CV), lo_spec, st(GROUP_W), st(GROUP_W), st(GROUP_W), st(LANES),
                vec(GROUP_W), vec(GROUP_W), vec(GROUP_W), vec(LANES),
                vec(GROUP_W), mat(), mat(), vec(GROUP_W), mat(), mat(),
                vec(GROUP_W), vec(GROUP_W), vec(GROUP_W)]
    args = [proj, proj, proj, proj, *shift_parts,
            params["mu_r"], params["mu_k"], params["mu_v"], params["mu_lo"],
            params["w0"], params["w2h"], params["w2l"], params["a0"], params["a2h"], params["a2l"],
            params["k_k"], params["k_a"], params["r_k"]]
    if has_vres:
        in_specs += [pl.BlockSpec((CHUNK, LANES), lambda b, c: (b * nc + c, COL_VRES // LANES)),
                     out, vec(GROUP_W), mat(), mat()]
        args += [proj, vres["v_first"], vres["v0"], vres["w2h"], vres["w2l"]]
    big = jax.ShapeDtypeStruct((m, GROUP_W), F32)
    out_shape = [big] * 6 + [jax.ShapeDtypeStruct((bsz * nc, 1, GROUP_W), F32)] + [big] * 4 + [
        jax.ShapeDtypeStruct((bsz, 1, GROUP_W), F32)] * 3 + [jax.ShapeDtypeStruct((bsz, 1, LANES), F32)]
    out_specs = [out] * 6 + [pl.BlockSpec((1, 1, GROUP_W), lambda b, c: (b * nc + c, 0, 0))] + [out] * 4 + [
        st(GROUP_W)] * 3 + [st(LANES)]
    return pl.pallas_call(
        functools.partial(_rwkprep_kernel, has_vres=has_vres),
        grid=(bsz, nc),
        in_specs=in_specs,
        out_specs=out_specs,
        out_shape=out_shape,
        scratch_shapes=[pltpu.VMEM((1, GROUP_W), F32)] * 3 + [pltpu.VMEM((1, LANES), F32)],
        compiler_params=_params(("parallel", "arbitrary")),
        name="rwkprep",
    )(*args)


def _rwksolve_kernel(n_ref, ra_ref, rg_ref, w_ref, u_ref):
    def group_body(tg, carry):
        t0 = pl.multiple_of(tg * SUBLANES, SUBLANES)
        for r in range(SUBLANES):
            t = t0 + r

            def s_body(sg, acc, t=t):
                s0 = pl.multiple_of(sg * SUBLANES, SUBLANES)
                coef = n_ref[t, pl.ds(s0, SUBLANES), :]
                acc_w, acc_u = acc
                for q in range(SUBLANES):
                    acc_w = acc_w + coef[q:q + 1] * w_ref[s0 + q]
                    acc_u = acc_u + coef[q:q + 1] * u_ref[s0 + q]
                return acc_w, acc_u

            acc_w, acc_u = lax.fori_loop(0, tg, s_body, (ra_ref[t], rg_ref[t]))
            coef = n_ref[t, pl.ds(t0, SUBLANES), :]
            for q in range(r):
                acc_w = acc_w + coef[q:q + 1] * w_ref[t0 + q]
                acc_u = acc_u + coef[q:q + 1] * u_ref[t0 + q]
            w_ref[t] = acc_w
            u_ref[t] = acc_u
        return carry

    lax.fori_loop(0, CHUNK // SUBLANES, group_body, 0)


def _rwksolve(n_bl, ra_bl, rg_bl):
    n_inst = n_bl.shape[-1]
    spec = pl.BlockSpec((CHUNK, C_DH, LANES), lambda i: (0, 0, i))
    shape = jax.ShapeDtypeStruct(ra_bl.shape, F32)
    return pl.pallas_call(
        _rwksolve_kernel,
        grid=(n_inst // LANES,),
        in_specs=[spec, spec, spec],
        out_specs=[spec, spec],
        out_shape=[shape, shape],
        compiler_params=_params(("parallel",)),
        name="rwksolve",
    )(n_bl, ra_bl, rg_bl)


def _rwkseq_kernel(w_ref, u0_ref, rt_ref, arb_ref, y0_ref, v_ref, bh_ref, kh_ref, gam_ref, bonus_ref,
                   lnw_ref, lnb_ref, s0_ref, o_ref, sfin_ref, st_scr):
    c = pl.program_id(1)

    @pl.when(c == 0)
    def _():
        st_scr[...] = s0_ref[0]

    ones = _seg_ones(LANES, C_DH)
    lane = lax.broadcasted_iota(jnp.int32, (CHUNK, LANES), 1)
    par1 = lane >= C_DH
    bd = _seg_mask(LANES, C_DH)
    for p in range(C_WIDTH // LANES):
        sl = slice(p * LANES, (p + 1) * LANES)
        st = st_scr[p]
        st16 = st.astype(BF16)
        u = _dot_nt(w_ref[:, sl].astype(BF16), st16) + u0_ref[:, sl]
        zero = jnp.zeros_like(u)
        ubd = jnp.concatenate([jnp.where(par1, zero, u), jnp.where(par1, u, zero)], axis=0).astype(BF16)
        y = (_dot_nt(rt_ref[:, sl].astype(BF16), st16) + _dot(arb_ref[:, sl].astype(BF16), ubd)
             + y0_ref[:, sl])
        uv = jnp.concatenate([u, v_ref[:, sl]], axis=0).astype(BF16)
        bk = jnp.concatenate([bh_ref[:, sl], kh_ref[:, sl]], axis=0).astype(BF16)
        st_scr[p] = st * gam_ref[0, :, sl] + jnp.where(bd, _dot_tn(uv, bk), 0.0)
        mu = _segsum(y, ones) * (1.0 / C_DH)
        d = y - mu
        var = _segsum(d * d, ones) * (1.0 / C_DH)
        o_ref[:, sl] = d * lax.rsqrt(var + C_GN_EPS) * lnw_ref[:, sl] + lnb_ref[:, sl] + bonus_ref[:, sl]

    sfin_ref[0] = st_scr[...]


def _rwkseq(w, u0, rt, arb, y0, v, bh, kh, gam, bonus, ln_w, ln_b, s0_bd, bsz, nc):
    m = w.shape[0]
    row = pl.BlockSpec((CHUNK, GROUP_W), lambda b, c: (b * nc + c, 0))
    vec = pl.BlockSpec((1, GROUP_W), lambda b, c: (0, 0))
    st_spec = pl.BlockSpec((1, C_WIDTH // LANES, LANES, LANES), lambda b, c: (b, 0, 0, 0))
    return pl.pallas_call(
        _rwkseq_kernel,
        grid=(bsz, nc),
        in_specs=[row] * 8 + [pl.BlockSpec((1, 1, GROUP_W), lambda b, c: (b * nc + c, 0, 0)), row, vec, vec,
                  st_spec],
        out_specs=[row, st_spec],
        out_shape=[jax.ShapeDtypeStruct((m, GROUP_W), F32),
                   jax.ShapeDtypeStruct((bsz, C_WIDTH // LANES, LANES, LANES), F32)],
        scratch_shapes=[pltpu.VMEM((C_WIDTH // LANES, LANES, LANES), F32)],
        compiler_params=_params(("parallel", "arbitrary")),
        name="rwkseq",
    )(w, u0, rt, arb, y0, v, bh, kh, gam, bonus, ln_w, ln_b, s0_bd)


def _merge_kernel(x_ref, oa_ref, ob_ref, oc_ref, ag_ref, bg_ref, cg_ref, ma_ref, mb_ref, mc_ref,
                  wa_ref, wb_ref, wc_ref, wo_ref, y_ref):
    def branch(o_ref, gate_ref, w_ref):
        return _dot((o_ref[...] * _silu(gate_ref[...])).astype(BF16), w_ref[...])

    merged = (_sigmoid(ma_ref[...]) * branch(oa_ref, ag_ref, wa_ref)
              + _sigmoid(mb_ref[...]) * branch(ob_ref, bg_ref, wb_ref)
              + _sigmoid(mc_ref[...]) * branch(oc_ref, cg_ref, wc_ref))
    y_ref[...] = x_ref[...] + _dot(merged.astype(BF16), wo_ref[...])


def _merge(x2d, o_a, o_b, o_c, proj, wa, wb, wc, wo):
    m = x2d.shape[0]
    tm = min(256, m)
    row = pl.BlockSpec((tm, GROUP_W), lambda i: (i, 0))
    grp = lambda g: pl.BlockSpec((tm, GROUP_W), lambda i, g=g: (i, g))
    wsp = pl.BlockSpec((GROUP_W, D_MODEL), lambda i: (0, 0))
    return pl.pallas_call(
        _merge_kernel,
        grid=(m // tm,),
        in_specs=[row, row, row, row, grp(G_AG), grp(G_BG), grp(G_CG), grp(G_MA), grp(G_MB), grp(G_MC),
                  wsp, wsp, wsp, wsp],
        out_specs=row,
        out_shape=jax.ShapeDtypeStruct((m, D_MODEL), F32),
        compiler_params=_params(("parallel",)),
        name="merge",
    )(x2d, o_a, o_b, o_c, proj, proj, proj, proj, proj, proj, wa, wb, wc, wo)


_C_OFF = 8 * GROUP_W
_CP_R = (_C_OFF, _C_OFF + C_WIDTH)
_CP_WLO = (_CP_R[1], _CP_R[1] + C_DECAY_RANK)
_CP_K = (_CP_WLO[1], _CP_WLO[1] + C_WIDTH)
_CP_V = (_CP_K[1], _CP_K[1] + C_WIDTH)
_CP_ALO = (_CP_V[1], _CP_V[1] + C_A_RANK)
_REST = _CP_ALO[1]


def _regroup_w_in(w, vres_w1):
    cols = [w[:, :_C_OFF], w[:, _CP_R[0]:_CP_R[1]], w[:, _CP_K[0]:_CP_K[1]], w[:, _CP_V[0]:_CP_V[1]],
            w[:, _REST:], w[:, _CP_WLO[0]:_CP_WLO[1]], w[:, _CP_ALO[0]:_CP_ALO[1]]]
    vres = jnp.zeros((D_MODEL, LANES), F32)
    if vres_w1 is not None:
        vres = vres.at[:, :C_VRES_RANK].set(vres_w1)
    cols.append(vres)
    cols.append(jnp.zeros((D_MODEL, N_PROJ_PAD - N_PROJ), F32))
    return jnp.concatenate(cols, axis=1).astype(BF16)


def _split_shift(s):
    o = _C_OFF
    part = lambda a: s[:, a[0] - o:a[1] - o]
    lo = jnp.concatenate([part(_CP_WLO), part(_CP_ALO)], axis=1)
    return [x[:, None, :] for x in (part(_CP_R), part(_CP_K), part(_CP_V), lo)]


def _join_shift(r, k, v, lo):
    r, k, v, lo = (x[:, 0, :] for x in (r, k, v, lo))
    return jnp.concatenate([r, lo[:, :C_DECAY_RANK], k, v, lo[:, C_DECAY_RANK:]], axis=1)


def _pad_rows(w, row0):
    out = jnp.zeros((LANES, w.shape[1]), F32).at[row0:row0 + w.shape[0]].set(w)
    hi = out.astype(BF16)
    return hi, (out - hi.astype(F32)).astype(BF16)


def _rope_tables(pos):
    half = A_ROT // 2
    inv_freq = ROPE_THETA ** (-(jnp.arange(half, dtype=F32) * (2.0 / A_ROT)))
    ang = pos.astype(F32)[:, None] * inv_freq[None, :]
    cos, sin = jnp.cos(ang), jnp.sin(ang)
    t = pos.shape[0]
    one = jnp.ones((t, A_DQK - A_ROT), F32)
    zero = jnp.zeros((t, A_DQK - A_ROT), F32)
    z8 = jnp.zeros((t, half), F32)
    c64 = jnp.concatenate([cos, cos, one], axis=1)
    s1 = jnp.concatenate([-sin, z8, zero], axis=1)
    s2 = jnp.concatenate([z8, sin, zero], axis=1)
    tile = lambda x: jnp.concatenate([x, x], axis=1)
    return tile(c64), tile(s1), tile(s2)


def _to_lanes(x, rows):
    return x.reshape(rows, CHUNK, C_HEADS, C_DH).transpose(1, 3, 0, 2).reshape(CHUNK, C_DH, rows * C_HEADS)


def _from_lanes(x, rows):
    return x.reshape(CHUNK, C_DH, rows, C_HEADS).transpose(2, 0, 3, 1).reshape(rows * CHUNK, C_WIDTH)


def _layer(l, x2d, bsz, t_len, q_off, P, lb, past, v_first):
    nc = t_len // CHUNK
    m = bsz * t_len
    w_in = _regroup_w_in(P["w_in"][l], P["c_vres_w1"][l - 1] if l > 0 else None)
    proj = _proj(x2d, P["norm_g"][l][None, :], w_in)

    pos = q_off + jnp.arange(t_len, dtype=jnp.int32)
    cos_t, sin1_t, sin2_t = _rope_tables(pos)
    tile2 = lambda g: jnp.concatenate([g, g])[None, :]
    q16, k32, k16, v32, v16 = _qkprep(proj, tile2(P["a_qnorm_g"][l]), tile2(P["a_knorm_g"][l]),
                                      cos_t, sin1_t, sin2_t, t_len)
    q16 = q16.reshape(bsz, t_len, GROUP_W)
    k16 = k16.reshape(bsz, t_len, GROUP_W)
    v16 = v16.reshape(bsz, t_len, GROUP_W)
    if past is not None:
        pk, pv = past[0][l], past[1][l]
        p_len = pk.shape[1]
        k16 = jnp.concatenate([pk.reshape(bsz, p_len, GROUP_W).astype(BF16), k16], axis=1)
        v16 = jnp.concatenate([pv.reshape(bsz, p_len, GROUP_W).astype(BF16), v16], axis=1)
    k16t = jnp.swapaxes(k16, 1, 2)
    lam_init = 0.8 - 0.6 * math.exp(-0.3 * l)
    lp = P["a_lambda"][l].astype(F32)
    lam = jnp.exp(jnp.sum(lp[0] * lp[1])) - jnp.exp(jnp.sum(lp[2] * lp[3])) + lam_init
    o_a = _attn(q16, k16t, v16, lam.reshape(1), P["a_subln_g"][l][None, :], q_off, 1.0 - lam_init)
    o_a = o_a.reshape(m, GROUP_W)

    if past is None:
        s_h = jnp.zeros((bsz, B_HEADS, LANES, LANES), F32)
    else:
        s_h = jnp.swapaxes(past[2][l].astype(F32), -1, -2)
    o_b, s_h_new = _hgrn(proj, lb[l][None, :], P["b_norm_g"][l][None, :], s_h, bsz, nc)
    s_h_new = jnp.swapaxes(s_h_new, -1, -2)

    if past is None:
        shift_prev = jnp.zeros((bsz, 3 * C_WIDTH + C_DECAY_RANK + C_A_RANK), F32)
        s_r = jnp.zeros((bsz, C_HEADS, C_DH, C_DH), F32)
    else:
        shift_prev, s_r = past[4][l], past[3][l].astype(F32)
    mu = _split_shift(P["c_shift_mu"][l][None, :])
    w2h, w2l = _pad_rows(P["c_w2"][l], 0)
    a2h, a2l = _pad_rows(P["c_a2"][l], C_DECAY_RANK)
    cparams = {"mu_r": mu[0][0], "mu_k": mu[1][0], "mu_v": mu[2][0], "mu_lo": mu[3][0],
               "w0": P["c_w0"][l][None, :], "w2h": w2h, "w2l": w2l,
               "a0": P["c_a0"][l][None, :], "a2h": a2h, "a2l": a2l,
               "k_k": P["c_k_k"][l][None, :], "k_a": P["c_k_a"][l][None, :],
               "r_k": P["c_r_k"][l].reshape(1, C_WIDTH)}
    vres = None
    if l > 0:
        vh, vl = _pad_rows(P["c_vres_w2"][l - 1], 0)
        vres = {"v_first": v_first, "v0": P["c_v0"][l - 1][None, :], "w2h": vh, "w2l": vl}
    (a_t, r_t, b_h, k_h, v_c, bonus, gam, nab, arb, g_mat, y0, sh_r, sh_k, sh_v, sh_lo) = _rwkprep(
        proj, _split_shift(shift_prev), cparams, vres, bsz, nc)
    rows = bsz * nc
    n_inst = rows * C_HEADS
    pad = (-n_inst) % LANES
    to_lanes = lambda x: jnp.pad(_to_lanes(x, rows), ((0, 0), (0, 0), (0, pad))) if pad else _to_lanes(x, rows)
    w_bl, u_bl = _rwksolve(to_lanes(nab), to_lanes(a_t), to_lanes(g_mat))
    w_mat = _from_lanes(w_bl[:, :, :n_inst], rows)
    u0 = _from_lanes(u_bl[:, :, :n_inst], rows)
    eye2 = jnp.eye(2, dtype=F32)
    s_bd = jnp.einsum("bpqvk,qr->bpqvrk", s_r.reshape(bsz, C_HEADS // 2, 2, C_DH, C_DH), eye2)
    s_bd = s_bd.reshape(bsz, C_HEADS // 2, LANES, LANES)
    o_c, s_bd_new = _rwkseq(w_mat, u0, r_t, arb, y0, v_c, b_h, k_h, gam, bonus,
                            P["c_ln_w"][l][None, :], P["c_ln_b"][l][None, :], s_bd, bsz, nc)
    s_new6 = s_bd_new.reshape(bsz, C_HEADS // 2, 2, C_DH, 2, C_DH)
    s_r_new = jnp.stack([s_new6[:, :, 0, :, 0, :], s_new6[:, :, 1, :, 1, :]], axis=2)
    s_r_new = s_r_new.reshape(bsz, C_HEADS, C_DH, C_DH)
    shift_new = _join_shift(sh_r, sh_k, sh_v, sh_lo)

    bf = lambda w: w.astype(BF16)
    y = _merge(x2d, o_a, o_b, o_c, proj, bf(P["w_out_a"][l]), bf(P["w_out_b"][l]), bf(P["w_out_c"][l]),
               bf(P["w_o"][l]))
    k_rows = k32.reshape(bsz, t_len, A_HEADS, 2 * A_DQK)
    v_rows = v32.reshape(bsz, t_len, A_HEADS, 2 * A_DQK)
    return y, (k_rows, v_rows, s_h_new, s_r_new, shift_new), v_c


def _run_trunk(x, q_off, P, lb, past):
    bsz, t_len, _ = x.shape
    depth = P["w_in"].shape[0]
    x2d = x.reshape(bsz * t_len, D_MODEL)
    outs = ([], [], [], [], [])
    v_first = None
    for l in range(depth):
        x2d, entries, v_c = _layer(l, x2d, bsz, t_len, q_off, P, lb, past, v_first)
        if l == 0:
            v_first = v_c
        for lst, e in zip(outs, entries):
            lst.append(e)
    return x2d.reshape(bsz, t_len, D_MODEL), [jnp.stack(lst) for lst in outs]


def kernel(x_prompt, x_sample, cache_attn_k, cache_attn_v, state_hgrn, state_rwkv, state_rwkv_shift,
           norm_g, w_in, a_qnorm_g, a_knorm_g, a_lambda, a_subln_g, b_lower, b_norm_g,
           c_shift_mu, c_w0, c_w2, c_a0, c_a2, c_k_k, c_k_a, c_r_k, c_ln_w, c_ln_b,
           c_vres_w1, c_vres_w2, c_v0, w_out_a, w_out_b, w_out_c, w_o):
    P = {"norm_g": norm_g, "w_in": w_in, "a_qnorm_g": a_qnorm_g, "a_knorm_g": a_knorm_g,
         "a_lambda": a_lambda, "a_subln_g": a_subln_g, "b_norm_g": b_norm_g,
         "c_shift_mu": c_shift_mu, "c_w0": c_w0, "c_w2": c_w2, "c_a0": c_a0, "c_a2": c_a2,
         "c_k_k": c_k_k, "c_k_a": c_k_a, "c_r_k": c_r_k, "c_ln_w": c_ln_w, "c_ln_b": c_ln_b,
         "c_vres_w1": c_vres_w1, "c_vres_w2": c_vres_w2, "c_v0": c_v0,
         "w_out_a": w_out_a, "w_out_b": w_out_b, "w_out_c": w_out_c, "w_o": w_o}
    sm = jax.nn.softmax(b_lower.astype(F32), axis=0)
    lb = jnp.cumsum(sm, axis=0) - sm[0:1]
    past_len = cache_attn_k.shape[2]
    y_p, (k_p, v_p, hg_p, rw_p, sh_p) = _run_trunk(x_prompt, 0, P, lb, None)
    y_s, (k_s, v_s, hg_s, rw_s, sh_s) = _run_trunk(
        x_sample, past_len, P, lb, (cache_attn_k, cache_attn_v, state_hgrn, state_rwkv, state_rwkv_shift))
    return (y_p, y_s, k_p, v_p, hg_p, rw_p, sh_p, k_s, v_s, hg_s, rw_s, sh_s)
```

```python
import functools
import math

import jax
import jax.numpy as jnp
from jax import lax
from jax.experimental import pallas as pl
from jax.experimental.pallas import tpu as pltpu

F32 = jnp.float32
BF16 = jnp.bfloat16

D_MODEL = 1024
CHUNK = 64
EPS = 1e-6
NEG_BIG = -1e30
LOG2_E = 1.4426950408889634
A_HEADS = 8
A_DQK = 64
A_ROT = 16
ROPE_THETA = 500000.0
B_HEADS = 8
C_HEADS = 16
C_DH = 64
C_WIDTH = 1024
C_DECAY_RANK = 64
C_A_RANK = 64
C_VRES_RANK = 32
C_GN_EPS = 64e-5
HGRN_SAFE_SPAN = 60.0
LANES = 128
SUBLANES = 8
GROUP_W = 1024
N_GROUPS = 15
COL_LO = N_GROUPS * GROUP_W
COL_VRES = COL_LO + LANES
N_PROJ = COL_VRES + LANES
PROJ_TN = 512
N_PROJ_PAD = -(-N_PROJ // PROJ_TN) * PROJ_TN
(G_AQ, G_AK, G_AV, G_AG, G_BQ, G_BF, G_BI, G_BG, G_CR, G_CK, G_CV, G_CG, G_MA, G_MB, G_MC) = range(N_GROUPS)
VMEM_LIMIT = 56 * 1024 * 1024


def _dot(a, b):
    return jnp.dot(a, b, preferred_element_type=F32)


def _dot_nt(a, b):
    return lax.dot_general(a, b, (((1,), (1,)), ((), ())), preferred_element_type=F32)


def _dot_tn(a, b):
    return lax.dot_general(a, b, (((0,), (0,)), ((), ())), preferred_element_type=F32)


def _split2(x):
    hi = x.astype(BF16)
    lo = (x - hi.astype(F32)).astype(BF16)
    return hi, lo


def _split3(x):
    hi = x.astype(BF16)
    r = x - hi.astype(F32)
    mid = r.astype(BF16)
    lo = (r - mid.astype(F32)).astype(BF16)
    return hi, mid, lo


def _sigmoid(x):
    return 1.0 / (1.0 + jnp.exp(-x))


def _silu(x):
    return x * _sigmoid(x)


def _seg_mask(n, seg):
    r = lax.broadcasted_iota(jnp.int32, (n, n), 0) // seg
    c = lax.broadcasted_iota(jnp.int32, (n, n), 1) // seg
    return r == c


def _seg_ones(n, seg):
    return _seg_mask(n, seg).astype(BF16)


def _segsum(x, ones_bf16):
    hi, lo = _split2(x)
    return _dot(hi, ones_bf16) + _dot(lo, ones_bf16)


def _tri_incl(n):
    r = lax.broadcasted_iota(jnp.int32, (n, n), 0)
    c = lax.broadcasted_iota(jnp.int32, (n, n), 1)
    return (c <= r).astype(BF16)


def _cumsum_rows(x, tri_bf16):
    hi, mid, lo = _split3(x)
    return _dot(tri_bf16, hi) + _dot(tri_bf16, mid) + _dot(tri_bf16, lo)


def _params(sem, vmem=None):
    return pltpu.CompilerParams(dimension_semantics=sem, vmem_limit_bytes=vmem or VMEM_LIMIT)


def _proj_kernel(x_ref, g_ref, w_ref, o_ref, h_scr):
    @pl.when(pl.program_id(1) == 0)
    def _():
        x = x_ref[...]
        ms = jnp.mean(x * x, axis=-1, keepdims=True)
        h_scr[...] = (x * lax.rsqrt(ms + EPS) * g_ref[...]).astype(BF16)

    o_ref[...] = _dot(h_scr[...], w_ref[...])


def _proj(x2d, g, w_bf16):
    m = x2d.shape[0]
    tm = min(2048, m)
    n = w_bf16.shape[1]
    return pl.pallas_call(
        _proj_kernel,
        grid=(m // tm, n // PROJ_TN),
        in_specs=[pl.BlockSpec((tm, D_MODEL), lambda i, j: (i, 0)),
                  pl.BlockSpec((1, D_MODEL), lambda i, j: (0, 0)),
                  pl.BlockSpec((D_MODEL, PROJ_TN), lambda i, j: (0, j))],
        out_specs=pl.BlockSpec((tm, PROJ_TN), lambda i, j: (i, j)),
        out_shape=jax.ShapeDtypeStruct((m, n), F32),
        scratch_shapes=[pltpu.VMEM((tm, D_MODEL), BF16)],
        compiler_params=_params(("parallel", "arbitrary")),
        name="proj",
    )(x2d, g, w_bf16)


def _qkprep_kernel(q_ref, k_ref, v_ref, qg_ref, kg_ref, c_ref, s1_ref, s2_ref,
                   q16_ref, k32_ref, k16_ref, v32_ref, v16_ref):
    ones = _seg_ones(LANES, A_DQK)
    cosv, sin1, sin2 = c_ref[...], s1_ref[...], s2_ref[...]

    def prep(x, gain):
        ss = _segsum(x * x, ones)
        y = x * lax.rsqrt(ss * (1.0 / A_DQK) + EPS) * gain
        return y * cosv + pltpu.roll(y, LANES - A_ROT // 2, 1) * sin1 + pltpu.roll(y, A_ROT // 2, 1) * sin2

    for c in range(GROUP_W // LANES):
        sl = slice(c * LANES, (c + 1) * LANES)
        q = prep(q_ref[:, sl], qg_ref[...])
        q16_ref[:, sl] = (q * (A_DQK ** -0.5 * LOG2_E)).astype(BF16)
        k = prep(k_ref[:, sl], kg_ref[...])
        k32_ref[:, sl] = k
        k16_ref[:, sl] = k.astype(BF16)
    v = v_ref[...]
    v32_ref[...] = v
    v16_ref[...] = v.astype(BF16)


def _qkprep(proj, qg128, kg128, cos_t, sin1_t, sin2_t, t_len):
    m = proj.shape[0]
    tm = min(512, t_len)
    nt = t_len // tm
    row = lambda g: pl.BlockSpec((tm, GROUP_W), lambda i, g=g: (i, g))
    tab = pl.BlockSpec((tm, LANES), lambda i: (i % nt, 0))
    vec = pl.BlockSpec((1, LANES), lambda i: (0, 0))
    out = pl.BlockSpec((tm, GROUP_W), lambda i: (i, 0))
    return pl.pallas_call(
        _qkprep_kernel,
        grid=(m // tm,),
        in_specs=[row(G_AQ), row(G_AK), row(G_AV), vec, vec, tab, tab, tab],
        out_specs=[out, out, out, out, out],
        out_shape=[jax.ShapeDtypeStruct((m, GROUP_W), BF16), jax.ShapeDtypeStruct((m, GROUP_W), F32),
                   jax.ShapeDtypeStruct((m, GROUP_W), BF16), jax.ShapeDtypeStruct((m, GROUP_W), F32),
                   jax.ShapeDtypeStruct((m, GROUP_W), BF16)],
        compiler_params=_params(("parallel",)),
        name="qkprep",
    )(proj, proj, proj, qg128, kg128, cos_t, sin1_t, sin2_t)


def _attn_kernel(lam_ref, q_ref, kt_ref, v_ref, g_ref, o_ref, m1, l1, a1, m2, l2, a2,
                 *, tq, tk, q_off, nk, out_scale):
    qi = pl.program_id(2)
    q_first = q_off + qi * tq
    first_chunk_end = (q_first // CHUNK) * CHUNK + CHUNK
    last_vis = ((q_first + tq - 1) // CHUNK) * CHUNK + CHUNK - 1
    n_blocks = jnp.minimum(nk, last_vis // tk + 1)
    n_full = jnp.minimum(n_blocks, first_chunk_end // tk)

    for m, l, a in ((m1, l1, a1), (m2, l2, a2)):
        m[...] = jnp.full(m.shape, NEG_BIG, F32)
        l[...] = jnp.zeros(l.shape, F32)
        a[...] = jnp.zeros(a.shape, F32)

    q = q_ref[0]
    lane = lax.broadcasted_iota(jnp.int32, q.shape, 1)
    zero = jnp.zeros_like(q)
    q_halves = (jnp.where(lane < A_DQK, q, zero), jnp.where(lane >= A_DQK, q, zero))

    def step(kj, masked):
        if nk == 1:
            k0, kt, v = 0, kt_ref[0], v_ref[0]
        else:
            k0 = pl.multiple_of(kj * tk, tk)
            kt = kt_ref[0, :, pl.ds(k0, tk)]
            v = v_ref[0, pl.ds(k0, tk), :]
        if masked:
            qpos = q_first + lax.broadcasted_iota(jnp.int32, (tq, tk), 0)
            kpos = k0 + lax.broadcasted_iota(jnp.int32, (tq, tk), 1)
            vis = (kpos // CHUNK) <= (qpos // CHUNK)
        for qh, (m, l, a) in zip(q_halves, ((m1, l1, a1), (m2, l2, a2))):
            s = _dot(qh, kt)
            if masked:
                s = jnp.where(vis, s, NEG_BIG)
            m_prev = m[...]
            m_new = jnp.maximum(m_prev, jnp.max(s, axis=1, keepdims=True))
            pr = jnp.exp2(s - m_new[:, :1])
            alpha = jnp.exp2(m_prev - m_new)
            l[...] = alpha * l[...] + jnp.sum(pr, axis=1, keepdims=True)
            a[...] = alpha * a[...] + _dot(pr.astype(BF16), v)
            m[...] = m_new

    def full_body(kj, carry):
        step(kj, False)
        return carry

    def masked_body(kj, carry):
        step(kj, True)
        return carry

    lax.fori_loop(0, n_full, full_body, 0)
    lax.fori_loop(n_full, n_blocks, masked_body, 0)

    lam = lam_ref[0]
    o = a1[...] / l1[...] - lam * (a2[...] / l2[...])
    ms = jnp.mean(o * o, axis=-1, keepdims=True)
    o_ref[0] = o * lax.rsqrt(ms + EPS) * g_ref[...] * out_scale


def _attn(q16, k16t, v16, lam, subln_g, q_off, out_scale):
    b, tq_len, _ = q16.shape
    tk_len = v16.shape[1]
    tq = min(512, tq_len)
    tk = tq if tk_len > 2048 else tk_len
    assert tq_len % tq == 0 and tk_len % tk == 0
    nq, nk = tq_len // tq, tk_len // tk
    kern = functools.partial(_attn_kernel, tq=tq, tk=tk, q_off=q_off, nk=nk, out_scale=out_scale)
    acc = pltpu.VMEM((tq, LANES), F32)
    return pl.pallas_call(
        kern,
        grid=(b, A_HEADS, nq),
        in_specs=[pl.BlockSpec(memory_space=pltpu.SMEM),
                  pl.BlockSpec((1, tq, LANES), lambda bi, h, qi: (bi, qi, h)),
                  pl.BlockSpec((1, LANES, tk_len), lambda bi, h, qi: (bi, h, 0)),
                  pl.BlockSpec((1, tk_len, LANES), lambda bi, h, qi: (bi, 0, h)),
                  pl.BlockSpec((1, LANES), lambda bi, h, qi: (0, 0))],
        out_specs=pl.BlockSpec((1, tq, LANES), lambda bi, h, qi: (bi, qi, h)),
        out_shape=jax.ShapeDtypeStruct((b, tq_len, GROUP_W), F32),
        scratch_shapes=[acc, acc, acc, acc, acc, acc],
        compiler_params=_params(("parallel", "parallel", "parallel")),
        name="attn",
    )(lam, q16, k16t, v16, subln_g)


def _hgrn_kernel(q_ref, f_ref, i_ref, lb_ref, g_ref, s0_ref, o_ref, sfin_ref,
                 st_scr, cum_scr, qk_scr, oi_scr):
    c = pl.program_id(1)

    @pl.when(c == 0)
    def _():
        st_scr[...] = s0_ref[0]

    z = f_ref[...]
    lb = lb_ref[...]
    log_f = jnp.log(lb + (1.0 - lb) * _sigmoid(z))
    k_in = (1.0 - lb) * _sigmoid(-z)
    q = _silu(q_ref[...])
    cum = _cumsum_rows(log_f, _tri_incl(CHUNK))
    row = lax.broadcasted_iota(jnp.int32, (CHUNK, 1), 0)
    head_slices = [slice(h * LANES, (h + 1) * LANES) for h in range(B_HEADS)]

    rel = cum - cum[CHUNK // 2 - 1:CHUNK // 2, :]
    safe = jnp.max(jnp.abs(rel)) <= HGRN_SAFE_SPAN

    @pl.when(safe)
    def _():
        qe = (q * jnp.exp(rel)).astype(BF16)
        ke = (k_in * jnp.exp(-rel)).astype(BF16)
        causal = lax.broadcasted_iota(jnp.int32, (CHUNK, CHUNK), 1) <= lax.broadcasted_iota(
            jnp.int32, (CHUNK, CHUNK), 0)
        scores = [jnp.where(causal, _dot_nt(qe[:, sl], ke[:, sl]), 0.0).astype(BF16) for sl in head_slices]
        i16 = i_ref[...].astype(BF16)
        for sl, sc in zip(head_slices, scores):
            oi_scr[:, sl] = _dot(sc, i16[:, sl])

    @pl.when(jnp.logical_not(safe))
    def _():
        cum_scr[...] = cum
        qk_scr[...] = q
        for h in range(B_HEADS):
            sl = slice(h * LANES, (h + 1) * LANES)
            cum_h = cum[:, sl]
            kin_h = k_in[:, sl]
            i_h = i_ref[:, sl]

            def body(g, carry, sl=sl, cum_h=cum_h, kin_h=kin_h, i_h=i_h):
                g8 = pl.multiple_of(g * SUBLANES, SUBLANES)
                c_tile = cum_scr[pl.ds(g8, SUBLANES), sl]
                q_tile = qk_scr[pl.ds(g8, SUBLANES), sl]
                rows = []
                for r in range(SUBLANES):
                    dec = jnp.exp(jnp.minimum(c_tile[r:r + 1] - cum_h, 0.0))
                    col = jnp.sum(dec * (kin_h * q_tile[r:r + 1]), axis=1, keepdims=True)
                    col = jnp.where(row <= g8 + r, col, 0.0)
                    rows.append(jnp.sum(col * i_h, axis=0, keepdims=True))
                oi_scr[pl.ds(g8, SUBLANES), sl] = jnp.concatenate(rows, axis=0)
                return carry

            lax.fori_loop(0, CHUNK // SUBLANES, body, 0)

    cum_last = cum[CHUNK - 1:CHUNK, :]
    q_dec = (q * jnp.exp(cum)).astype(BF16)
    k_tail = (k_in * jnp.exp(cum_last - cum)).astype(BF16)
    i16 = i_ref[...].astype(BF16)
    decay = jnp.exp(cum_last)
    states = [st_scr[h] for h in range(B_HEADS)]
    inter = [_dot_nt(q_dec[:, sl], st.astype(BF16)) for sl, st in zip(head_slices, states)]
    update = [_dot_tn(i16[:, sl], k_tail[:, sl]) for sl in head_slices]
    for h, sl in enumerate(head_slices):
        st_scr[h] = states[h] * decay[:, sl] + update[h]
        o = oi_scr[:, sl] + inter[h]
        ms = jnp.mean(o * o, axis=-1, keepdims=True)
        o_ref[:, sl] = o * lax.rsqrt(ms + EPS) * g_ref[...]

    sfin_ref[0] = st_scr[...]


def _hgrn(proj, lb, norm_g128, s0_t, bsz, nc):
    m = proj.shape[0]
    row = lambda g: pl.BlockSpec((CHUNK, GROUP_W), lambda b, c, g=g: (b * nc + c, g))
    st_spec = pl.BlockSpec((1, B_HEADS, LANES, LANES), lambda b, c: (b, 0, 0, 0))
    return pl.pallas_call(
        _hgrn_kernel,
        grid=(bsz, nc),
        in_specs=[row(G_BQ), row(G_BF), row(G_BI),
                  pl.BlockSpec((1, GROUP_W), lambda b, c: (0, 0)),
                  pl.BlockSpec((1, LANES), lambda b, c: (0, 0)),
                  st_spec],
        out_specs=[pl.BlockSpec((CHUNK, GROUP_W), lambda b, c: (b * nc + c, 0)), st_spec],
        out_shape=[jax.ShapeDtypeStruct((m, GROUP_W), F32),
                   jax.ShapeDtypeStruct((bsz, B_HEADS, LANES, LANES), F32)],
        scratch_shapes=[pltpu.VMEM((B_HEADS, LANES, LANES), F32),
                        pltpu.VMEM((CHUNK, GROUP_W), F32),
                        pltpu.VMEM((CHUNK, GROUP_W), F32),
                        pltpu.VMEM((CHUNK, GROUP_W), F32)],
        compiler_params=_params(("parallel", "arbitrary")),
        name="hgrn",
    )(proj, proj, proj, lb, norm_g128, s0_t)


def _rwkprep_kernel(*refs, has_vres):
    (cr_ref, ck_ref, cv_ref, clo_ref, spr_ref, spk_ref, spv_ref, splo_ref,
     mur_ref, muk_ref, muv_ref, mulo_ref, w0_ref, w2h_ref, w2l_ref, a0_ref, a2h_ref, a2l_ref,
     kk_ref, ka_ref, rk_ref) = refs[:21]
    pos = 21
    if has_vres:
        vres_ref, vf_ref, v0_ref, vw2h_ref, vw2l_ref = refs[pos:pos + 5]
        pos += 5
    (at_ref, rt_ref, bh_ref, kh_ref, vc_ref, bonus_ref, gam_ref, nab_ref, arb_ref, g_ref, y0_ref,
     shr_ref, shk_ref, shv_ref, shlo_ref) = refs[pos:pos + 15]
    pr_scr, pk_scr, pv_scr, plo_scr = refs[pos + 15:]
    c = pl.program_id(1)

    @pl.when(c == 0)
    def _():
        pr_scr[...] = spr_ref[0]
        pk_scr[...] = spk_ref[0]
        pv_scr[...] = spv_ref[0]
        plo_scr[...] = splo_ref[0]

    def shifted(x_ref, prev_scr, mu_ref, last_ref):
        x = x_ref[...]
        row = lax.broadcasted_iota(jnp.int32, x.shape, 0)
        prev = jnp.where(row == 0, prev_scr[...], pltpu.roll(x, 1, 0))
        last = x[CHUNK - 1:CHUNK, :]
        prev_scr[...] = last
        last_ref[0] = last
        return x + (prev - x) * mu_ref[...]

    r = shifted(cr_ref, pr_scr, mur_ref, shr_ref)
    k0 = shifted(ck_ref, pk_scr, muk_ref, shk_ref)
    v = shifted(cv_ref, pv_scr, muv_ref, shv_ref)
    lo = shifted(clo_ref, plo_scr, mulo_ref, shlo_ref)

    def lowrank(x, wh_ref, wl_ref):
        xh, xl = _split2(x)
        return _dot(xh, wh_ref[...]) + _dot(xl, wh_ref[...]) + _dot(xh, wl_ref[...])

    w_in = w0_ref[...] + lowrank(jnp.tanh(lo), w2h_ref, w2l_ref)
    nw = -w_in
    softplus = jnp.maximum(nw, 0.0) + jnp.log(1.0 + jnp.exp(-jnp.abs(nw)))
    log_decay = -jnp.exp(-softplus - 0.5)
    a_sig = _sigmoid(a0_ref[...] + lowrank(lo, a2h_ref, a2l_ref))
    if has_vres:
        v_mix = _sigmoid(v0_ref[...] + lowrank(vres_ref[...], vw2h_ref, vw2l_ref))
        v = v + (vf_ref[...] - v) * v_mix
    vc_ref[...] = v

    ones = _seg_ones(LANES, C_DH)
    cum = _cumsum_rows(log_decay, _tri_incl(CHUNK))
    cum_last = cum[CHUNK - 1:CHUNK, :]
    gam_ref[0] = jnp.exp(cum_last)
    e_prev = jnp.exp(cum - log_decay)
    e_cum = jnp.exp(cum)
    e_inv = jnp.exp(-cum)
    e_tail = jnp.exp(cum_last - cum)

    lane = lax.broadcasted_iota(jnp.int32, (CHUNK, LANES), 1)
    par1 = lane >= C_DH
    tcol = lax.broadcasted_iota(jnp.int32, (CHUNK, LANES), 0)
    scol = lane % C_DH
    strict = jnp.concatenate([scol < tcol, scol < tcol], axis=1)
    incl = jnp.concatenate([scol <= tcol, scol <= tcol], axis=1)
    low_mask = jnp.concatenate([strict, incl], axis=0)

    pairs = [slice(p * LANES, (p + 1) * LANES) for p in range(C_WIDTH // LANES)]

    def head_sums(x):
        stacked = _segsum(jnp.concatenate([x[:, sl] for sl in pairs], axis=0), ones)
        return jnp.concatenate([stacked[p * CHUNK:(p + 1) * CHUNK] for p in range(len(pairs))], axis=1)

    kk = k0 * kk_ref[...]
    kk = kk / jnp.maximum(jnp.sqrt(head_sums(kk * kk)), 1e-12)
    k = k0 * (1.0 + (a_sig - 1.0) * ka_ref[...])
    b_vec = kk * a_sig
    bonus_ref[...] = head_sums(r * k * rk_ref[...]) * v
    a_t = -kk * e_prev
    r_t = r * e_cum
    at_ref[...] = a_t
    rt_ref[...] = r_t
    bh_ref[...] = b_vec * e_tail
    kh_ref[...] = k * e_tail
    a16, r16 = a_t.astype(BF16), r_t.astype(BF16)
    b16, k16, v16 = (b_vec * e_inv).astype(BF16), (k * e_inv).astype(BF16), v.astype(BF16)
    zero = jnp.zeros((CHUNK, LANES), BF16)

    def block_diag(x):
        return [jnp.where(par1, zero, x), jnp.where(par1, x, zero)]

    prods = [jnp.where(low_mask,
                       _dot_nt(jnp.concatenate([a16[:, sl], r16[:, sl]], axis=0),
                               jnp.concatenate(block_diag(b16[:, sl]) + block_diag(k16[:, sl]), axis=0)),
                       0.0) for sl in pairs]
    gys = [_dot(prod[:, LANES:].astype(BF16), jnp.concatenate(block_diag(v16[:, sl]), axis=0))
           for sl, prod in zip(pairs, prods)]
    for sl, prod, gy in zip(pairs, prods, gys):
        nab_ref[:, sl] = prod[:CHUNK, :LANES]
        arb_ref[:, sl] = prod[CHUNK:, :LANES]
        g_ref[:, sl] = gy[:CHUNK]
        y0_ref[:, sl] = gy[CHUNK:]


def _rwkprep(proj, shift_parts, params, vres, bsz, nc):
    m = proj.shape[0]
    has_vres = vres is not None
    row = lambda g: pl.BlockSpec((CHUNK, GROUP_W), lambda b, c, g=g: (b * nc + c, g))
    lo_spec = pl.BlockSpec((CHUNK, LANES), lambda b, c: (b * nc + c, COL_LO // LANES))
    st = lambda w: pl.BlockSpec((1, 1, w), lambda b, c: (b, 0, 0))
    vec = lambda w: pl.BlockSpec((1, w), lambda b, c: (0, 0))
    mat = lambda: pl.BlockSpec((LANES, GROUP_W), lambda b, c: (0, 0))
    out = pl.BlockSpec((CHUNK, GROUP_W), lambda b, c: (b * nc + c, 0))
    in_specs = [row(G_CR), row(G_CK), row(G_CV), lo_spec, st(GROUP_W), st(GROUP_W), st(GROUP_W), st(LANES),
                vec(GROUP_W), vec(GROUP_W), vec(GROUP_W), vec(LANES),
                vec(GROUP_W), mat(), mat(), vec(GROUP_W), mat(), mat(),
                vec(GROUP_W), vec(GROUP_W), vec(GROUP_W)]
    args = [proj, proj, proj, proj, *shift_parts,
            params["mu_r"], params["mu_k"], params["mu_v"], params["mu_lo"],
            params["w0"], params["w2h"], params["w2l"], params["a0"], params["a2h"], params["a2l"],
            params["k_k"], params["k_a"], params["r_k"]]
    if has_vres:
        in_specs += [pl.BlockSpec((CHUNK, LANES), lambda b, c: (b * nc + c, COL_VRES // LANES)),
                     out, vec(GROUP_W), mat(), mat()]
        args += [proj, vres["v_first"], vres["v0"], vres["w2h"], vres["w2l"]]
    big = jax.ShapeDtypeStruct((m, GROUP_W), F32)
    out_shape = [big] * 6 + [jax.ShapeDtypeStruct((bsz * nc, 1, GROUP_W), F32)] + [big] * 4 + [
        jax.ShapeDtypeStruct((bsz, 1, GROUP_W), F32)] * 3 + [jax.ShapeDtypeStruct((bsz, 1, LANES), F32)]
    out_specs = [out] * 6 + [pl.BlockSpec((1, 1, GROUP_W), lambda b, c: (b * nc + c, 0, 0))] + [out] * 4 + [
        st(GROUP_W)] * 3 + [st(LANES)]
    return pl.pallas_call(
        functools.partial(_rwkprep_kernel, has_vres=has_vres),
        grid=(bsz, nc),
        in_specs=in_specs,
        out_specs=out_specs,
        out_shape=out_shape,
        scratch_shapes=[pltpu.VMEM((1, GROUP_W), F32)] * 3 + [pltpu.VMEM((1, LANES), F32)],
        compiler_params=_params(("parallel", "arbitrary")),
        name="rwkprep",
    )(*args)


def _rwksolve_kernel(n_ref, ra_ref, rg_ref, w_ref, u_ref):
    def group_body(tg, carry):
        t0 = pl.multiple_of(tg * SUBLANES, SUBLANES)
        for r in range(SUBLANES):
            t = t0 + r

            def s_body(sg, acc, t=t):
                s0 = pl.multiple_of(sg * SUBLANES, SUBLANES)
                coef = n_ref[t, pl.ds(s0, SUBLANES), :]
                acc_w, acc_u = acc
                for q in range(SUBLANES):
                    acc_w = acc_w + coef[q:q + 1] * w_ref[s0 + q]
                    acc_u = acc_u + coef[q:q + 1] * u_ref[s0 + q]
                return acc_w, acc_u

            acc_w, acc_u = lax.fori_loop(0, tg, s_body, (ra_ref[t], rg_ref[t]))
            coef = n_ref[t, pl.ds(t0, SUBLANES), :]
            for q in range(r):
                acc_w = acc_w + coef[q:q + 1] * w_ref[t0 + q]
                acc_u = acc_u + coef[q:q + 1] * u_ref[t0 + q]
            w_ref[t] = acc_w
            u_ref[t] = acc_u
        return carry

    lax.fori_loop(0, CHUNK // SUBLANES, group_body, 0)


def _rwksolve(n_bl, ra_bl, rg_bl):
    n_inst = n_bl.shape[-1]
    spec = pl.BlockSpec((CHUNK, C_DH, LANES), lambda i: (0, 0, i))
    shape = jax.ShapeDtypeStruct(ra_bl.shape, F32)
    return pl.pallas_call(
        _rwksolve_kernel,
        grid=(n_inst // LANES,),
        in_specs=[spec, spec, spec],
        out_specs=[spec, spec],
        out_shape=[shape, shape],
        compiler_params=_params(("parallel",)),
        name="rwksolve",
    )(n_bl, ra_bl, rg_bl)


def _rwkseq_kernel(w_ref, u0_ref, rt_ref, arb_ref, y0_ref, v_ref, bh_ref, kh_ref, gam_ref, bonus_ref,
                   lnw_ref, lnb_ref, s0_ref, o_ref, sfin_ref, st_scr):
    c = pl.program_id(1)

    @pl.when(c == 0)
    def _():
        st_scr[...] = s0_ref[0]

    ones = _seg_ones(LANES, C_DH)
    lane = lax.broadcasted_iota(jnp.int32, (CHUNK, LANES), 1)
    par1 = lane >= C_DH
    bd = _seg_mask(LANES, C_DH)
    pairs = [slice(p * LANES, (p + 1) * LANES) for p in range(C_WIDTH // LANES)]
    states = [st_scr[p] for p in range(len(pairs))]
    st16 = [st.astype(BF16) for st in states]
    w16 = w_ref[...].astype(BF16)
    rt16 = rt_ref[...].astype(BF16)
    arb16 = arb_ref[...].astype(BF16)
    zero = jnp.zeros((CHUNK, LANES), F32)
    u = [_dot_nt(w16[:, sl], st) + u0_ref[:, sl] for sl, st in zip(pairs, st16)]
    y_state = [_dot_nt(rt16[:, sl], st) for sl, st in zip(pairs, st16)]
    ubd = [jnp.concatenate([jnp.where(par1, zero, x), jnp.where(par1, x, zero)], axis=0).astype(BF16) for x in u]
    y_u = [_dot(arb16[:, sl], x) for sl, x in zip(pairs, ubd)]
    v16 = v_ref[...].astype(BF16)
    bh16 = bh_ref[...].astype(BF16)
    kh16 = kh_ref[...].astype(BF16)
    upd = [_dot_tn(jnp.concatenate([x.astype(BF16), v16[:, sl]], axis=0),
                   jnp.concatenate([bh16[:, sl], kh16[:, sl]], axis=0)) for sl, x in zip(pairs, u)]
    for p, sl in enumerate(pairs):
        st_scr[p] = states[p] * gam_ref[0, :, sl] + jnp.where(bd, upd[p], 0.0)
    y = jnp.concatenate([a + b + y0_ref[:, sl] for sl, a, b in zip(pairs, y_state, y_u)], axis=0)
    d = y - _segsum(y, ones) * (1.0 / C_DH)
    var = _segsum(d * d, ones) * (1.0 / C_DH)
    dn = d * lax.rsqrt(var + C_GN_EPS)
    for p, sl in enumerate(pairs):
        o_ref[:, sl] = dn[p * CHUNK:(p + 1) * CHUNK] * lnw_ref[:, sl] + lnb_ref[:, sl] + bonus_ref[:, sl]

    sfin_ref[0] = st_scr[...]


def _rwkseq(w, u0, rt, arb, y0, v, bh, kh, gam, bonus, ln_w, ln_b, s0_bd, bsz, nc):
    m = w.shape[0]
    row = pl.BlockSpec((CHUNK, GROUP_W), lambda b, c: (b * nc + c, 0))
    vec = pl.BlockSpec((1, GROUP_W), lambda b, c: (0, 0))
    st_spec = pl.BlockSpec((1, C_WIDTH // LANES, LANES, LANES), lambda b, c: (b, 0, 0, 0))
    return pl.pallas_call(
        _rwkseq_kernel,
        grid=(bsz, nc),
        in_specs=[row] * 8 + [pl.BlockSpec((1, 1, GROUP_W), lambda b, c: (b * nc + c, 0, 0)), row, vec, vec,
                  st_spec],
        out_specs=[row, st_spec],
        out_shape=[jax.ShapeDtypeStruct((m, GROUP_W), F32),
                   jax.ShapeDtypeStruct((bsz, C_WIDTH // LANES, LANES, LANES), F32)],
        scratch_shapes=[pltpu.VMEM((C_WIDTH // LANES, LANES, LANES), F32)],
        compiler_params=_params(("parallel", "arbitrary")),
        name="rwkseq",
    )(w, u0, rt, arb, y0, v, bh, kh, gam, bonus, ln_w, ln_b, s0_bd)


def _merge_kernel(x_ref, oa_ref, ob_ref, oc_ref, ag_ref, bg_ref, cg_ref, ma_ref, mb_ref, mc_ref,
                  wa_ref, wb_ref, wc_ref, wo_ref, y_ref):
    def branch(o_ref, gate_ref, w_ref):
        return _dot((o_ref[...] * _silu(gate_ref[...])).astype(BF16), w_ref[...])

    merged = (_sigmoid(ma_ref[...]) * branch(oa_ref, ag_ref, wa_ref)
              + _sigmoid(mb_ref[...]) * branch(ob_ref, bg_ref, wb_ref)
              + _sigmoid(mc_ref[...]) * branch(oc_ref, cg_ref, wc_ref))
    y_ref[...] = x_ref[...] + _dot(merged.astype(BF16), wo_ref[...])


def _merge(x2d, o_a, o_b, o_c, proj, wa, wb, wc, wo):
    m = x2d.shape[0]
    tm = min(256, m)
    row = pl.BlockSpec((tm, GROUP_W), lambda i: (i, 0))
    grp = lambda g: pl.BlockSpec((tm, GROUP_W), lambda i, g=g: (i, g))
    wsp = pl.BlockSpec((GROUP_W, D_MODEL), lambda i: (0, 0))
    return pl.pallas_call(
        _merge_kernel,
        grid=(m // tm,),
        in_specs=[row, row, row, row, grp(G_AG), grp(G_BG), grp(G_CG), grp(G_MA), grp(G_MB), grp(G_MC),
                  wsp, wsp, wsp, wsp],
        out_specs=row,
        out_shape=jax.ShapeDtypeStruct((m, D_MODEL), F32),
        compiler_params=_params(("parallel",)),
        name="merge",
    )(x2d, o_a, o_b, o_c, proj, proj, proj, proj, proj, proj, wa, wb, wc, wo)


_C_OFF = 8 * GROUP_W
_CP_R = (_C_OFF, _C_OFF + C_WIDTH)
_CP_WLO = (_CP_R[1], _CP_R[1] + C_DECAY_RANK)
_CP_K = (_CP_WLO[1], _CP_WLO[1] + C_WIDTH)
_CP_V = (_CP_K[1], _CP_K[1] + C_WIDTH)
_CP_ALO = (_CP_V[1], _CP_V[1] + C_A_RANK)
_REST = _CP_ALO[1]


def _regroup_w_in(w, vres_w1):
    cols = [w[:, :_C_OFF], w[:, _CP_R[0]:_CP_R[1]], w[:, _CP_K[0]:_CP_K[1]], w[:, _CP_V[0]:_CP_V[1]],
            w[:, _REST:], w[:, _CP_WLO[0]:_CP_WLO[1]], w[:, _CP_ALO[0]:_CP_ALO[1]]]
    vres = jnp.zeros((D_MODEL, LANES), F32)
    if vres_w1 is not None:
        vres = vres.at[:, :C_VRES_RANK].set(vres_w1)
    cols.append(vres)
    cols.append(jnp.zeros((D_MODEL, N_PROJ_PAD - N_PROJ), F32))
    return jnp.concatenate(cols, axis=1).astype(BF16)


def _split_shift(s):
    o = _C_OFF
    part = lambda a: s[:, a[0] - o:a[1] - o]
    lo = jnp.concatenate([part(_CP_WLO), part(_CP_ALO)], axis=1)
    return [x[:, None, :] for x in (part(_CP_R), part(_CP_K), part(_CP_V), lo)]


def _join_shift(r, k, v, lo):
    r, k, v, lo = (x[:, 0, :] for x in (r, k, v, lo))
    return jnp.concatenate([r, lo[:, :C_DECAY_RANK], k, v, lo[:, C_DECAY_RANK:]], axis=1)


def _pad_rows(w, row0):
    out = jnp.zeros((LANES, w.shape[1]), F32).at[row0:row0 + w.shape[0]].set(w)
    hi = out.astype(BF16)
    return hi, (out - hi.astype(F32)).astype(BF16)


def _rope_tables(pos):
    half = A_ROT // 2
    inv_freq = ROPE_THETA ** (-(jnp.arange(half, dtype=F32) * (2.0 / A_ROT)))
    ang = pos.astype(F32)[:, None] * inv_freq[None, :]
    cos, sin = jnp.cos(ang), jnp.sin(ang)
    t = pos.shape[0]
    one = jnp.ones((t, A_DQK - A_ROT), F32)
    zero = jnp.zeros((t, A_DQK - A_ROT), F32)
    z8 = jnp.zeros((t, half), F32)
    c64 = jnp.concatenate([cos, cos, one], axis=1)
    s1 = jnp.concatenate([-sin, z8, zero], axis=1)
    s2 = jnp.concatenate([z8, sin, zero], axis=1)
    tile = lambda x: jnp.concatenate([x, x], axis=1)
    return tile(c64), tile(s1), tile(s2)


def _to_lanes(x, rows):
    return x.reshape(rows, CHUNK, C_HEADS, C_DH).transpose(1, 3, 0, 2).reshape(CHUNK, C_DH, rows * C_HEADS)


def _from_lanes(x, rows):
    return x.reshape(CHUNK, C_DH, rows, C_HEADS).transpose(2, 0, 3, 1).reshape(rows * CHUNK, C_WIDTH)


def _layer(l, x2d, bsz, t_len, q_off, P, lb, past, v_first):
    nc = t_len // CHUNK
    m = bsz * t_len
    w_in = _regroup_w_in(P["w_in"][l], P["c_vres_w1"][l - 1] if l > 0 else None)
    proj = _proj(x2d, P["norm_g"][l][None, :], w_in)

    pos = q_off + jnp.arange(t_len, dtype=jnp.int32)
    cos_t, sin1_t, sin2_t = _rope_tables(pos)
    tile2 = lambda g: jnp.concatenate([g, g])[None, :]
    q16, k32, k16, v32, v16 = _qkprep(proj, tile2(P["a_qnorm_g"][l]), tile2(P["a_knorm_g"][l]),
                                      cos_t, sin1_t, sin2_t, t_len)
    q16 = q16.reshape(bsz, t_len, GROUP_W)
    k16 = k16.reshape(bsz, t_len, GROUP_W)
    v16 = v16.reshape(bsz, t_len, GROUP_W)
    if past is not None:
        pk, pv = past[0][l], past[1][l]
        p_len = pk.shape[1]
        k16 = jnp.concatenate([pk.reshape(bsz, p_len, GROUP_W).astype(BF16), k16], axis=1)
        v16 = jnp.concatenate([pv.reshape(bsz, p_len, GROUP_W).astype(BF16), v16], axis=1)
    k16t = jnp.swapaxes(k16, 1, 2)
    lam_init = 0.8 - 0.6 * math.exp(-0.3 * l)
    lp = P["a_lambda"][l].astype(F32)
    lam = jnp.exp(jnp.sum(lp[0] * lp[1])) - jnp.exp(jnp.sum(lp[2] * lp[3])) + lam_init
    o_a = _attn(q16, k16t, v16, lam.reshape(1), P["a_subln_g"][l][None, :], q_off, 1.0 - lam_init)
    o_a = o_a.reshape(m, GROUP_W)

    if past is None:
        s_h = jnp.zeros((bsz, B_HEADS, LANES, LANES), F32)
    else:
        s_h = jnp.swapaxes(past[2][l].astype(F32), -1, -2)
    o_b, s_h_new = _hgrn(proj, lb[l][None, :], P["b_norm_g"][l][None, :], s_h, bsz, nc)
    s_h_new = jnp.swapaxes(s_h_new, -1, -2)

    if past is None:
        shift_prev = jnp.zeros((bsz, 3 * C_WIDTH + C_DECAY_RANK + C_A_RANK), F32)
        s_r = jnp.zeros((bsz, C_HEADS, C_DH, C_DH), F32)
    else:
        shift_prev, s_r = past[4][l], past[3][l].astype(F32)
    mu = _split_shift(P["c_shift_mu"][l][None, :])
    w2h, w2l = _pad_rows(P["c_w2"][l], 0)
    a2h, a2l = _pad_rows(P["c_a2"][l], C_DECAY_RANK)
    cparams = {"mu_r": mu[0][0], "mu_k": mu[1][0], "mu_v": mu[2][0], "mu_lo": mu[3][0],
               "w0": P["c_w0"][l][None, :], "w2h": w2h, "w2l": w2l,
               "a0": P["c_a0"][l][None, :], "a2h": a2h, "a2l": a2l,
               "k_k": P["c_k_k"][l][None, :], "k_a": P["c_k_a"][l][None, :],
               "r_k": P["c_r_k"][l].reshape(1, C_WIDTH)}
    vres = None
    if l > 0:
        vh, vl = _pad_rows(P["c_vres_w2"][l - 1], 0)
        vres = {"v_first": v_first, "v0": P["c_v0"][l - 1][None, :], "w2h": vh, "w2l": vl}
    (a_t, r_t, b_h, k_h, v_c, bonus, gam, nab, arb, g_mat, y0, sh_r, sh_k, sh_v, sh_lo) = _rwkprep(
        proj, _split_shift(shift_prev), cparams, vres, bsz, nc)
    rows = bsz * nc
    n_inst = rows * C_HEADS
    pad = (-n_inst) % LANES
    to_lanes = lambda x: jnp.pad(_to_lanes(x, rows), ((0, 0), (0, 0), (0, pad))) if pad else _to_lanes(x, rows)
    w_bl, u_bl = _rwksolve(to_lanes(nab), to_lanes(a_t), to_lanes(g_mat))
    w_mat = _from_lanes(w_bl[:, :, :n_inst], rows)
    u0 = _from_lanes(u_bl[:, :, :n_inst], rows)
    eye2 = jnp.eye(2, dtype=F32)
    s_bd = jnp.einsum("bpqvk,qr->bpqvrk", s_r.reshape(bsz, C_HEADS // 2, 2, C_DH, C_DH), eye2)
    s_bd = s_bd.reshape(bsz, C_HEADS // 2, LANES, LANES)
    o_c, s_bd_new = _rwkseq(w_mat, u0, r_t, arb, y0, v_c, b_h, k_h, gam, bonus,
                            P["c_ln_w"][l][None, :], P["c_ln_b"][l][None, :], s_bd, bsz, nc)
    s_new6 = s_bd_new.reshape(bsz, C_HEADS // 2, 2, C_DH, 2, C_DH)
    s_r_new = jnp.stack([s_new6[:, :, 0, :, 0, :], s_new6[:, :, 1, :, 1, :]], axis=2)
    s_r_new = s_r_new.reshape(bsz, C_HEADS, C_DH, C_DH)
    shift_new = _join_shift(sh_r, sh_k, sh_v, sh_lo)

    bf = lambda w: w.astype(BF16)
    y = _merge(x2d, o_a, o_b, o_c, proj, bf(P["w_out_a"][l]), bf(P["w_out_b"][l]), bf(P["w_out_c"][l]),
               bf(P["w_o"][l]))
    k_rows = k32.reshape(bsz, t_len, A_HEADS, 2 * A_DQK)
    v_rows = v32.reshape(bsz, t_len, A_HEADS, 2 * A_DQK)
    return y, (k_rows, v_rows, s_h_new, s_r_new, shift_new), v_c


def _run_trunk(x, q_off, P, lb, past):
    bsz, t_len, _ = x.shape
    depth = P["w_in"].shape[0]
    x2d = x.reshape(bsz * t_len, D_MODEL)
    outs = ([], [], [], [], [])
    v_first = None
    for l in range(depth):
        x2d, entries, v_c = _layer(l, x2d, bsz, t_len, q_off, P, lb, past, v_first)
        if l == 0:
            v_first = v_c
        for lst, e in zip(outs, entries):
            lst.append(e)
    return x2d.reshape(bsz, t_len, D_MODEL), [jnp.stack(lst) for lst in outs]


def kernel(x_prompt, x_sample, cache_attn_k, cache_attn_v, state_hgrn, state_rwkv, state_rwkv_shift,
           norm_g, w_in, a_qnorm_g, a_knorm_g, a_lambda, a_subln_g, b_lower, b_norm_g,
           c_shift_mu, c_w0, c_w2, c_a0, c_a2, c_k_k, c_k_a, c_r_k, c_ln_w, c_ln_b,
           c_vres_w1, c_vres_w2, c_v0, w_out_a, w_out_b, w_out_c, w_o):
    P = {"norm_g": norm_g, "w_in": w_in, "a_qnorm_g": a_qnorm_g, "a_knorm_g": a_knorm_g,
         "a_lambda": a_lambda, "a_subln_g": a_subln_g, "b_norm_g": b_norm_g,
         "c_shift_mu": c_shift_mu, "c_w0": c_w0, "c_w2": c_w2, "c_a0": c_a0, "c_a2": c_a2,
         "c_k_k": c_k_k, "c_k_a": c_k_a, "c_r_k": c_r_k, "c_ln_w": c_ln_w, "c_ln_b": c_ln_b,
         "c_vres_w1": c_vres_w1, "c_vres_w2": c_vres_w2, "c_v0": c_v0,
         "w_out_a": w_out_a, "w_out_b": w_out_b, "w_out_c": w_out_c, "w_o": w_o}
    sm = jax.nn.softmax(b_lower.astype(F32), axis=0)
    lb = jnp.cumsum(sm, axis=0) - sm[0:1]
    past_len = cache_attn_k.shape[2]
    y_p, (k_p, v_p, hg_p, rw_p, sh_p) = _run_trunk(x_prompt, 0, P, lb, None)
    y_s, (k_s, v_s, hg_s, rw_s, sh_s) = _run_trunk(
        x_sample, past_len, P, lb, (cache_attn_k, cache_attn_v, state_hgrn, state_rwkv, state_rwkv_shift))
    return (y_p, y_s, k_p, v_p, hg_p, rw_p, sh_p, k_s, v_s, hg_s, rw_s, sh_s)
```

```python
import functools
import math

import jax
import jax.numpy as jnp
from jax import lax
from jax.experimental import pallas as pl
from jax.experimental.pallas import tpu as pltpu

F32 = jnp.float32
BF16 = jnp.bfloat16

D_MODEL = 1024
CHUNK = 64
EPS = 1e-6
NEG_BIG = -1e30
LOG2_E = 1.4426950408889634
A_HEADS = 8
A_DQK = 64
A_ROT = 16
ROPE_THETA = 500000.0
B_HEADS = 8
C_HEADS = 16
C_DH = 64
C_WIDTH = 1024
C_DECAY_RANK = 64
C_A_RANK = 64
C_VRES_RANK = 32
C_GN_EPS = 64e-5
ATTN_BOUND_LIMIT = 60.0
HGRN_SAFE_SPAN = 60.0
LANES = 128
SUBLANES = 8
GROUP_W = 1024
N_GROUPS = 15
COL_LO = N_GROUPS * GROUP_W
COL_VRES = COL_LO + LANES
N_PROJ = COL_VRES + LANES
PROJ_TN = 512
N_PROJ_PAD = -(-N_PROJ // PROJ_TN) * PROJ_TN
(G_AQ, G_AK, G_AV, G_AG, G_BQ, G_BF, G_BI, G_BG, G_CR, G_CK, G_CV, G_CG, G_MA, G_MB, G_MC) = range(N_GROUPS)
VMEM_LIMIT = 56 * 1024 * 1024


def _dot(a, b):
    return jnp.dot(a, b, preferred_element_type=F32)


def _dot_nt(a, b):
    return lax.dot_general(a, b, (((1,), (1,)), ((), ())), preferred_element_type=F32)


def _dot_tn(a, b):
    return lax.dot_general(a, b, (((0,), (0,)), ((), ())), preferred_element_type=F32)


def _split2(x):
    hi = x.astype(BF16)
    lo = (x - hi.astype(F32)).astype(BF16)
    return hi, lo


def _split3(x):
    hi = x.astype(BF16)
    r = x - hi.astype(F32)
    mid = r.astype(BF16)
    lo = (r - mid.astype(F32)).astype(BF16)
    return hi, mid, lo


def _sigmoid(x):
    return 1.0 / (1.0 + jnp.exp(-x))


def _silu(x):
    return x * _sigmoid(x)


def _seg_mask(n, seg):
    r = lax.broadcasted_iota(jnp.int32, (n, n), 0) // seg
    c = lax.broadcasted_iota(jnp.int32, (n, n), 1) // seg
    return r == c


def _seg_ones(n, seg):
    return _seg_mask(n, seg).astype(BF16)


def _segsum(x, ones_bf16):
    hi, lo = _split2(x)
    return _dot(hi, ones_bf16) + _dot(lo, ones_bf16)


def _tri_incl(n):
    r = lax.broadcasted_iota(jnp.int32, (n, n), 0)
    c = lax.broadcasted_iota(jnp.int32, (n, n), 1)
    return (c <= r).astype(BF16)


def _cumsum_rows(x, tri_bf16):
    hi, mid, lo = _split3(x)
    return _dot(tri_bf16, hi) + _dot(tri_bf16, mid) + _dot(tri_bf16, lo)


def _params(sem, vmem=None):
    return pltpu.CompilerParams(dimension_semantics=sem, vmem_limit_bytes=vmem or VMEM_LIMIT)


def _proj_kernel(x_ref, g_ref, w_ref, o_ref, h_scr):
    @pl.when(pl.program_id(1) == 0)
    def _():
        x = x_ref[...]
        ms = jnp.mean(x * x, axis=-1, keepdims=True)
        h_scr[...] = (x * lax.rsqrt(ms + EPS) * g_ref[...]).astype(BF16)

    o_ref[...] = _dot(h_scr[...], w_ref[...])


def _proj(x2d, g, w_bf16):
    m = x2d.shape[0]
    tm = min(2048, m)
    n = w_bf16.shape[1]
    return pl.pallas_call(
        _proj_kernel,
        grid=(m // tm, n // PROJ_TN),
        in_specs=[pl.BlockSpec((tm, D_MODEL), lambda i, j: (i, 0)),
                  pl.BlockSpec((1, D_MODEL), lambda i, j: (0, 0)),
                  pl.BlockSpec((D_MODEL, PROJ_TN), lambda i, j: (0, j))],
        out_specs=pl.BlockSpec((tm, PROJ_TN), lambda i, j: (i, j)),
        out_shape=jax.ShapeDtypeStruct((m, n), F32),
        scratch_shapes=[pltpu.VMEM((tm, D_MODEL), BF16)],
        compiler_params=_params(("parallel", "arbitrary")),
        name="proj",
    )(x2d, g, w_bf16)


def _qkprep_kernel(q_ref, k_ref, v_ref, qg_ref, kg_ref, c_ref, s1_ref, s2_ref,
                   q16t_ref, k32_ref, k16_ref, v32_ref, v16t_ref):
    ones = _seg_ones(LANES, A_DQK)
    cosv, sin1, sin2 = c_ref[...], s1_ref[...], s2_ref[...]

    def prep(x, gain):
        ss = _segsum(x * x, ones)
        y = x * lax.rsqrt(ss * (1.0 / A_DQK) + EPS) * gain
        return y * cosv + pltpu.roll(y, LANES - A_ROT // 2, 1) * sin1 + pltpu.roll(y, A_ROT // 2, 1) * sin2

    for c in range(GROUP_W // LANES):
        sl = slice(c * LANES, (c + 1) * LANES)
        q = prep(q_ref[:, sl], qg_ref[...])
        q16t_ref[0, sl, :] = (q * (A_DQK ** -0.5 * LOG2_E)).T.astype(BF16)
        k = prep(k_ref[:, sl], kg_ref[...])
        k32_ref[:, sl] = k
        k16_ref[:, sl] = k.astype(BF16)
        v = v_ref[:, sl]
        v32_ref[:, sl] = v
        v16t_ref[0, sl, :] = v.T.astype(BF16)


def _qkprep(proj, qg128, kg128, cos_t, sin1_t, sin2_t, t_len):
    m = proj.shape[0]
    tm = min(512, t_len)
    nt = t_len // tm
    row = lambda g: pl.BlockSpec((tm, GROUP_W), lambda i, g=g: (i, g))
    tab = pl.BlockSpec((tm, LANES), lambda i: (i % nt, 0))
    vec = pl.BlockSpec((1, LANES), lambda i: (0, 0))
    out = pl.BlockSpec((tm, GROUP_W), lambda i: (i, 0))
    out_t = pl.BlockSpec((1, GROUP_W, tm), lambda i: (i // nt, 0, i % nt))
    transposed = jax.ShapeDtypeStruct((m // t_len, GROUP_W, t_len), BF16)
    return pl.pallas_call(
        _qkprep_kernel,
        grid=(m // tm,),
        in_specs=[row(G_AQ), row(G_AK), row(G_AV), vec, vec, tab, tab, tab],
        out_specs=[out_t, out, out, out, out_t],
        out_shape=[transposed, jax.ShapeDtypeStruct((m, GROUP_W), F32),
                   jax.ShapeDtypeStruct((m, GROUP_W), BF16), jax.ShapeDtypeStruct((m, GROUP_W), F32),
                   transposed],
        compiler_params=_params(("parallel",)),
        name="qkprep",
    )(proj, proj, proj, qg128, kg128, cos_t, sin1_t, sin2_t)


def _attn_kernel(sc_ref, qt_ref, k_ref, vt_ref, g_ref, o_ref, m1, l1, a1, m2, l2, a2,
                 *, tq, tk, q_off, nk, out_scale, bounded):
    qi = pl.program_id(2)
    q_first = q_off + qi * tq
    first_chunk_end = (q_first // CHUNK) * CHUNK + CHUNK
    last_vis = ((q_first + tq - 1) // CHUNK) * CHUNK + CHUNK - 1
    n_blocks = jnp.minimum(nk, last_vis // tk + 1)
    n_full = jnp.minimum(n_blocks, first_chunk_end // tk)

    for m, l, a in ((m1, l1, a1), (m2, l2, a2)):
        m[...] = jnp.full(m.shape, NEG_BIG, F32)
        l[...] = jnp.zeros(l.shape, F32)
        a[...] = jnp.zeros(a.shape, F32)

    qt = qt_ref[0]
    dim = lax.broadcasted_iota(jnp.int32, qt.shape, 0)
    zero = jnp.zeros_like(qt)
    q_halves = (jnp.where(dim < A_DQK, qt, zero), jnp.where(dim >= A_DQK, qt, zero))

    def scores(kj, masked):
        if nk == 1:
            k0, k, vt = 0, k_ref[0], vt_ref[0]
        else:
            k0 = pl.multiple_of(kj * tk, tk)
            k = k_ref[0, pl.ds(k0, tk), :]
            vt = vt_ref[0, :, pl.ds(k0, tk)]
        vis = None
        if masked:
            kpos = k0 + lax.broadcasted_iota(jnp.int32, (tk, tq), 0)
            qpos = q_first + lax.broadcasted_iota(jnp.int32, (tk, tq), 1)
            vis = (kpos // CHUNK) <= (qpos // CHUNK)

        def score(qh):
            s = _dot(k, qh)
            return jnp.where(vis, s, NEG_BIG) if masked else s

        return score, vt

    def key_partial_sums(pr):
        return jnp.sum(pr.reshape(tk // SUBLANES, SUBLANES, tq), axis=0)

    def online_step(kj, masked):
        score, vt = scores(kj, masked)
        for qh, (m, l, a) in zip(q_halves, ((m1, l1, a1), (m2, l2, a2))):
            s = score(qh)
            m_prev = m[...]
            m_new = jnp.maximum(m_prev, jnp.max(s, axis=0, keepdims=True))
            pr = jnp.exp2(s - m_new)
            alpha = jnp.exp2(m_prev - m_new)
            l[...] = alpha * l[...] + key_partial_sums(pr)
            a[...] = alpha * a[...] + _dot(vt, pr.astype(BF16))
            m[...] = m_new

    def bounded_step(kj, masked):
        score, vt = scores(kj, masked)
        for qh, (m, l, a) in zip(q_halves, ((m1, l1, a1), (m2, l2, a2))):
            pr = jnp.exp2(score(qh) - bound)
            l[...] += key_partial_sums(pr)
            a[...] += _dot(vt, pr.astype(BF16))

    def run(step):
        def full_body(kj, carry):
            step(kj, False)
            return carry

        def masked_body(kj, carry):
            step(kj, True)
            return carry

        lax.fori_loop(0, n_full, full_body, 0)
        lax.fori_loop(n_full, n_blocks, masked_body, 0)

    lam = sc_ref[0]
    bound = sc_ref[1]
    if bounded:
        in_range = bound <= ATTN_BOUND_LIMIT

        @pl.when(in_range)
        def _():
            run(bounded_step)

        @pl.when(jnp.logical_not(in_range))
        def _():
            run(online_step)
    else:
        run(online_step)

    l1_tot = jnp.sum(l1[...], axis=0, keepdims=True)
    l2_tot = jnp.sum(l2[...], axis=0, keepdims=True)
    o = (a1[...] / l1_tot - lam * (a2[...] / l2_tot)).T
    ms = jnp.mean(o * o, axis=-1, keepdims=True)
    o_ref[0] = o * lax.rsqrt(ms + EPS) * g_ref[...] * out_scale


def _attn(q16t, k16, v16t, scalars, subln_g, q_off, out_scale, bounded):
    b, _, tq_len = q16t.shape
    tk_len = k16.shape[1]
    tq = min(512, tq_len)
    tk = tq if tk_len > 2048 else tk_len
    assert tq_len % tq == 0 and tk_len % tk == 0
    nq, nk = tq_len // tq, tk_len // tk
    kern = functools.partial(_attn_kernel, tq=tq, tk=tk, q_off=q_off, nk=nk, out_scale=out_scale,
                             bounded=bounded)
    run_max = pltpu.VMEM((1, tq), F32)
    key_sum = pltpu.VMEM((SUBLANES, tq), F32)
    acc = pltpu.VMEM((LANES, tq), F32)
    return pl.pallas_call(
        kern,
        grid=(b, A_HEADS, nq),
        in_specs=[pl.BlockSpec(memory_space=pltpu.SMEM),
                  pl.BlockSpec((1, LANES, tq), lambda bi, h, qi: (bi, h, qi)),
                  pl.BlockSpec((1, tk_len, LANES), lambda bi, h, qi: (bi, 0, h)),
                  pl.BlockSpec((1, LANES, tk_len), lambda bi, h, qi: (bi, h, 0)),
                  pl.BlockSpec((1, LANES), lambda bi, h, qi: (0, 0))],
        out_specs=pl.BlockSpec((1, tq, LANES), lambda bi, h, qi: (bi, qi, h)),
        out_shape=jax.ShapeDtypeStruct((b, tq_len, GROUP_W), F32),
        scratch_shapes=[run_max, key_sum, acc, run_max, key_sum, acc],
        compiler_params=_params(("parallel", "parallel", "parallel")),
        name="attn",
    )(scalars, q16t, k16, v16t, subln_g)


def _hgrn_kernel(q_ref, f_ref, i_ref, lb_ref, g_ref, s0_ref, o_ref, sfin_ref,
                 st_scr, cum_scr, qk_scr, oi_scr):
    c = pl.program_id(1)

    @pl.when(c == 0)
    def _():
        st_scr[...] = s0_ref[0]

    z = f_ref[...]
    lb = lb_ref[...]
    log_f = jnp.log(lb + (1.0 - lb) * _sigmoid(z))
    k_in = (1.0 - lb) * _sigmoid(-z)
    q = _silu(q_ref[...])
    cum = _cumsum_rows(log_f, _tri_incl(CHUNK))
    row = lax.broadcasted_iota(jnp.int32, (CHUNK, 1), 0)
    head_slices = [slice(h * LANES, (h + 1) * LANES) for h in range(B_HEADS)]

    rel = cum - cum[CHUNK // 2 - 1:CHUNK // 2, :]
    safe = jnp.max(jnp.abs(rel)) <= HGRN_SAFE_SPAN

    @pl.when(safe)
    def _():
        qe = (q * jnp.exp(rel)).astype(BF16)
        ke = (k_in * jnp.exp(-rel)).astype(BF16)
        causal = lax.broadcasted_iota(jnp.int32, (CHUNK, CHUNK), 1) <= lax.broadcasted_iota(
            jnp.int32, (CHUNK, CHUNK), 0)
        scores = [jnp.where(causal, _dot_nt(qe[:, sl], ke[:, sl]), 0.0).astype(BF16) for sl in head_slices]
        i16 = i_ref[...].astype(BF16)
        for sl, sc in zip(head_slices, scores):
            oi_scr[:, sl] = _dot(sc, i16[:, sl])

    @pl.when(jnp.logical_not(safe))
    def _():
        cum_scr[...] = cum
        qk_scr[...] = q
        for h in range(B_HEADS):
            sl = slice(h * LANES, (h + 1) * LANES)
            cum_h = cum[:, sl]
            kin_h = k_in[:, sl]
            i_h = i_ref[:, sl]

            def body(g, carry, sl=sl, cum_h=cum_h, kin_h=kin_h, i_h=i_h):
                g8 = pl.multiple_of(g * SUBLANES, SUBLANES)
                c_tile = cum_scr[pl.ds(g8, SUBLANES), sl]
                q_tile = qk_scr[pl.ds(g8, SUBLANES), sl]
                rows = []
                for r in range(SUBLANES):
                    dec = jnp.exp(jnp.minimum(c_tile[r:r + 1] - cum_h, 0.0))
                    col = jnp.sum(dec * (kin_h * q_tile[r:r + 1]), axis=1, keepdims=True)
                    col = jnp.where(row <= g8 + r, col, 0.0)
                    rows.append(jnp.sum(col * i_h, axis=0, keepdims=True))
                oi_scr[pl.ds(g8, SUBLANES), sl] = jnp.concatenate(rows, axis=0)
                return carry

            lax.fori_loop(0, CHUNK // SUBLANES, body, 0)

    cum_last = cum[CHUNK - 1:CHUNK, :]
    q_dec = (q * jnp.exp(cum)).astype(BF16)
    k_tail = (k_in * jnp.exp(cum_last - cum)).astype(BF16)
    i16 = i_ref[...].astype(BF16)
    decay = jnp.exp(cum_last)
    states = [st_scr[h] for h in range(B_HEADS)]
    inter = [_dot_nt(q_dec[:, sl], st.astype(BF16)) for sl, st in zip(head_slices, states)]
    update = [_dot_tn(i16[:, sl], k_tail[:, sl]) for sl in head_slices]
    for h, sl in enumerate(head_slices):
        st_scr[h] = states[h] * decay[:, sl] + update[h]
        o = oi_scr[:, sl] + inter[h]
        ms = jnp.mean(o * o, axis=-1, keepdims=True)
        o_ref[:, sl] = o * lax.rsqrt(ms + EPS) * g_ref[...]

    sfin_ref[0] = st_scr[...]


def _hgrn(proj, lb, norm_g128, s0_t, bsz, nc):
    m = proj.shape[0]
    row = lambda g: pl.BlockSpec((CHUNK, GROUP_W), lambda b, c, g=g: (b * nc + c, g))
    st_spec = pl.BlockSpec((1, B_HEADS, LANES, LANES), lambda b, c: (b, 0, 0, 0))
    return pl.pallas_call(
        _hgrn_kernel,
        grid=(bsz, nc),
        in_specs=[row(G_BQ), row(G_BF), row(G_BI),
                  pl.BlockSpec((1, GROUP_W), lambda b, c: (0, 0)),
                  pl.BlockSpec((1, LANES), lambda b, c: (0, 0)),
                  st_spec],
        out_specs=[pl.BlockSpec((CHUNK, GROUP_W), lambda b, c: (b * nc + c, 0)), st_spec],
        out_shape=[jax.ShapeDtypeStruct((m, GROUP_W), F32),
                   jax.ShapeDtypeStruct((bsz, B_HEADS, LANES, LANES), F32)],
        scratch_shapes=[pltpu.VMEM((B_HEADS, LANES, LANES), F32),
                        pltpu.VMEM((CHUNK, GROUP_W), F32),
                        pltpu.VMEM((CHUNK, GROUP_W), F32),
                        pltpu.VMEM((CHUNK, GROUP_W), F32)],
        compiler_params=_params(("parallel", "arbitrary")),
        name="hgrn",
    )(proj, proj, proj, lb, norm_g128, s0_t)


def _rwkprep_kernel(*refs, has_vres):
    (cr_ref, ck_ref, cv_ref, clo_ref, spr_ref, spk_ref, spv_ref, splo_ref,
     mur_ref, muk_ref, muv_ref, mulo_ref, w0_ref, w2h_ref, w2l_ref, a0_ref, a2h_ref, a2l_ref,
     kk_ref, ka_ref, rk_ref) = refs[:21]
    pos = 21
    if has_vres:
        vres_ref, vf_ref, v0_ref, vw2h_ref, vw2l_ref = refs[pos:pos + 5]
        pos += 5
    (at_ref, rt_ref, bh_ref, kh_ref, vc_ref, bonus_ref, gam_ref, nab_ref, arb_ref, g_ref, y0_ref,
     shr_ref, shk_ref, shv_ref, shlo_ref) = refs[pos:pos + 15]
    pr_scr, pk_scr, pv_scr, plo_scr = refs[pos + 15:]
    c = pl.program_id(1)

    @pl.when(c == 0)
    def _():
        pr_scr[...] = spr_ref[0]
        pk_scr[...] = spk_ref[0]
        pv_scr[...] = spv_ref[0]
        plo_scr[...] = splo_ref[0]

    def shifted(x_ref, prev_scr, mu_ref, last_ref):
        x = x_ref[...]
        row = lax.broadcasted_iota(jnp.int32, x.shape, 0)
        prev = jnp.where(row == 0, prev_scr[...], pltpu.roll(x, 1, 0))
        last = x[CHUNK - 1:CHUNK, :]
        prev_scr[...] = last
        last_ref[0] = last
        return x + (prev - x) * mu_ref[...]

    r = shifted(cr_ref, pr_scr, mur_ref, shr_ref)
    k0 = shifted(ck_ref, pk_scr, muk_ref, shk_ref)
    v = shifted(cv_ref, pv_scr, muv_ref, shv_ref)
    lo = shifted(clo_ref, plo_scr, mulo_ref, shlo_ref)

    def lowrank(x, wh_ref, wl_ref):
        xh, xl = _split2(x)
        return _dot(xh, wh_ref[...]) + _dot(xl, wh_ref[...]) + _dot(xh, wl_ref[...])

    w_in = w0_ref[...] + lowrank(jnp.tanh(lo), w2h_ref, w2l_ref)
    nw = -w_in
    softplus = jnp.maximum(nw, 0.0) + jnp.log(1.0 + jnp.exp(-jnp.abs(nw)))
    log_decay = -jnp.exp(-softplus - 0.5)
    a_sig = _sigmoid(a0_ref[...] + lowrank(lo, a2h_ref, a2l_ref))
    if has_vres:
        v_mix = _sigmoid(v0_ref[...] + lowrank(vres_ref[...], vw2h_ref, vw2l_ref))
        v = v + (vf_ref[...] - v) * v_mix
    vc_ref[...] = v

    ones = _seg_ones(LANES, C_DH)
    cum = _cumsum_rows(log_decay, _tri_incl(CHUNK))
    cum_last = cum[CHUNK - 1:CHUNK, :]
    gam_ref[0] = jnp.exp(cum_last)
    e_prev = jnp.exp(cum - log_decay)
    e_cum = jnp.exp(cum)
    e_inv = jnp.exp(-cum)
    e_tail = jnp.exp(cum_last - cum)

    lane = lax.broadcasted_iota(jnp.int32, (CHUNK, LANES), 1)
    par1 = lane >= C_DH
    tcol = lax.broadcasted_iota(jnp.int32, (CHUNK, LANES), 0)
    scol = lane % C_DH
    strict = jnp.concatenate([scol < tcol, scol < tcol], axis=1)
    incl = jnp.concatenate([scol <= tcol, scol <= tcol], axis=1)
    low_mask = jnp.concatenate([strict, incl], axis=0)

    pairs = [slice(p * LANES, (p + 1) * LANES) for p in range(C_WIDTH // LANES)]

    def head_sums(x):
        stacked = _segsum(jnp.concatenate([x[:, sl] for sl in pairs], axis=0), ones)
        return jnp.concatenate([stacked[p * CHUNK:(p + 1) * CHUNK] for p in range(len(pairs))], axis=1)

    kk = k0 * kk_ref[...]
    kk = kk / jnp.maximum(jnp.sqrt(head_sums(kk * kk)), 1e-12)
    k = k0 * (1.0 + (a_sig - 1.0) * ka_ref[...])
    b_vec = kk * a_sig
    bonus_ref[...] = head_sums(r * k * rk_ref[...]) * v
    a_t = -kk * e_prev
    r_t = r * e_cum
    at_ref[...] = a_t
    rt_ref[...] = r_t
    bh_ref[...] = b_vec * e_tail
    kh_ref[...] = k * e_tail
    a16, r16 = a_t.astype(BF16), r_t.astype(BF16)
    b16, k16, v16 = (b_vec * e_inv).astype(BF16), (k * e_inv).astype(BF16), v.astype(BF16)
    zero = jnp.zeros((CHUNK, LANES), BF16)

    def block_diag(x):
        return [jnp.where(par1, zero, x), jnp.where(par1, x, zero)]

    prods = [jnp.where(low_mask,
                       _dot_nt(jnp.concatenate([a16[:, sl], r16[:, sl]], axis=0),
                               jnp.concatenate(block_diag(b16[:, sl]) + block_diag(k16[:, sl]), axis=0)),
                       0.0) for sl in pairs]
    gys = [_dot(prod[:, LANES:].astype(BF16), jnp.concatenate(block_diag(v16[:, sl]), axis=0))
           for sl, prod in zip(pairs, prods)]
    for sl, prod, gy in zip(pairs, prods, gys):
        nab_ref[:, sl] = prod[:CHUNK, :LANES]
        arb_ref[:, sl] = prod[CHUNK:, :LANES]
        g_ref[:, sl] = gy[:CHUNK]
        y0_ref[:, sl] = gy[CHUNK:]


def _rwkprep(proj, shift_parts, params, vres, bsz, nc):
    m = proj.shape[0]
    has_vres = vres is not None
    row = lambda g: pl.BlockSpec((CHUNK, GROUP_W), lambda b, c, g=g: (b * nc + c, g))
    lo_spec = pl.BlockSpec((CHUNK, LANES), lambda b, c: (b * nc + c, COL_LO // LANES))
    st = lambda w: pl.BlockSpec((1, 1, w), lambda b, c: (b, 0, 0))
    vec = lambda w: pl.BlockSpec((1, w), lambda b, c: (0, 0))
    mat = lambda: pl.BlockSpec((LANES, GROUP_W), lambda b, c: (0, 0))
    out = pl.BlockSpec((CHUNK, GROUP_W), lambda b, c: (b * nc + c, 0))
    in_specs = [row(G_CR), row(G_CK), row(G_CV), lo_spec, st(GROUP_W), st(GROUP_W), st(GROUP_W), st(LANES),
                vec(GROUP_W), vec(GROUP_W), vec(GROUP_W), vec(LANES),
                vec(GROUP_W), mat(), mat(), vec(GROUP_W), mat(), mat(),
                vec(GROUP_W), vec(GROUP_W), vec(GROUP_W)]
    args = [proj, proj, proj, proj, *shift_parts,
            params["mu_r"], params["mu_k"], params["mu_v"], params["mu_lo"],
            params["w0"], params["w2h"], params["w2l"], params["a0"], params["a2h"], params["a2l"],
            params["k_k"], params["k_a"], params["r_k"]]
    if has_vres:
        in_specs += [pl.BlockSpec((CHUNK, LANES), lambda b, c: (b * nc + c, COL_VRES // LANES)),
                     out, vec(GROUP_W), mat(), mat()]
        args += [proj, vres["v_first"], vres["v0"], vres["w2h"], vres["w2l"]]
    big = jax.ShapeDtypeStruct((m, GROUP_W), F32)
    out_shape = [big] * 6 + [jax.ShapeDtypeStruct((bsz * nc, 1, GROUP_W), F32)] + [big] * 4 + [
        jax.ShapeDtypeStruct((bsz, 1, GROUP_W), F32)] * 3 + [jax.ShapeDtypeStruct((bsz, 1, LANES), F32)]
    out_specs = [out] * 6 + [pl.BlockSpec((1, 1, GROUP_W), lambda b, c: (b * nc + c, 0, 0))] + [out] * 4 + [
        st(GROUP_W)] * 3 + [st(LANES)]
    return pl.pallas_call(
        functools.partial(_rwkprep_kernel, has_vres=has_vres),
        grid=(bsz, nc),
        in_specs=in_specs,
        out_specs=out_specs,
        out_shape=out_shape,
        scratch_shapes=[pltpu.VMEM((1, GROUP_W), F32)] * 3 + [pltpu.VMEM((1, LANES), F32)],
        compiler_params=_params(("parallel", "arbitrary")),
        name="rwkprep",
    )(*args)


def _rwksolve_kernel(n_ref, ra_ref, rg_ref, w_ref, u_ref):
    def group_body(tg, carry):
        t0 = pl.multiple_of(tg * SUBLANES, SUBLANES)
        for r in range(SUBLANES):
            t = t0 + r

            def s_body(sg, acc, t=t):
                s0 = pl.multiple_of(sg * SUBLANES, SUBLANES)
                coef = n_ref[t, pl.ds(s0, SUBLANES), :]
                acc_w, acc_u = acc
                for q in range(SUBLANES):
                    acc_w = acc_w + coef[q:q + 1] * w_ref[s0 + q]
                    acc_u = acc_u + coef[q:q + 1] * u_ref[s0 + q]
                return acc_w, acc_u

            acc_w, acc_u = lax.fori_loop(0, tg, s_body, (ra_ref[t], rg_ref[t]))
            coef = n_ref[t, pl.ds(t0, SUBLANES), :]
            for q in range(r):
                acc_w = acc_w + coef[q:q + 1] * w_ref[t0 + q]
                acc_u = acc_u + coef[q:q + 1] * u_ref[t0 + q]
            w_ref[t] = acc_w
            u_ref[t] = acc_u
        return carry

    lax.fori_loop(0, CHUNK // SUBLANES, group_body, 0)


def _rwksolve(n_bl, ra_bl, rg_bl):
    n_inst = n_bl.shape[-1]
    spec = pl.BlockSpec((CHUNK, C_DH, LANES), lambda i: (0, 0, i))
    shape = jax.ShapeDtypeStruct(ra_bl.shape, F32)
    return pl.pallas_call(
        _rwksolve_kernel,
        grid=(n_inst // LANES,),
        in_specs=[spec, spec, spec],
        out_specs=[spec, spec],
        out_shape=[shape, shape],
        compiler_params=_params(("parallel",)),
        name="rwksolve",
    )(n_bl, ra_bl, rg_bl)


def _rwkseq_kernel(w_ref, u0_ref, rt_ref, arb_ref, y0_ref, v_ref, bh_ref, kh_ref, gam_ref, bonus_ref,
                   lnw_ref, lnb_ref, s0_ref, o_ref, sfin_ref, st_scr):
    c = pl.program_id(1)

    @pl.when(c == 0)
    def _():
        st_scr[...] = s0_ref[0]

    ones = _seg_ones(LANES, C_DH)
    lane = lax.broadcasted_iota(jnp.int32, (CHUNK, LANES), 1)
    par1 = lane >= C_DH
    bd = _seg_mask(LANES, C_DH)
    pairs = [slice(p * LANES, (p + 1) * LANES) for p in range(C_WIDTH // LANES)]
    states = [st_scr[p] for p in range(len(pairs))]
    st16 = [st.astype(BF16) for st in states]
    w16 = w_ref[...].astype(BF16)
    rt16 = rt_ref[...].astype(BF16)
    arb16 = arb_ref[...].astype(BF16)
    zero = jnp.zeros((CHUNK, LANES), F32)
    u = [_dot_nt(w16[:, sl], st) + u0_ref[:, sl] for sl, st in zip(pairs, st16)]
    y_state = [_dot_nt(rt16[:, sl], st) for sl, st in zip(pairs, st16)]
    ubd = [jnp.concatenate([jnp.where(par1, zero, x), jnp.where(par1, x, zero)], axis=0).astype(BF16) for x in u]
    y_u = [_dot(arb16[:, sl], x) for sl, x in zip(pairs, ubd)]
    v16 = v_ref[...].astype(BF16)
    bh16 = bh_ref[...].astype(BF16)
    kh16 = kh_ref[...].astype(BF16)
    upd = [_dot_tn(jnp.concatenate([x.astype(BF16), v16[:, sl]], axis=0),
                   jnp.concatenate([bh16[:, sl], kh16[:, sl]], axis=0)) for sl, x in zip(pairs, u)]
    for p, sl in enumerate(pairs):
        st_scr[p] = states[p] * gam_ref[0, :, sl] + jnp.where(bd, upd[p], 0.0)
    y = jnp.concatenate([a + b + y0_ref[:, sl] for sl, a, b in zip(pairs, y_state, y_u)], axis=0)
    d = y - _segsum(y, ones) * (1.0 / C_DH)
    var = _segsum(d * d, ones) * (1.0 / C_DH)
    dn = d * lax.rsqrt(var + C_GN_EPS)
    for p, sl in enumerate(pairs):
        o_ref[:, sl] = dn[p * CHUNK:(p + 1) * CHUNK] * lnw_ref[:, sl] + lnb_ref[:, sl] + bonus_ref[:, sl]

    sfin_ref[0] = st_scr[...]


def _rwkseq(w, u0, rt, arb, y0, v, bh, kh, gam, bonus, ln_w, ln_b, s0_bd, bsz, nc):
    m = w.shape[0]
    row = pl.BlockSpec((CHUNK, GROUP_W), lambda b, c: (b * nc + c, 0))
    vec = pl.BlockSpec((1, GROUP_W), lambda b, c: (0, 0))
    st_spec = pl.BlockSpec((1, C_WIDTH // LANES, LANES, LANES), lambda b, c: (b, 0, 0, 0))
    return pl.pallas_call(
        _rwkseq_kernel,
        grid=(bsz, nc),
        in_specs=[row] * 8 + [pl.BlockSpec((1, 1, GROUP_W), lambda b, c: (b * nc + c, 0, 0)), row, vec, vec,
                  st_spec],
        out_specs=[row, st_spec],
        out_shape=[jax.ShapeDtypeStruct((m, GROUP_W), F32),
                   jax.ShapeDtypeStruct((bsz, C_WIDTH // LANES, LANES, LANES), F32)],
        scratch_shapes=[pltpu.VMEM((C_WIDTH // LANES, LANES, LANES), F32)],
        compiler_params=_params(("parallel", "arbitrary")),
        name="rwkseq",
    )(w, u0, rt, arb, y0, v, bh, kh, gam, bonus, ln_w, ln_b, s0_bd)


def _merge_kernel(x_ref, oa_ref, ob_ref, oc_ref, ag_ref, bg_ref, cg_ref, ma_ref, mb_ref, mc_ref,
                  wa_ref, wb_ref, wc_ref, wo_ref, y_ref):
    def branch(o_ref, gate_ref, w_ref):
        return _dot((o_ref[...] * _silu(gate_ref[...])).astype(BF16), w_ref[...])

    merged = (_sigmoid(ma_ref[...]) * branch(oa_ref, ag_ref, wa_ref)
              + _sigmoid(mb_ref[...]) * branch(ob_ref, bg_ref, wb_ref)
              + _sigmoid(mc_ref[...]) * branch(oc_ref, cg_ref, wc_ref))
    y_ref[...] = x_ref[...] + _dot(merged.astype(BF16), wo_ref[...])


def _merge(x2d, o_a, o_b, o_c, proj, wa, wb, wc, wo):
    m = x2d.shape[0]
    tm = min(256, m)
    row = pl.BlockSpec((tm, GROUP_W), lambda i: (i, 0))
    grp = lambda g: pl.BlockSpec((tm, GROUP_W), lambda i, g=g: (i, g))
    wsp = pl.BlockSpec((GROUP_W, D_MODEL), lambda i: (0, 0))
    return pl.pallas_call(
        _merge_kernel,
        grid=(m // tm,),
        in_specs=[row, row, row, row, grp(G_AG), grp(G_BG), grp(G_CG), grp(G_MA), grp(G_MB), grp(G_MC),
                  wsp, wsp, wsp, wsp],
        out_specs=row,
        out_shape=jax.ShapeDtypeStruct((m, D_MODEL), F32),
        compiler_params=_params(("parallel",)),
        name="merge",
    )(x2d, o_a, o_b, o_c, proj, proj, proj, proj, proj, proj, wa, wb, wc, wo)


_C_OFF = 8 * GROUP_W
_CP_R = (_C_OFF, _C_OFF + C_WIDTH)
_CP_WLO = (_CP_R[1], _CP_R[1] + C_DECAY_RANK)
_CP_K = (_CP_WLO[1], _CP_WLO[1] + C_WIDTH)
_CP_V = (_CP_K[1], _CP_K[1] + C_WIDTH)
_CP_ALO = (_CP_V[1], _CP_V[1] + C_A_RANK)
_REST = _CP_ALO[1]


def _regroup_w_in(w, vres_w1):
    cols = [w[:, :_C_OFF], w[:, _CP_R[0]:_CP_R[1]], w[:, _CP_K[0]:_CP_K[1]], w[:, _CP_V[0]:_CP_V[1]],
            w[:, _REST:], w[:, _CP_WLO[0]:_CP_WLO[1]], w[:, _CP_ALO[0]:_CP_ALO[1]]]
    vres = jnp.zeros((D_MODEL, LANES), F32)
    if vres_w1 is not None:
        vres = vres.at[:, :C_VRES_RANK].set(vres_w1)
    cols.append(vres)
    cols.append(jnp.zeros((D_MODEL, N_PROJ_PAD - N_PROJ), F32))
    return jnp.concatenate(cols, axis=1).astype(BF16)


def _split_shift(s):
    o = _C_OFF
    part = lambda a: s[:, a[0] - o:a[1] - o]
    lo = jnp.concatenate([part(_CP_WLO), part(_CP_ALO)], axis=1)
    return [x[:, None, :] for x in (part(_CP_R), part(_CP_K), part(_CP_V), lo)]


def _join_shift(r, k, v, lo):
    r, k, v, lo = (x[:, 0, :] for x in (r, k, v, lo))
    return jnp.concatenate([r, lo[:, :C_DECAY_RANK], k, v, lo[:, C_DECAY_RANK:]], axis=1)


def _pad_rows(w, row0):
    out = jnp.zeros((LANES, w.shape[1]), F32).at[row0:row0 + w.shape[0]].set(w)
    hi = out.astype(BF16)
    return hi, (out - hi.astype(F32)).astype(BF16)


def _rope_tables(pos):
    half = A_ROT // 2
    inv_freq = ROPE_THETA ** (-(jnp.arange(half, dtype=F32) * (2.0 / A_ROT)))
    ang = pos.astype(F32)[:, None] * inv_freq[None, :]
    cos, sin = jnp.cos(ang), jnp.sin(ang)
    t = pos.shape[0]
    one = jnp.ones((t, A_DQK - A_ROT), F32)
    zero = jnp.zeros((t, A_DQK - A_ROT), F32)
    z8 = jnp.zeros((t, half), F32)
    c64 = jnp.concatenate([cos, cos, one], axis=1)
    s1 = jnp.concatenate([-sin, z8, zero], axis=1)
    s2 = jnp.concatenate([z8, sin, zero], axis=1)
    tile = lambda x: jnp.concatenate([x, x], axis=1)
    return tile(c64), tile(s1), tile(s2)


def _to_lanes(x, rows):
    return x.reshape(rows, CHUNK, C_HEADS, C_DH).transpose(1, 3, 0, 2).reshape(CHUNK, C_DH, rows * C_HEADS)


def _from_lanes(x, rows):
    return x.reshape(CHUNK, C_DH, rows, C_HEADS).transpose(2, 0, 3, 1).reshape(rows * CHUNK, C_WIDTH)


def _layer(l, x2d, bsz, t_len, q_off, P, lb, past, v_first):
    nc = t_len // CHUNK
    m = bsz * t_len
    w_in = _regroup_w_in(P["w_in"][l], P["c_vres_w1"][l - 1] if l > 0 else None)
    proj = _proj(x2d, P["norm_g"][l][None, :], w_in)

    pos = q_off + jnp.arange(t_len, dtype=jnp.int32)
    cos_t, sin1_t, sin2_t = _rope_tables(pos)
    tile2 = lambda g: jnp.concatenate([g, g])[None, :]
    q16t, k32, k16, v32, v16t = _qkprep(proj, tile2(P["a_qnorm_g"][l]), tile2(P["a_knorm_g"][l]),
                                        cos_t, sin1_t, sin2_t, t_len)
    k16 = k16.reshape(bsz, t_len, GROUP_W)
    if past is not None:
        pk, pv = past[0][l], past[1][l]
        p_len = pk.shape[1]
        k16 = jnp.concatenate([pk.reshape(bsz, p_len, GROUP_W).astype(BF16), k16], axis=1)
        pv_t = jnp.swapaxes(pv.reshape(bsz, p_len, GROUP_W).astype(BF16), 1, 2)
        v16t = jnp.concatenate([pv_t, v16t], axis=2)
    lam_init = 0.8 - 0.6 * math.exp(-0.3 * l)
    lp = P["a_lambda"][l].astype(F32)
    lam = jnp.exp(jnp.sum(lp[0] * lp[1])) - jnp.exp(jnp.sum(lp[2] * lp[3])) + lam_init
    bound = (8.0 * LOG2_E * 1.02) * jnp.max(jnp.abs(P["a_qnorm_g"][l])) * jnp.max(jnp.abs(P["a_knorm_g"][l]))
    o_a = _attn(q16t, k16, v16t, jnp.stack([lam, bound]).astype(F32), P["a_subln_g"][l][None, :], q_off,
                1.0 - lam_init, bounded=past is None)
    o_a = o_a.reshape(m, GROUP_W)

    if past is None:
        s_h = jnp.zeros((bsz, B_HEADS, LANES, LANES), F32)
    else:
        s_h = jnp.swapaxes(past[2][l].astype(F32), -1, -2)
    o_b, s_h_new = _hgrn(proj, lb[l][None, :], P["b_norm_g"][l][None, :], s_h, bsz, nc)
    s_h_new = jnp.swapaxes(s_h_new, -1, -2)

    if past is None:
        shift_prev = jnp.zeros((bsz, 3 * C_WIDTH + C_DECAY_RANK + C_A_RANK), F32)
        s_r = jnp.zeros((bsz, C_HEADS, C_DH, C_DH), F32)
    else:
        shift_prev, s_r = past[4][l], past[3][l].astype(F32)
    mu = _split_shift(P["c_shift_mu"][l][None, :])
    w2h, w2l = _pad_rows(P["c_w2"][l], 0)
    a2h, a2l = _pad_rows(P["c_a2"][l], C_DECAY_RANK)
    cparams = {"mu_r": mu[0][0], "mu_k": mu[1][0], "mu_v": mu[2][0], "mu_lo": mu[3][0],
               "w0": P["c_w0"][l][None, :], "w2h": w2h, "w2l": w2l,
               "a0": P["c_a0"][l][None, :], "a2h": a2h, "a2l": a2l,
               "k_k": P["c_k_k"][l][None, :], "k_a": P["c_k_a"][l][None, :],
               "r_k": P["c_r_k"][l].reshape(1, C_WIDTH)}
    vres = None
    if l > 0:
        vh, vl = _pad_rows(P["c_vres_w2"][l - 1], 0)
        vres = {"v_first": v_first, "v0": P["c_v0"][l - 1][None, :], "w2h": vh, "w2l": vl}
    (a_t, r_t, b_h, k_h, v_c, bonus, gam, nab, arb, g_mat, y0, sh_r, sh_k, sh_v, sh_lo) = _rwkprep(
        proj, _split_shift(shift_prev), cparams, vres, bsz, nc)
    rows = bsz * nc
    n_inst = rows * C_HEADS
    pad = (-n_inst) % LANES
    to_lanes = lambda x: jnp.pad(_to_lanes(x, rows), ((0, 0), (0, 0), (0, pad))) if pad else _to_lanes(x, rows)
    w_bl, u_bl = _rwksolve(to_lanes(nab), to_lanes(a_t), to_lanes(g_mat))
    w_mat = _from_lanes(w_bl[:, :, :n_inst], rows)
    u0 = _from_lanes(u_bl[:, :, :n_inst], rows)
    eye2 = jnp.eye(2, dtype=F32)
    s_bd = jnp.einsum("bpqvk,qr->bpqvrk", s_r.reshape(bsz, C_HEADS // 2, 2, C_DH, C_DH), eye2)
    s_bd = s_bd.reshape(bsz, C_HEADS // 2, LANES, LANES)
    o_c, s_bd_new = _rwkseq(w_mat, u0, r_t, arb, y0, v_c, b_h, k_h, gam, bonus,
                            P["c_ln_w"][l][None, :], P["c_ln_b"][l][None, :], s_bd, bsz, nc)
    s_new6 = s_bd_new.reshape(bsz, C_HEADS // 2, 2, C_DH, 2, C_DH)
    s_r_new = jnp.stack([s_new6[:, :, 0, :, 0, :], s_new6[:, :, 1, :, 1, :]], axis=2)
    s_r_new = s_r_new.reshape(bsz, C_HEADS, C_DH, C_DH)
    shift_new = _join_shift(sh_r, sh_k, sh_v, sh_lo)

    bf = lambda w: w.astype(BF16)
    y = _merge(x2d, o_a, o_b, o_c, proj, bf(P["w_out_a"][l]), bf(P["w_out_b"][l]), bf(P["w_out_c"][l]),
               bf(P["w_o"][l]))
    k_rows = k32.reshape(bsz, t_len, A_HEADS, 2 * A_DQK)
    v_rows = v32.reshape(bsz, t_len, A_HEADS, 2 * A_DQK)
    return y, (k_rows, v_rows, s_h_new, s_r_new, shift_new), v_c


def _run_trunk(x, q_off, P, lb, past):
    bsz, t_len, _ = x.shape
    depth = P["w_in"].shape[0]
    x2d = x.reshape(bsz * t_len, D_MODEL)
    outs = ([], [], [], [], [])
    v_first = None
    for l in range(depth):
        x2d, entries, v_c = _layer(l, x2d, bsz, t_len, q_off, P, lb, past, v_first)
        if l == 0:
            v_first = v_c
        for lst, e in zip(outs, entries):
            lst.append(e)
    return x2d.reshape(bsz, t_len, D_MODEL), [jnp.stack(lst) for lst in outs]


def kernel(x_prompt, x_sample, cache_attn_k, cache_attn_v, state_hgrn, state_rwkv, state_rwkv_shift,
           norm_g, w_in, a_qnorm_g, a_knorm_g, a_lambda, a_subln_g, b_lower, b_norm_g,
           c_shift_mu, c_w0, c_w2, c_a0, c_a2, c_k_k, c_k_a, c_r_k, c_ln_w, c_ln_b,
           c_vres_w1, c_vres_w2, c_v0, w_out_a, w_out_b, w_out_c, w_o):
    P = {"norm_g": norm_g, "w_in": w_in, "a_qnorm_g": a_qnorm_g, "a_knorm_g": a_knorm_g,
         "a_lambda": a_lambda, "a_subln_g": a_subln_g, "b_norm_g": b_norm_g,
         "c_shift_mu": c_shift_mu, "c_w0": c_w0, "c_w2": c_w2, "c_a0": c_a0, "c_a2": c_a2,
         "c_k_k": c_k_k, "c_k_a": c_k_a, "c_r_k": c_r_k, "c_ln_w": c_ln_w, "c_ln_b": c_ln_b,
         "c_vres_w1": c_vres_w1, "c_vres_w2": c_vres_w2, "c_v0": c_v0,
         "w_out_a": w_out_a, "w_out_b": w_out_b, "w_out_c": w_out_c, "w_o": w_o}
    sm = jax.nn.softmax(b_lower.astype(F32), axis=0)
    lb = jnp.cumsum(sm, axis=0) - sm[0:1]
    past_len = cache_attn_k.shape[2]
    y_p, (k_p, v_p, hg_p, rw_p, sh_p) = _run_trunk(x_prompt, 0, P, lb, None)
    y_s, (k_s, v_s, hg_s, rw_s, sh_s) = _run_trunk(
        x_sample, past_len, P, lb, (cache_attn_k, cache_attn_v, state_hgrn, state_rwkv, state_rwkv_shift))
    return (y_p, y_s, k_p, v_p, hg_p, rw_p, sh_p, k_s, v_s, hg_s, rw_s, sh_s)
```

```python
import functools
import math

import jax
import jax.numpy as jnp
from jax import lax
from jax.experimental import pallas as pl
from jax.experimental.pallas import tpu as pltpu

F32 = jnp.float32
BF16 = jnp.bfloat16

D_MODEL = 1024
CHUNK = 64
EPS = 1e-6
NEG_BIG = -1e30
LOG2_E = 1.4426950408889634
A_HEADS = 8
A_DQK = 64
A_ROT = 16
ROPE_THETA = 500000.0
B_HEADS = 8
C_HEADS = 16
C_DH = 64
C_WIDTH = 1024
C_DECAY_RANK = 64
C_A_RANK = 64
C_VRES_RANK = 32
C_GN_EPS = 64e-5
ATTN_BOUND_LIMIT = 60.0
HGRN_SAFE_SPAN = 60.0
LANES = 128
SUBLANES = 8
GROUP_W = 1024
N_GROUPS = 15
COL_LO = N_GROUPS * GROUP_W
COL_VRES = COL_LO + LANES
N_PROJ = COL_VRES + LANES
PROJ_TN = 512
N_PROJ_PAD = -(-N_PROJ // PROJ_TN) * PROJ_TN
(G_AQ, G_AK, G_AV, G_AG, G_BQ, G_BF, G_BI, G_BG, G_CR, G_CK, G_CV, G_CG, G_MA, G_MB, G_MC) = range(N_GROUPS)
VMEM_LIMIT = 56 * 1024 * 1024


def _dot(a, b):
    return jnp.dot(a, b, preferred_element_type=F32)


def _dot_nt(a, b):
    return lax.dot_general(a, b, (((1,), (1,)), ((), ())), preferred_element_type=F32)


def _dot_tn(a, b):
    return lax.dot_general(a, b, (((0,), (0,)), ((), ())), preferred_element_type=F32)


def _split2(x):
    hi = x.astype(BF16)
    lo = (x - hi.astype(F32)).astype(BF16)
    return hi, lo


def _split3(x):
    hi = x.astype(BF16)
    r = x - hi.astype(F32)
    mid = r.astype(BF16)
    lo = (r - mid.astype(F32)).astype(BF16)
    return hi, mid, lo


def _sigmoid(x):
    return 1.0 / (1.0 + jnp.exp(-x))


def _silu(x):
    return x * _sigmoid(x)


def _seg_mask(n, seg):
    r = lax.broadcasted_iota(jnp.int32, (n, n), 0) // seg
    c = lax.broadcasted_iota(jnp.int32, (n, n), 1) // seg
    return r == c


def _seg_ones(n, seg):
    return _seg_mask(n, seg).astype(BF16)


def _segsum(x, ones_bf16):
    hi, lo = _split2(x)
    return _dot(hi, ones_bf16) + _dot(lo, ones_bf16)


def _tri_incl(n):
    r = lax.broadcasted_iota(jnp.int32, (n, n), 0)
    c = lax.broadcasted_iota(jnp.int32, (n, n), 1)
    return (c <= r).astype(BF16)


def _cumsum_rows(x, tri_bf16):
    hi, mid, lo = _split3(x)
    return _dot(tri_bf16, hi) + _dot(tri_bf16, mid) + _dot(tri_bf16, lo)


def _params(sem, vmem=None):
    return pltpu.CompilerParams(dimension_semantics=sem, vmem_limit_bytes=vmem or VMEM_LIMIT)


def _proj_kernel(x_ref, g_ref, w_ref, o_ref, h_scr):
    @pl.when(pl.program_id(1) == 0)
    def _():
        x = x_ref[...]
        ms = jnp.mean(x * x, axis=-1, keepdims=True)
        h_scr[...] = (x * lax.rsqrt(ms + EPS) * g_ref[...]).astype(BF16)

    o_ref[...] = _dot(h_scr[...], w_ref[...]).astype(o_ref.dtype)


def _proj(x2d, g, w_bf16):
    m = x2d.shape[0]
    tm = min(2048, m)
    n = w_bf16.shape[1]
    return pl.pallas_call(
        _proj_kernel,
        grid=(m // tm, n // PROJ_TN),
        in_specs=[pl.BlockSpec((tm, D_MODEL), lambda i, j: (i, 0)),
                  pl.BlockSpec((1, D_MODEL), lambda i, j: (0, 0)),
                  pl.BlockSpec((D_MODEL, PROJ_TN), lambda i, j: (0, j))],
        out_specs=pl.BlockSpec((tm, PROJ_TN), lambda i, j: (i, j)),
        out_shape=jax.ShapeDtypeStruct((m, n), BF16),
        scratch_shapes=[pltpu.VMEM((tm, D_MODEL), BF16)],
        compiler_params=_params(("parallel", "arbitrary")),
        name="proj",
    )(x2d, g, w_bf16)


def _qkprep_kernel(q_ref, k_ref, v_ref, qg_ref, kg_ref, c_ref, s1_ref, s2_ref,
                   q16t_ref, k32_ref, k16_ref, v32_ref, v16t_ref):
    ones = _seg_ones(LANES, A_DQK)
    cosv, sin1, sin2 = c_ref[...], s1_ref[...], s2_ref[...]

    def prep(x, gain):
        ss = _segsum(x * x, ones)
        y = x * lax.rsqrt(ss * (1.0 / A_DQK) + EPS) * gain
        return y * cosv + pltpu.roll(y, LANES - A_ROT // 2, 1) * sin1 + pltpu.roll(y, A_ROT // 2, 1) * sin2

    for c in range(GROUP_W // LANES):
        sl = slice(c * LANES, (c + 1) * LANES)
        q = prep(q_ref[:, sl].astype(F32), qg_ref[...])
        q16t_ref[0, sl, :] = (q * (A_DQK ** -0.5 * LOG2_E)).T.astype(BF16)
        k = prep(k_ref[:, sl].astype(F32), kg_ref[...])
        k32_ref[:, sl] = k
        k16_ref[:, sl] = k.astype(BF16)
        v = v_ref[:, sl].astype(F32)
        v32_ref[:, sl] = v
        v16t_ref[0, sl, :] = v.T.astype(BF16)


def _qkprep(proj, qg128, kg128, cos_t, sin1_t, sin2_t, t_len):
    m = proj.shape[0]
    tm = min(512, t_len)
    nt = t_len // tm
    row = lambda g: pl.BlockSpec((tm, GROUP_W), lambda i, g=g: (i, g))
    tab = pl.BlockSpec((tm, LANES), lambda i: (i % nt, 0))
    vec = pl.BlockSpec((1, LANES), lambda i: (0, 0))
    out = pl.BlockSpec((tm, GROUP_W), lambda i: (i, 0))
    out_t = pl.BlockSpec((1, GROUP_W, tm), lambda i: (i // nt, 0, i % nt))
    transposed = jax.ShapeDtypeStruct((m // t_len, GROUP_W, t_len), BF16)
    return pl.pallas_call(
        _qkprep_kernel,
        grid=(m // tm,),
        in_specs=[row(G_AQ), row(G_AK), row(G_AV), vec, vec, tab, tab, tab],
        out_specs=[out_t, out, out, out, out_t],
        out_shape=[transposed, jax.ShapeDtypeStruct((m, GROUP_W), F32),
                   jax.ShapeDtypeStruct((m, GROUP_W), BF16), jax.ShapeDtypeStruct((m, GROUP_W), F32),
                   transposed],
        compiler_params=_params(("parallel",)),
        name="qkprep",
    )(proj, proj, proj, qg128, kg128, cos_t, sin1_t, sin2_t)


def _attn_kernel(sc_ref, qt_ref, k_ref, vt_ref, g_ref, o_ref, m1, l1, a1, m2, l2, a2,
                 *, tq, tk, q_off, nk, out_scale, bounded):
    qi = pl.program_id(2)
    q_first = q_off + qi * tq
    first_chunk_end = (q_first // CHUNK) * CHUNK + CHUNK
    last_vis = ((q_first + tq - 1) // CHUNK) * CHUNK + CHUNK - 1
    n_blocks = jnp.minimum(nk, last_vis // tk + 1)
    n_full = jnp.minimum(n_blocks, first_chunk_end // tk)

    for m, l, a in ((m1, l1, a1), (m2, l2, a2)):
        m[...] = jnp.full(m.shape, NEG_BIG, F32)
        l[...] = jnp.zeros(l.shape, F32)
        a[...] = jnp.zeros(a.shape, F32)

    qt = qt_ref[0]
    dim = lax.broadcasted_iota(jnp.int32, qt.shape, 0)
    zero = jnp.zeros_like(qt)
    q_halves = (jnp.where(dim < A_DQK, qt, zero), jnp.where(dim >= A_DQK, qt, zero))

    def scores(kj, masked):
        if nk == 1:
            k0, k, vt = 0, k_ref[0], vt_ref[0]
        else:
            k0 = pl.multiple_of(kj * tk, tk)
            k = k_ref[0, pl.ds(k0, tk), :]
            vt = vt_ref[0, :, pl.ds(k0, tk)]
        vis = None
        if masked:
            kpos = k0 + lax.broadcasted_iota(jnp.int32, (tk, tq), 0)
            qpos = q_first + lax.broadcasted_iota(jnp.int32, (tk, tq), 1)
            vis = (kpos // CHUNK) <= (qpos // CHUNK)

        def score(qh):
            s = _dot(k, qh)
            return jnp.where(vis, s, NEG_BIG) if masked else s

        return score, vt

    def key_partial_sums(pr):
        return jnp.sum(pr.reshape(tk // SUBLANES, SUBLANES, tq), axis=0)

    def online_step(kj, masked):
        score, vt = scores(kj, masked)
        for qh, (m, l, a) in zip(q_halves, ((m1, l1, a1), (m2, l2, a2))):
            s = score(qh)
            m_prev = m[...]
            m_new = jnp.maximum(m_prev, jnp.max(s, axis=0, keepdims=True))
            pr = jnp.exp2(s - m_new)
            alpha = jnp.exp2(m_prev - m_new)
            l[...] = alpha * l[...] + key_partial_sums(pr)
            a[...] = alpha * a[...] + _dot(vt, pr.astype(BF16))
            m[...] = m_new

    def bounded_step(kj, masked):
        score, vt = scores(kj, masked)
        for qh, (m, l, a) in zip(q_halves, ((m1, l1, a1), (m2, l2, a2))):
            pr = jnp.exp2(score(qh) - bound)
            l[...] += key_partial_sums(pr)
            a[...] += _dot(vt, pr.astype(BF16))

    def run(step):
        def full_body(kj, carry):
            step(kj, False)
            return carry

        def masked_body(kj, carry):
            step(kj, True)
            return carry

        lax.fori_loop(0, n_full, full_body, 0)
        lax.fori_loop(n_full, n_blocks, masked_body, 0)

    lam = sc_ref[0]
    bound = sc_ref[1]
    if bounded:
        in_range = bound <= ATTN_BOUND_LIMIT

        @pl.when(in_range)
        def _():
            run(bounded_step)

        @pl.when(jnp.logical_not(in_range))
        def _():
            run(online_step)
    else:
        run(online_step)

    l1_tot = jnp.sum(l1[...], axis=0, keepdims=True)
    l2_tot = jnp.sum(l2[...], axis=0, keepdims=True)
    o = (a1[...] / l1_tot - lam * (a2[...] / l2_tot)).T
    ms = jnp.mean(o * o, axis=-1, keepdims=True)
    o_ref[0] = o * lax.rsqrt(ms + EPS) * g_ref[...] * out_scale


def _attn(q16t, k16, v16t, scalars, subln_g, q_off, out_scale, bounded):
    b, _, tq_len = q16t.shape
    tk_len = k16.shape[1]
    tq = min(1024, tq_len)
    tk = 512 if tk_len > 2048 else tk_len
    assert tq_len % tq == 0 and tk_len % tk == 0
    nq, nk = tq_len // tq, tk_len // tk
    kern = functools.partial(_attn_kernel, tq=tq, tk=tk, q_off=q_off, nk=nk, out_scale=out_scale,
                             bounded=bounded)
    run_max = pltpu.VMEM((1, tq), F32)
    key_sum = pltpu.VMEM((SUBLANES, tq), F32)
    acc = pltpu.VMEM((LANES, tq), F32)
    return pl.pallas_call(
        kern,
        grid=(b, A_HEADS, nq),
        in_specs=[pl.BlockSpec(memory_space=pltpu.SMEM),
                  pl.BlockSpec((1, LANES, tq), lambda bi, h, qi: (bi, h, qi)),
                  pl.BlockSpec((1, tk_len, LANES), lambda bi, h, qi: (bi, 0, h)),
                  pl.BlockSpec((1, LANES, tk_len), lambda bi, h, qi: (bi, h, 0)),
                  pl.BlockSpec((1, LANES), lambda bi, h, qi: (0, 0))],
        out_specs=pl.BlockSpec((1, tq, LANES), lambda bi, h, qi: (bi, qi, h)),
        out_shape=jax.ShapeDtypeStruct((b, tq_len, GROUP_W), F32),
        scratch_shapes=[run_max, key_sum, acc, run_max, key_sum, acc],
        compiler_params=_params(("parallel", "parallel", "parallel")),
        name="attn",
    )(scalars, q16t, k16, v16t, subln_g)


def _hgrn_kernel(q_ref, f_ref, i_ref, lb_ref, g_ref, s0_ref, o_ref, sfin_ref,
                 st_scr, cum_scr, qk_scr, oi_scr):
    c = pl.program_id(1)

    @pl.when(c == 0)
    def _():
        st_scr[...] = s0_ref[0]

    z = f_ref[...].astype(F32)
    lb = lb_ref[...]
    log_f = jnp.log(lb + (1.0 - lb) * _sigmoid(z))
    k_in = (1.0 - lb) * _sigmoid(-z)
    q = _silu(q_ref[...].astype(F32))
    cum = _cumsum_rows(log_f, _tri_incl(CHUNK))
    row = lax.broadcasted_iota(jnp.int32, (CHUNK, 1), 0)
    head_slices = [slice(h * LANES, (h + 1) * LANES) for h in range(B_HEADS)]

    rel = cum - cum[CHUNK // 2 - 1:CHUNK // 2, :]
    safe = jnp.max(jnp.abs(rel)) <= HGRN_SAFE_SPAN

    @pl.when(safe)
    def _():
        qe = (q * jnp.exp(rel)).astype(BF16)
        ke = (k_in * jnp.exp(-rel)).astype(BF16)
        causal = lax.broadcasted_iota(jnp.int32, (CHUNK, CHUNK), 1) <= lax.broadcasted_iota(
            jnp.int32, (CHUNK, CHUNK), 0)
        scores = [jnp.where(causal, _dot_nt(qe[:, sl], ke[:, sl]), 0.0).astype(BF16) for sl in head_slices]
        i16 = i_ref[...].astype(BF16)
        for sl, sc in zip(head_slices, scores):
            oi_scr[:, sl] = _dot(sc, i16[:, sl])

    @pl.when(jnp.logical_not(safe))
    def _():
        cum_scr[...] = cum
        qk_scr[...] = q
        for h in range(B_HEADS):
            sl = slice(h * LANES, (h + 1) * LANES)
            cum_h = cum[:, sl]
            kin_h = k_in[:, sl]
            i_h = i_ref[:, sl].astype(F32)

            def body(g, carry, sl=sl, cum_h=cum_h, kin_h=kin_h, i_h=i_h):
                g8 = pl.multiple_of(g * SUBLANES, SUBLANES)
                c_tile = cum_scr[pl.ds(g8, SUBLANES), sl]
                q_tile = qk_scr[pl.ds(g8, SUBLANES), sl]
                rows = []
                for r in range(SUBLANES):
                    dec = jnp.exp(jnp.minimum(c_tile[r:r + 1] - cum_h, 0.0))
                    col = jnp.sum(dec * (kin_h * q_tile[r:r + 1]), axis=1, keepdims=True)
                    col = jnp.where(row <= g8 + r, col, 0.0)
                    rows.append(jnp.sum(col * i_h, axis=0, keepdims=True))
                oi_scr[pl.ds(g8, SUBLANES), sl] = jnp.concatenate(rows, axis=0)
                return carry

            lax.fori_loop(0, CHUNK // SUBLANES, body, 0)

    cum_last = cum[CHUNK - 1:CHUNK, :]
    q_dec = (q * jnp.exp(cum)).astype(BF16)
    k_tail = (k_in * jnp.exp(cum_last - cum)).astype(BF16)
    i16 = i_ref[...].astype(BF16)
    decay = jnp.exp(cum_last)
    states = [st_scr[h] for h in range(B_HEADS)]
    inter = [_dot_nt(q_dec[:, sl], st.astype(BF16)) for sl, st in zip(head_slices, states)]
    update = [_dot_tn(i16[:, sl], k_tail[:, sl]) for sl in head_slices]
    for h, sl in enumerate(head_slices):
        st_scr[h] = states[h] * decay[:, sl] + update[h]
        o = oi_scr[:, sl] + inter[h]
        ms = jnp.mean(o * o, axis=-1, keepdims=True)
        o_ref[:, sl] = o * lax.rsqrt(ms + EPS) * g_ref[...]

    sfin_ref[0] = st_scr[...]


def _hgrn(proj, lb, norm_g128, s0_t, bsz, nc):
    m = proj.shape[0]
    row = lambda g: pl.BlockSpec((CHUNK, GROUP_W), lambda b, c, g=g: (b * nc + c, g))
    st_spec = pl.BlockSpec((1, B_HEADS, LANES, LANES), lambda b, c: (b, 0, 0, 0))
    return pl.pallas_call(
        _hgrn_kernel,
        grid=(bsz, nc),
        in_specs=[row(G_BQ), row(G_BF), row(G_BI),
                  pl.BlockSpec((1, GROUP_W), lambda b, c: (0, 0)),
                  pl.BlockSpec((1, LANES), lambda b, c: (0, 0)),
                  st_spec],
        out_specs=[pl.BlockSpec((CHUNK, GROUP_W), lambda b, c: (b * nc + c, 0)), st_spec],
        out_shape=[jax.ShapeDtypeStruct((m, GROUP_W), F32),
                   jax.ShapeDtypeStruct((bsz, B_HEADS, LANES, LANES), F32)],
        scratch_shapes=[pltpu.VMEM((B_HEADS, LANES, LANES), F32),
                        pltpu.VMEM((CHUNK, GROUP_W), F32),
                        pltpu.VMEM((CHUNK, GROUP_W), F32),
                        pltpu.VMEM((CHUNK, GROUP_W), F32)],
        compiler_params=_params(("parallel", "arbitrary")),
        name="hgrn",
    )(proj, proj, proj, lb, norm_g128, s0_t)


def _rwkprep_kernel(*refs, has_vres):
    (cr_ref, ck_ref, cv_ref, clo_ref, spr_ref, spk_ref, spv_ref, splo_ref,
     mur_ref, muk_ref, muv_ref, mulo_ref, w0_ref, w2h_ref, w2l_ref, a0_ref, a2h_ref, a2l_ref,
     kk_ref, ka_ref, rk_ref) = refs[:21]
    pos = 21
    if has_vres:
        vres_ref, vf_ref, v0_ref, vw2h_ref, vw2l_ref = refs[pos:pos + 5]
        pos += 5
    (at_ref, rt_ref, bh_ref, kh_ref, vc_ref, bonus_ref, gam_ref, nab_ref, arb_ref, g_ref, y0_ref,
     shr_ref, shk_ref, shv_ref, shlo_ref) = refs[pos:pos + 15]
    pr_scr, pk_scr, pv_scr, plo_scr = refs[pos + 15:]
    c = pl.program_id(1)

    @pl.when(c == 0)
    def _():
        pr_scr[...] = spr_ref[0]
        pk_scr[...] = spk_ref[0]
        pv_scr[...] = spv_ref[0]
        plo_scr[...] = splo_ref[0]

    def shifted(x_ref, prev_scr, mu_ref, last_ref):
        x = x_ref[...].astype(F32)
        row = lax.broadcasted_iota(jnp.int32, x.shape, 0)
        prev = jnp.where(row == 0, prev_scr[...], pltpu.roll(x, 1, 0))
        last = x[CHUNK - 1:CHUNK, :]
        prev_scr[...] = last
        last_ref[0] = last
        return x + (prev - x) * mu_ref[...]

    r = shifted(cr_ref, pr_scr, mur_ref, shr_ref)
    k0 = shifted(ck_ref, pk_scr, muk_ref, shk_ref)
    v = shifted(cv_ref, pv_scr, muv_ref, shv_ref)
    lo = shifted(clo_ref, plo_scr, mulo_ref, shlo_ref)

    def lowrank(x, wh_ref, wl_ref):
        xh, xl = _split2(x)
        return _dot(xh, wh_ref[...]) + _dot(xl, wh_ref[...]) + _dot(xh, wl_ref[...])

    w_in = w0_ref[...] + lowrank(jnp.tanh(lo), w2h_ref, w2l_ref)
    nw = -w_in
    softplus = jnp.maximum(nw, 0.0) + jnp.log(1.0 + jnp.exp(-jnp.abs(nw)))
    log_decay = -jnp.exp(-softplus - 0.5)
    a_sig = _sigmoid(a0_ref[...] + lowrank(lo, a2h_ref, a2l_ref))
    if has_vres:
        v_mix = _sigmoid(v0_ref[...] + lowrank(vres_ref[...].astype(F32), vw2h_ref, vw2l_ref))
        v = v + (vf_ref[...] - v) * v_mix
    vc_ref[...] = v

    ones = _seg_ones(LANES, C_DH)
    cum = _cumsum_rows(log_decay, _tri_incl(CHUNK))
    cum_last = cum[CHUNK - 1:CHUNK, :]
    gam_ref[0] = jnp.exp(cum_last)
    e_prev = jnp.exp(cum - log_decay)
    e_cum = jnp.exp(cum)
    e_inv = jnp.exp(-cum)
    e_tail = jnp.exp(cum_last - cum)

    lane = lax.broadcasted_iota(jnp.int32, (CHUNK, LANES), 1)
    par1 = lane >= C_DH
    tcol = lax.broadcasted_iota(jnp.int32, (CHUNK, LANES), 0)
    scol = lane % C_DH
    strict = jnp.concatenate([scol < tcol, scol < tcol], axis=1)
    incl = jnp.concatenate([scol <= tcol, scol <= tcol], axis=1)
    low_mask = jnp.concatenate([strict, incl], axis=0)

    pairs = [slice(p * LANES, (p + 1) * LANES) for p in range(C_WIDTH // LANES)]

    def head_sums(x):
        stacked = _segsum(jnp.concatenate([x[:, sl] for sl in pairs], axis=0), ones)
        return jnp.concatenate([stacked[p * CHUNK:(p + 1) * CHUNK] for p in range(len(pairs))], axis=1)

    kk = k0 * kk_ref[...]
    kk = kk / jnp.maximum(jnp.sqrt(head_sums(kk * kk)), 1e-12)
    k = k0 * (1.0 + (a_sig - 1.0) * ka_ref[...])
    b_vec = kk * a_sig
    bonus_ref[...] = head_sums(r * k * rk_ref[...]) * v
    a_t = -kk * e_prev
    r_t = r * e_cum
    a16, r16 = a_t.astype(BF16), r_t.astype(BF16)
    at_ref[...] = a16
    rt_ref[...] = r16
    bh_ref[...] = (b_vec * e_tail).astype(BF16)
    kh_ref[...] = (k * e_tail).astype(BF16)
    b16, k16, v16 = (b_vec * e_inv).astype(BF16), (k * e_inv).astype(BF16), v.astype(BF16)
    zero = jnp.zeros((CHUNK, LANES), BF16)

    def block_diag(x):
        return [jnp.where(par1, zero, x), jnp.where(par1, x, zero)]

    prods = [jnp.where(low_mask,
                       _dot_nt(jnp.concatenate([a16[:, sl], r16[:, sl]], axis=0),
                               jnp.concatenate(block_diag(b16[:, sl]) + block_diag(k16[:, sl]), axis=0)),
                       0.0) for sl in pairs]
    gys = [_dot(prod[:, LANES:].astype(BF16), jnp.concatenate(block_diag(v16[:, sl]), axis=0))
           for sl, prod in zip(pairs, prods)]
    for sl, prod, gy in zip(pairs, prods, gys):
        nab_ref[:, sl] = prod[:CHUNK, :LANES]
        arb_ref[:, sl] = prod[CHUNK:, :LANES].astype(BF16)
        g_ref[:, sl] = gy[:CHUNK].astype(BF16)
        y0_ref[:, sl] = gy[CHUNK:]


def _rwkprep(proj, shift_parts, params, vres, bsz, nc):
    m = proj.shape[0]
    has_vres = vres is not None
    row = lambda g: pl.BlockSpec((CHUNK, GROUP_W), lambda b, c, g=g: (b * nc + c, g))
    lo_spec = pl.BlockSpec((CHUNK, LANES), lambda b, c: (b * nc + c, COL_LO // LANES))
    st = lambda w: pl.BlockSpec((1, 1, w), lambda b, c: (b, 0, 0))
    vec = lambda w: pl.BlockSpec((1, w), lambda b, c: (0, 0))
    mat = lambda: pl.BlockSpec((LANES, GROUP_W), lambda b, c: (0, 0))
    out = pl.BlockSpec((CHUNK, GROUP_W), lambda b, c: (b * nc + c, 0))
    in_specs = [row(G_CR), row(G_CK), row(G_CV), lo_spec, st(GROUP_W), st(GROUP_W), st(GROUP_W), st(LANES),
                vec(GROUP_W), vec(GROUP_W), vec(GROUP_W), vec(LANES),
                vec(GROUP_W), mat(), mat(), vec(GROUP_W), mat(), mat(),
                vec(GROUP_W), vec(GROUP_W), vec(GROUP_W)]
    args = [proj, proj, proj, proj, *shift_parts,
            params["mu_r"], params["mu_k"], params["mu_v"], params["mu_lo"],
            params["w0"], params["w2h"], params["w2l"], params["a0"], params["a2h"], params["a2l"],
            params["k_k"], params["k_a"], params["r_k"]]
    if has_vres:
        in_specs += [pl.BlockSpec((CHUNK, LANES), lambda b, c: (b * nc + c, COL_VRES // LANES)),
                     out, vec(GROUP_W), mat(), mat()]
        args += [proj, vres["v_first"], vres["v0"], vres["w2h"], vres["w2l"]]
    big = jax.ShapeDtypeStruct((m, GROUP_W), F32)
    half = jax.ShapeDtypeStruct((m, GROUP_W), BF16)
    out_shape = [half] * 4 + [big] * 2 + [jax.ShapeDtypeStruct((bsz * nc, 1, GROUP_W), F32)] + [
        big, half, half, big] + [
        jax.ShapeDtypeStruct((bsz, 1, GROUP_W), F32)] * 3 + [jax.ShapeDtypeStruct((bsz, 1, LANES), F32)]
    out_specs = [out] * 6 + [pl.BlockSpec((1, 1, GROUP_W), lambda b, c: (b * nc + c, 0, 0))] + [out] * 4 + [
        st(GROUP_W)] * 3 + [st(LANES)]
    return pl.pallas_call(
        functools.partial(_rwkprep_kernel, has_vres=has_vres),
        grid=(bsz, nc),
        in_specs=in_specs,
        out_specs=out_specs,
        out_shape=out_shape,
        scratch_shapes=[pltpu.VMEM((1, GROUP_W), F32)] * 3 + [pltpu.VMEM((1, LANES), F32)],
        compiler_params=_params(("parallel", "arbitrary")),
        name="rwkprep",
    )(*args)


def _rwksolve_kernel(n_ref, t_ref):
    col = lax.broadcasted_iota(jnp.int32, (CHUNK, LANES), 0)

    def group_body(tg, carry):
        t0 = pl.multiple_of(tg * SUBLANES, SUBLANES)
        for r in range(SUBLANES):
            t = t0 + r

            def s_body(sg, acc, t=t):
                s0 = pl.multiple_of(sg * SUBLANES, SUBLANES)
                coef = n_ref[t, pl.ds(s0, SUBLANES), :]
                for q in range(SUBLANES):
                    acc = acc + coef[q:q + 1] * t_ref[s0 + q]
                return acc

            acc = lax.fori_loop(0, tg, s_body, jnp.where(col == t, 1.0, 0.0).astype(F32))
            coef = n_ref[t, pl.ds(t0, SUBLANES), :]
            for q in range(r):
                acc = acc + coef[q:q + 1] * t_ref[t0 + q]
            t_ref[t] = acc
        return carry

    lax.fori_loop(0, CHUNK // SUBLANES, group_body, 0)


def _rwksolve(n_bl):
    n_inst = n_bl.shape[-1]
    spec = pl.BlockSpec((CHUNK, CHUNK, LANES), lambda i: (0, 0, i))
    return pl.pallas_call(
        _rwksolve_kernel,
        grid=(n_inst // LANES,),
        in_specs=[spec],
        out_specs=spec,
        out_shape=jax.ShapeDtypeStruct(n_bl.shape, F32),
        compiler_params=_params(("parallel",)),
        name="rwksolve",
    )(n_bl)


def _rwkseq_kernel(tinv_ref, at_ref, g_ref, rt_ref, arb_ref, y0_ref, v_ref, bh_ref, kh_ref, gam_ref, bonus_ref,
                   lnw_ref, lnb_ref, s0_ref, o_ref, sfin_ref, st_scr):
    c = pl.program_id(1)

    @pl.when(c == 0)
    def _():
        st_scr[...] = s0_ref[0]

    ones = _seg_ones(LANES, C_DH)
    lane = lax.broadcasted_iota(jnp.int32, (CHUNK, LANES), 1)
    par1 = lane >= C_DH
    bd = _seg_mask(LANES, C_DH)
    pairs = [slice(p * LANES, (p + 1) * LANES) for p in range(C_WIDTH // LANES)]
    states = [st_scr[p] for p in range(len(pairs))]
    st16 = [st.astype(BF16) for st in states]
    zero = jnp.zeros((CHUNK, LANES), BF16)

    def block_diag(x):
        return jnp.concatenate([jnp.where(par1, zero, x), jnp.where(par1, x, zero)], axis=0)

    tinv16 = tinv_ref[...].astype(BF16)
    at16, g16, rt16, arb16 = at_ref[...], g_ref[...], rt_ref[...], arb_ref[...]
    wu = [_dot(tinv16[:, sl], jnp.concatenate([block_diag(at16[:, sl]), block_diag(g16[:, sl])], axis=1))
          for sl in pairs]
    u = [_dot_nt(x[:, :LANES].astype(BF16), st) + x[:, LANES:] for x, st in zip(wu, st16)]
    y_state = [_dot_nt(rt16[:, sl], st) for sl, st in zip(pairs, st16)]
    u16 = [x.astype(BF16) for x in u]
    y_u = [_dot(arb16[:, sl], block_diag(x)) for sl, x in zip(pairs, u16)]
    v16, bh16, kh16 = v_ref[...].astype(BF16), bh_ref[...], kh_ref[...]
    upd = [_dot_tn(jnp.concatenate([x, v16[:, sl]], axis=0),
                   jnp.concatenate([bh16[:, sl], kh16[:, sl]], axis=0)) for sl, x in zip(pairs, u16)]
    for p, sl in enumerate(pairs):
        st_scr[p] = states[p] * gam_ref[0, :, sl] + jnp.where(bd, upd[p], 0.0)
    y = jnp.concatenate([a + b + y0_ref[:, sl] for sl, a, b in zip(pairs, y_state, y_u)], axis=0)
    d = y - _segsum(y, ones) * (1.0 / C_DH)
    var = _segsum(d * d, ones) * (1.0 / C_DH)
    dn = d * lax.rsqrt(var + C_GN_EPS)
    for p, sl in enumerate(pairs):
        o_ref[:, sl] = dn[p * CHUNK:(p + 1) * CHUNK] * lnw_ref[:, sl] + lnb_ref[:, sl] + bonus_ref[:, sl]

    sfin_ref[0] = st_scr[...]


def _rwkseq(tinv, at, g_mat, rt, arb, y0, v, bh, kh, gam, bonus, ln_w, ln_b, s0_bd, bsz, nc):
    m = tinv.shape[0]
    row = pl.BlockSpec((CHUNK, GROUP_W), lambda b, c: (b * nc + c, 0))
    vec = pl.BlockSpec((1, GROUP_W), lambda b, c: (0, 0))
    st_spec = pl.BlockSpec((1, C_WIDTH // LANES, LANES, LANES), lambda b, c: (b, 0, 0, 0))
    return pl.pallas_call(
        _rwkseq_kernel,
        grid=(bsz, nc),
        in_specs=[row] * 9 + [pl.BlockSpec((1, 1, GROUP_W), lambda b, c: (b * nc + c, 0, 0)), row, vec, vec,
                  st_spec],
        out_specs=[row, st_spec],
        out_shape=[jax.ShapeDtypeStruct((m, GROUP_W), F32),
                   jax.ShapeDtypeStruct((bsz, C_WIDTH // LANES, LANES, LANES), F32)],
        scratch_shapes=[pltpu.VMEM((C_WIDTH // LANES, LANES, LANES), F32)],
        compiler_params=_params(("parallel", "arbitrary")),
        name="rwkseq",
    )(tinv, at, g_mat, rt, arb, y0, v, bh, kh, gam, bonus, ln_w, ln_b, s0_bd)


def _merge_kernel(x_ref, oa_ref, ob_ref, oc_ref, ag_ref, bg_ref, cg_ref, ma_ref, mb_ref, mc_ref,
                  wa_ref, wb_ref, wc_ref, wo_ref, y_ref):
    def branch(o_ref, gate_ref, w_ref):
        return _dot((o_ref[...] * _silu(gate_ref[...].astype(F32))).astype(BF16), w_ref[...])

    merged = (_sigmoid(ma_ref[...].astype(F32)) * branch(oa_ref, ag_ref, wa_ref)
              + _sigmoid(mb_ref[...].astype(F32)) * branch(ob_ref, bg_ref, wb_ref)
              + _sigmoid(mc_ref[...].astype(F32)) * branch(oc_ref, cg_ref, wc_ref))
    y_ref[...] = x_ref[...] + _dot(merged.astype(BF16), wo_ref[...])


def _merge(x2d, o_a, o_b, o_c, proj, wa, wb, wc, wo):
    m = x2d.shape[0]
    tm = min(256, m)
    row = pl.BlockSpec((tm, GROUP_W), lambda i: (i, 0))
    grp = lambda g: pl.BlockSpec((tm, GROUP_W), lambda i, g=g: (i, g))
    wsp = pl.BlockSpec((GROUP_W, D_MODEL), lambda i: (0, 0))
    return pl.pallas_call(
        _merge_kernel,
        grid=(m // tm,),
        in_specs=[row, row, row, row, grp(G_AG), grp(G_BG), grp(G_CG), grp(G_MA), grp(G_MB), grp(G_MC),
                  wsp, wsp, wsp, wsp],
        out_specs=row,
        out_shape=jax.ShapeDtypeStruct((m, D_MODEL), F32),
        compiler_params=_params(("parallel",)),
        name="merge",
    )(x2d, o_a, o_b, o_c, proj, proj, proj, proj, proj, proj, wa, wb, wc, wo)


_C_OFF = 8 * GROUP_W
_CP_R = (_C_OFF, _C_OFF + C_WIDTH)
_CP_WLO = (_CP_R[1], _CP_R[1] + C_DECAY_RANK)
_CP_K = (_CP_WLO[1], _CP_WLO[1] + C_WIDTH)
_CP_V = (_CP_K[1], _CP_K[1] + C_WIDTH)
_CP_ALO = (_CP_V[1], _CP_V[1] + C_A_RANK)
_REST = _CP_ALO[1]


def _regroup_w_in(w, vres_w1):
    cols = [w[:, :_C_OFF], w[:, _CP_R[0]:_CP_R[1]], w[:, _CP_K[0]:_CP_K[1]], w[:, _CP_V[0]:_CP_V[1]],
            w[:, _REST:], w[:, _CP_WLO[0]:_CP_WLO[1]], w[:, _CP_ALO[0]:_CP_ALO[1]]]
    vres = jnp.zeros((D_MODEL, LANES), F32)
    if vres_w1 is not None:
        vres = vres.at[:, :C_VRES_RANK].set(vres_w1)
    cols.append(vres)
    cols.append(jnp.zeros((D_MODEL, N_PROJ_PAD - N_PROJ), F32))
    return jnp.concatenate(cols, axis=1).astype(BF16)


def _split_shift(s):
    o = _C_OFF
    part = lambda a: s[:, a[0] - o:a[1] - o]
    lo = jnp.concatenate([part(_CP_WLO), part(_CP_ALO)], axis=1)
    return [x[:, None, :] for x in (part(_CP_R), part(_CP_K), part(_CP_V), lo)]


def _join_shift(r, k, v, lo):
    r, k, v, lo = (x[:, 0, :] for x in (r, k, v, lo))
    return jnp.concatenate([r, lo[:, :C_DECAY_RANK], k, v, lo[:, C_DECAY_RANK:]], axis=1)


def _pad_rows(w, row0):
    out = jnp.zeros((LANES, w.shape[1]), F32).at[row0:row0 + w.shape[0]].set(w)
    hi = out.astype(BF16)
    return hi, (out - hi.astype(F32)).astype(BF16)


def _rope_tables(pos):
    half = A_ROT // 2
    inv_freq = ROPE_THETA ** (-(jnp.arange(half, dtype=F32) * (2.0 / A_ROT)))
    ang = pos.astype(F32)[:, None] * inv_freq[None, :]
    cos, sin = jnp.cos(ang), jnp.sin(ang)
    t = pos.shape[0]
    one = jnp.ones((t, A_DQK - A_ROT), F32)
    zero = jnp.zeros((t, A_DQK - A_ROT), F32)
    z8 = jnp.zeros((t, half), F32)
    c64 = jnp.concatenate([cos, cos, one], axis=1)
    s1 = jnp.concatenate([-sin, z8, zero], axis=1)
    s2 = jnp.concatenate([z8, sin, zero], axis=1)
    tile = lambda x: jnp.concatenate([x, x], axis=1)
    return tile(c64), tile(s1), tile(s2)


def _to_lanes(x, rows):
    return x.reshape(rows, CHUNK, C_HEADS, C_DH).transpose(1, 3, 0, 2).reshape(CHUNK, C_DH, rows * C_HEADS)


def _from_lanes(x, rows):
    return x.reshape(CHUNK, C_DH, rows, C_HEADS).transpose(2, 0, 3, 1).reshape(rows * CHUNK, C_WIDTH)


def _layer(l, x2d, bsz, t_len, q_off, P, lb, past, v_first):
    nc = t_len // CHUNK
    m = bsz * t_len
    w_in = _regroup_w_in(P["w_in"][l], P["c_vres_w1"][l - 1] if l > 0 else None)
    proj = _proj(x2d, P["norm_g"][l][None, :], w_in)

    pos = q_off + jnp.arange(t_len, dtype=jnp.int32)
    cos_t, sin1_t, sin2_t = _rope_tables(pos)
    tile2 = lambda g: jnp.concatenate([g, g])[None, :]
    q16t, k32, k16, v32, v16t = _qkprep(proj, tile2(P["a_qnorm_g"][l]), tile2(P["a_knorm_g"][l]),
                                        cos_t, sin1_t, sin2_t, t_len)
    k16 = k16.reshape(bsz, t_len, GROUP_W)
    if past is not None:
        pk, pv = past[0][l], past[1][l]
        p_len = pk.shape[1]
        k16 = jnp.concatenate([pk.reshape(bsz, p_len, GROUP_W).astype(BF16), k16], axis=1)
        pv_t = jnp.swapaxes(pv.reshape(bsz, p_len, GROUP_W).astype(BF16), 1, 2)
        v16t = jnp.concatenate([pv_t, v16t], axis=2)
    lam_init = 0.8 - 0.6 * math.exp(-0.3 * l)
    lp = P["a_lambda"][l].astype(F32)
    lam = jnp.exp(jnp.sum(lp[0] * lp[1])) - jnp.exp(jnp.sum(lp[2] * lp[3])) + lam_init
    bound = (8.0 * LOG2_E * 1.02) * jnp.max(jnp.abs(P["a_qnorm_g"][l])) * jnp.max(jnp.abs(P["a_knorm_g"][l]))
    o_a = _attn(q16t, k16, v16t, jnp.stack([lam, bound]).astype(F32), P["a_subln_g"][l][None, :], q_off,
                1.0 - lam_init, bounded=past is None)
    o_a = o_a.reshape(m, GROUP_W)

    if past is None:
        s_h = jnp.zeros((bsz, B_HEADS, LANES, LANES), F32)
    else:
        s_h = jnp.swapaxes(past[2][l].astype(F32), -1, -2)
    o_b, s_h_new = _hgrn(proj, lb[l][None, :], P["b_norm_g"][l][None, :], s_h, bsz, nc)
    s_h_new = jnp.swapaxes(s_h_new, -1, -2)

    if past is None:
        shift_prev = jnp.zeros((bsz, 3 * C_WIDTH + C_DECAY_RANK + C_A_RANK), F32)
        s_r = jnp.zeros((bsz, C_HEADS, C_DH, C_DH), F32)
    else:
        shift_prev, s_r = past[4][l], past[3][l].astype(F32)
    mu = _split_shift(P["c_shift_mu"][l][None, :])
    w2h, w2l = _pad_rows(P["c_w2"][l], 0)
    a2h, a2l = _pad_rows(P["c_a2"][l], C_DECAY_RANK)
    cparams = {"mu_r": mu[0][0], "mu_k": mu[1][0], "mu_v": mu[2][0], "mu_lo": mu[3][0],
               "w0": P["c_w0"][l][None, :], "w2h": w2h, "w2l": w2l,
               "a0": P["c_a0"][l][None, :], "a2h": a2h, "a2l": a2l,
               "k_k": P["c_k_k"][l][None, :], "k_a": P["c_k_a"][l][None, :],
               "r_k": P["c_r_k"][l].reshape(1, C_WIDTH)}
    vres = None
    if l > 0:
        vh, vl = _pad_rows(P["c_vres_w2"][l - 1], 0)
        vres = {"v_first": v_first, "v0": P["c_v0"][l - 1][None, :], "w2h": vh, "w2l": vl}
    (a_t, r_t, b_h, k_h, v_c, bonus, gam, nab, arb, g_mat, y0, sh_r, sh_k, sh_v, sh_lo) = _rwkprep(
        proj, _split_shift(shift_prev), cparams, vres, bsz, nc)
    rows = bsz * nc
    n_inst = rows * C_HEADS
    pad = (-n_inst) % LANES
    to_lanes = lambda x: jnp.pad(_to_lanes(x, rows), ((0, 0), (0, 0), (0, pad))) if pad else _to_lanes(x, rows)
    tinv = _from_lanes(_rwksolve(to_lanes(nab))[:, :, :n_inst], rows)
    eye2 = jnp.eye(2, dtype=F32)
    s_bd = jnp.einsum("bpqvk,qr->bpqvrk", s_r.reshape(bsz, C_HEADS // 2, 2, C_DH, C_DH), eye2)
    s_bd = s_bd.reshape(bsz, C_HEADS // 2, LANES, LANES)
    o_c, s_bd_new = _rwkseq(tinv, a_t, g_mat, r_t, arb, y0, v_c, b_h, k_h, gam, bonus,
                            P["c_ln_w"][l][None, :], P["c_ln_b"][l][None, :], s_bd, bsz, nc)
    s_new6 = s_bd_new.reshape(bsz, C_HEADS // 2, 2, C_DH, 2, C_DH)
    s_r_new = jnp.stack([s_new6[:, :, 0, :, 0, :], s_new6[:, :, 1, :, 1, :]], axis=2)
    s_r_new = s_r_new.reshape(bsz, C_HEADS, C_DH, C_DH)
    shift_new = _join_shift(sh_r, sh_k, sh_v, sh_lo)

    bf = lambda w: w.astype(BF16)
    y = _merge(x2d, o_a, o_b, o_c, proj, bf(P["w_out_a"][l]), bf(P["w_out_b"][l]), bf(P["w_out_c"][l]),
               bf(P["w_o"][l]))
    k_rows = k32.reshape(bsz, t_len, A_HEADS, 2 * A_DQK)
    v_rows = v32.reshape(bsz, t_len, A_HEADS, 2 * A_DQK)
    return y, (k_rows, v_rows, s_h_new, s_r_new, shift_new), v_c


def _run_trunk(x, q_off, P, lb, past):
    bsz, t_len, _ = x.shape
    depth = P["w_in"].shape[0]
    x2d = x.reshape(bsz * t_len, D_MODEL)
    outs = ([], [], [], [], [])
    v_first = None
    for l in range(depth):
        x2d, entries, v_c = _layer(l, x2d, bsz, t_len, q_off, P, lb, past, v_first)
        if l == 0:
            v_first = v_c
        for lst, e in zip(outs, entries):
            lst.append(e)
    return x2d.reshape(bsz, t_len, D_MODEL), [jnp.stack(lst) for lst in outs]


def kernel(x_prompt, x_sample, cache_attn_k, cache_attn_v, state_hgrn, state_rwkv, state_rwkv_shift,
           norm_g, w_in, a_qnorm_g, a_knorm_g, a_lambda, a_subln_g, b_lower, b_norm_g,
           c_shift_mu, c_w0, c_w2, c_a0, c_a2, c_k_k, c_k_a, c_r_k, c_ln_w, c_ln_b,
           c_vres_w1, c_vres_w2, c_v0, w_out_a, w_out_b, w_out_c, w_o):
    P = {"norm_g": norm_g, "w_in": w_in, "a_qnorm_g": a_qnorm_g, "a_knorm_g": a_knorm_g,
         "a_lambda": a_lambda, "a_subln_g": a_subln_g, "b_norm_g": b_norm_g,
         "c_shift_mu": c_shift_mu, "c_w0": c_w0, "c_w2": c_w2, "c_a0": c_a0, "c_a2": c_a2,
         "c_k_k": c_k_k, "c_k_a": c_k_a, "c_r_k": c_r_k, "c_ln_w": c_ln_w, "c_ln_b": c_ln_b,
         "c_vres_w1": c_vres_w1, "c_vres_w2": c_vres_w2, "c_v0": c_v0,
         "w_out_a": w_out_a, "w_out_b": w_out_b, "w_out_c": w_out_c, "w_o": w_o}
    sm = jax.nn.softmax(b_lower.astype(F32), axis=0)
    lb = jnp.cumsum(sm, axis=0) - sm[0:1]
    past_len = cache_attn_k.shape[2]
    y_p, (k_p, v_p, hg_p, rw_p, sh_p) = _run_trunk(x_prompt, 0, P, lb, None)
    y_s, (k_s, v_s, hg_s, rw_s, sh_s) = _run_trunk(
        x_sample, past_len, P, lb, (cache_attn_k, cache_attn_v, state_hgrn, state_rwkv, state_rwkv_shift))
    return (y_p, y_s, k_p, v_p, hg_p, rw_p, sh_p, k_s, v_s, hg_s, rw_s, sh_s)
```

```python
import functools
import math

import jax
import jax.numpy as jnp
from jax import lax
from jax.experimental import pallas as pl
from jax.experimental.pallas import tpu as pltpu

F32 = jnp.float32
BF16 = jnp.bfloat16
BRANCH_DTYPE = BF16

D_MODEL = 1024
CHUNK = 64
EPS = 1e-6
NEG_BIG = -1e30
LOG2_E = 1.4426950408889634
A_HEADS = 8
A_DQK = 64
A_ROT = 16
ROPE_THETA = 500000.0
B_HEADS = 8
C_HEADS = 16
C_DH = 64
C_WIDTH = 1024
C_DECAY_RANK = 64
C_A_RANK = 64
C_VRES_RANK = 32
C_GN_EPS = 64e-5
ATTN_BOUND_LIMIT = 60.0
HGRN_SAFE_SPAN = 60.0
LANES = 128
SUBLANES = 8
GROUP_W = 1024
N_GROUPS = 15
COL_LO = N_GROUPS * GROUP_W
COL_VRES = COL_LO + LANES
N_PROJ = COL_VRES + LANES
PROJ_TN = 512
N_PROJ_PAD = -(-N_PROJ // PROJ_TN) * PROJ_TN
(G_AQ, G_AK, G_AV, G_AG, G_BQ, G_BF, G_BI, G_BG, G_CR, G_CK, G_CV, G_CG, G_MA, G_MB, G_MC) = range(N_GROUPS)
VMEM_LIMIT = 56 * 1024 * 1024


def _dot(a, b):
    return jnp.dot(a, b, preferred_element_type=F32)


def _dot_nt(a, b):
    return lax.dot_general(a, b, (((1,), (1,)), ((), ())), preferred_element_type=F32)


def _dot_tn(a, b):
    return lax.dot_general(a, b, (((0,), (0,)), ((), ())), preferred_element_type=F32)


def _split2(x):
    hi = x.astype(BF16)
    lo = (x - hi.astype(F32)).astype(BF16)
    return hi, lo


def _split3(x):
    hi = x.astype(BF16)
    r = x - hi.astype(F32)
    mid = r.astype(BF16)
    lo = (r - mid.astype(F32)).astype(BF16)
    return hi, mid, lo


def _sigmoid(x):
    return 1.0 / (1.0 + jnp.exp(-x))


def _silu(x):
    return x * _sigmoid(x)


def _seg_mask(n, seg):
    r = lax.broadcasted_iota(jnp.int32, (n, n), 0) // seg
    c = lax.broadcasted_iota(jnp.int32, (n, n), 1) // seg
    return r == c


def _seg_ones(n, seg):
    return _seg_mask(n, seg).astype(BF16)


def _segsum(x, ones_bf16):
    hi, lo = _split2(x)
    return _dot(hi, ones_bf16) + _dot(lo, ones_bf16)


def _tri_incl(n):
    r = lax.broadcasted_iota(jnp.int32, (n, n), 0)
    c = lax.broadcasted_iota(jnp.int32, (n, n), 1)
    return (c <= r).astype(BF16)


def _cumsum_rows(x, tri_bf16):
    hi, mid, lo = _split3(x)
    return _dot(tri_bf16, hi) + _dot(tri_bf16, mid) + _dot(tri_bf16, lo)


def _params(sem, vmem=None):
    return pltpu.CompilerParams(dimension_semantics=sem, vmem_limit_bytes=vmem or VMEM_LIMIT)


def _proj_kernel(x_ref, g_ref, w_ref, o_ref, h_scr):
    @pl.when(pl.program_id(1) == 0)
    def _():
        x = x_ref[...]
        ms = jnp.mean(x * x, axis=-1, keepdims=True)
        h_scr[...] = (x * lax.rsqrt(ms + EPS) * g_ref[...]).astype(BF16)

    o_ref[...] = _dot(h_scr[...], w_ref[...]).astype(o_ref.dtype)


def _proj(x2d, g, w_bf16):
    m = x2d.shape[0]
    tm = min(2048, m)
    n = w_bf16.shape[1]
    return pl.pallas_call(
        _proj_kernel,
        grid=(m // tm, n // PROJ_TN),
        in_specs=[pl.BlockSpec((tm, D_MODEL), lambda i, j: (i, 0)),
                  pl.BlockSpec((1, D_MODEL), lambda i, j: (0, 0)),
                  pl.BlockSpec((D_MODEL, PROJ_TN), lambda i, j: (0, j))],
        out_specs=pl.BlockSpec((tm, PROJ_TN), lambda i, j: (i, j)),
        out_shape=jax.ShapeDtypeStruct((m, n), BF16),
        scratch_shapes=[pltpu.VMEM((tm, D_MODEL), BF16)],
        compiler_params=_params(("parallel", "arbitrary")),
        name="proj",
    )(x2d, g, w_bf16)


def _qkprep_kernel(*refs):
    q_ref, k_ref, v_ref, qg_ref, kg_ref, c_ref, s1_ref, s2_ref = refs[:8]
    q16t_ref, k32_ref, k16_ref, v32_ref, v16t_ref = refs[-5:]
    ones = _seg_ones(LANES, A_DQK)
    cosv, sin1, sin2 = c_ref[...], s1_ref[...], s2_ref[...]

    def prep(x, gain):
        ss = _segsum(x * x, ones)
        y = x * lax.rsqrt(ss * (1.0 / A_DQK) + EPS) * gain
        return y * cosv + pltpu.roll(y, LANES - A_ROT // 2, 1) * sin1 + pltpu.roll(y, A_ROT // 2, 1) * sin2

    for c in range(GROUP_W // LANES):
        sl = slice(c * LANES, (c + 1) * LANES)
        q = prep(q_ref[:, sl].astype(F32), qg_ref[...])
        q16t_ref[0, sl, :] = (q * (A_DQK ** -0.5 * LOG2_E)).T.astype(BF16)
        k = prep(k_ref[:, sl].astype(F32), kg_ref[...])
        k32_ref[0, :, sl] = k
        k16_ref[:, sl] = k.astype(BF16)
        v = v_ref[:, sl].astype(F32)
        v32_ref[0, :, sl] = v
        v16t_ref[0, sl, :] = v.T.astype(BF16)


def _qkprep(proj, qg128, kg128, cos_t, sin1_t, sin2_t, t_len, layer, depth, kv_stack):
    m = proj.shape[0]
    tm = min(512, t_len)
    nt = t_len // tm
    row = lambda g: pl.BlockSpec((tm, GROUP_W), lambda i, g=g: (i, g))
    tab = pl.BlockSpec((tm, LANES), lambda i: (i % nt, 0))
    vec = pl.BlockSpec((1, LANES), lambda i: (0, 0))
    out = pl.BlockSpec((tm, GROUP_W), lambda i: (i, 0))
    out_t = pl.BlockSpec((1, GROUP_W, tm), lambda i: (i // nt, 0, i % nt))
    transposed = jax.ShapeDtypeStruct((m // t_len, GROUP_W, t_len), BF16)
    stack = jax.ShapeDtypeStruct((depth, m, GROUP_W), F32)
    out_stack = pl.BlockSpec((1, tm, GROUP_W), lambda i: (layer, i, 0))
    in_specs = [row(G_AQ), row(G_AK), row(G_AV), vec, vec, tab, tab, tab]
    args = [proj, proj, proj, qg128, kg128, cos_t, sin1_t, sin2_t]
    aliases = {}
    if kv_stack is not None:
        aliases = {len(args): 1, len(args) + 1: 3}
        in_specs += [pl.BlockSpec(memory_space=pl.ANY)] * 2
        args += list(kv_stack)
    return pl.pallas_call(
        _qkprep_kernel,
        grid=(m // tm,),
        in_specs=in_specs,
        out_specs=[out_t, out_stack, out, out_stack, out_t],
        out_shape=[transposed, stack, jax.ShapeDtypeStruct((m, GROUP_W), BF16), stack, transposed],
        input_output_aliases=aliases,
        compiler_params=_params(("parallel",)),
        name="qkprep",
    )(*args)


def _attn_kernel(sc_ref, qt_ref, k_ref, vt_ref, g_ref, o_ref, m1, l1, a1, m2, l2, a2,
                 *, tq, tk, q_off, nk, out_scale, bounded):
    qi = pl.program_id(2)
    q_first = q_off + qi * tq
    first_chunk_end = (q_first // CHUNK) * CHUNK + CHUNK
    last_vis = ((q_first + tq - 1) // CHUNK) * CHUNK + CHUNK - 1
    n_blocks = jnp.minimum(nk, last_vis // tk + 1)
    n_full = jnp.minimum(n_blocks, first_chunk_end // tk)

    for m, l, a in ((m1, l1, a1), (m2, l2, a2)):
        m[...] = jnp.full(m.shape, NEG_BIG, F32)
        l[...] = jnp.zeros(l.shape, F32)
        a[...] = jnp.zeros(a.shape, F32)

    qt = qt_ref[0]
    dim = lax.broadcasted_iota(jnp.int32, qt.shape, 0)
    zero = jnp.zeros_like(qt)
    q_halves = (jnp.where(dim < A_DQK, qt, zero), jnp.where(dim >= A_DQK, qt, zero))

    def scores(kj, masked):
        if nk == 1:
            k0, k, vt = 0, k_ref[0], vt_ref[0]
        else:
            k0 = pl.multiple_of(kj * tk, tk)
            k = k_ref[0, pl.ds(k0, tk), :]
            vt = vt_ref[0, :, pl.ds(k0, tk)]
        vis = None
        if masked:
            k_chunk = (k0 + lax.broadcasted_iota(jnp.int32, (tk, 1), 0)) // CHUNK
            q_chunk = (q_first + lax.broadcasted_iota(jnp.int32, (1, tq), 1)) // CHUNK
            vis = k_chunk <= q_chunk

        def score(qh):
            s = _dot(k, qh)
            return jnp.where(vis, s, NEG_BIG) if masked else s

        return score, vt

    def key_partial_sums(pr):
        return jnp.sum(pr.reshape(tk // SUBLANES, SUBLANES, tq), axis=0)

    def online_step(kj, masked):
        score, vt = scores(kj, masked)
        for qh, (m, l, a) in zip(q_halves, ((m1, l1, a1), (m2, l2, a2))):
            s = score(qh)
            m_prev = m[...]
            m_new = jnp.maximum(m_prev, jnp.max(s, axis=0, keepdims=True))
            pr = jnp.exp2(s - m_new)
            alpha = jnp.exp2(m_prev - m_new)
            l[...] = alpha * l[...] + key_partial_sums(pr)
            a[...] = alpha * a[...] + _dot(vt, pr.astype(BF16))
            m[...] = m_new

    def bounded_step(kj, masked):
        score, vt = scores(kj, masked)
        for qh, (m, l, a) in zip(q_halves, ((m1, l1, a1), (m2, l2, a2))):
            pr = jnp.exp2(score(qh) - bound)
            l[...] += key_partial_sums(pr)
            a[...] += _dot(vt, pr.astype(BF16))

    def run(step):
        def full_body(kj, carry):
            step(kj, False)
            return carry

        def masked_body(kj, carry):
            step(kj, True)
            return carry

        lax.fori_loop(0, n_full, full_body, 0)
        lax.fori_loop(n_full, n_blocks, masked_body, 0)

    lam = sc_ref[0]
    bound = sc_ref[1]
    if bounded:
        in_range = bound <= ATTN_BOUND_LIMIT

        @pl.when(in_range)
        def _():
            run(bounded_step)

        @pl.when(jnp.logical_not(in_range))
        def _():
            run(online_step)
    else:
        run(online_step)

    l1_tot = jnp.sum(l1[...], axis=0, keepdims=True)
    l2_tot = jnp.sum(l2[...], axis=0, keepdims=True)
    o = (a1[...] / l1_tot - lam * (a2[...] / l2_tot)).T
    ms = jnp.mean(o * o, axis=-1, keepdims=True)
    o_ref[0] = (o * lax.rsqrt(ms + EPS) * g_ref[...] * out_scale).astype(o_ref.dtype)


def _attn(q16t, k16, v16t, scalars, subln_g, q_off, out_scale, bounded):
    b, _, tq_len = q16t.shape
    tk_len = k16.shape[1]
    tq = min(1024, tq_len)
    tk = 512 if tk_len > 2048 else tk_len
    assert tq_len % tq == 0 and tk_len % tk == 0
    nq, nk = tq_len // tq, tk_len // tk
    kern = functools.partial(_attn_kernel, tq=tq, tk=tk, q_off=q_off, nk=nk, out_scale=out_scale,
                             bounded=bounded)
    run_max = pltpu.VMEM((1, tq), F32)
    key_sum = pltpu.VMEM((SUBLANES, tq), F32)
    acc = pltpu.VMEM((LANES, tq), F32)
    return pl.pallas_call(
        kern,
        grid=(b, A_HEADS, nq),
        in_specs=[pl.BlockSpec(memory_space=pltpu.SMEM),
                  pl.BlockSpec((1, LANES, tq), lambda bi, h, qi: (bi, h, qi)),
                  pl.BlockSpec((1, tk_len, LANES), lambda bi, h, qi: (bi, 0, h)),
                  pl.BlockSpec((1, LANES, tk_len), lambda bi, h, qi: (bi, h, 0)),
                  pl.BlockSpec((1, LANES), lambda bi, h, qi: (0, 0))],
        out_specs=pl.BlockSpec((1, tq, LANES), lambda bi, h, qi: (bi, qi, h)),
        out_shape=jax.ShapeDtypeStruct((b, tq_len, GROUP_W), BRANCH_DTYPE),
        scratch_shapes=[run_max, key_sum, acc, run_max, key_sum, acc],
        compiler_params=_params(("parallel", "parallel", "parallel")),
        name="attn",
    )(scalars, q16t, k16, v16t, subln_g)


def _hgrn_kernel(q_ref, f_ref, i_ref, lb_ref, g_ref, s0_ref, o_ref, sfin_ref,
                 st_scr, cum_scr, qk_scr, oi_scr):
    c = pl.program_id(1)

    @pl.when(c == 0)
    def _():
        st_scr[...] = s0_ref[0]

    z = f_ref[...].astype(F32)
    lb = lb_ref[...]
    log_f = jnp.log(lb + (1.0 - lb) * _sigmoid(z))
    k_in = (1.0 - lb) * _sigmoid(-z)
    q = _silu(q_ref[...].astype(F32))
    cum = _cumsum_rows(log_f, _tri_incl(CHUNK))
    row = lax.broadcasted_iota(jnp.int32, (CHUNK, 1), 0)
    head_slices = [slice(h * LANES, (h + 1) * LANES) for h in range(B_HEADS)]

    rel = cum - cum[CHUNK // 2 - 1:CHUNK // 2, :]
    safe = jnp.max(jnp.abs(rel)) <= HGRN_SAFE_SPAN

    @pl.when(safe)
    def _():
        qe = (q * jnp.exp(rel)).astype(BF16)
        ke = (k_in * jnp.exp(-rel)).astype(BF16)
        causal = lax.broadcasted_iota(jnp.int32, (CHUNK, CHUNK), 1) <= lax.broadcasted_iota(
            jnp.int32, (CHUNK, CHUNK), 0)
        scores = [jnp.where(causal, _dot_nt(qe[:, sl], ke[:, sl]), 0.0).astype(BF16) for sl in head_slices]
        i16 = i_ref[...].astype(BF16)
        for sl, sc in zip(head_slices, scores):
            oi_scr[:, sl] = _dot(sc, i16[:, sl])

    @pl.when(jnp.logical_not(safe))
    def _():
        cum_scr[...] = cum
        qk_scr[...] = q
        for h in range(B_HEADS):
            sl = slice(h * LANES, (h + 1) * LANES)
            cum_h = cum[:, sl]
            kin_h = k_in[:, sl]
            i_h = i_ref[:, sl].astype(F32)

            def body(g, carry, sl=sl, cum_h=cum_h, kin_h=kin_h, i_h=i_h):
                g8 = pl.multiple_of(g * SUBLANES, SUBLANES)
                c_tile = cum_scr[pl.ds(g8, SUBLANES), sl]
                q_tile = qk_scr[pl.ds(g8, SUBLANES), sl]
                rows = []
                for r in range(SUBLANES):
                    dec = jnp.exp(jnp.minimum(c_tile[r:r + 1] - cum_h, 0.0))
                    col = jnp.sum(dec * (kin_h * q_tile[r:r + 1]), axis=1, keepdims=True)
                    col = jnp.where(row <= g8 + r, col, 0.0)
                    rows.append(jnp.sum(col * i_h, axis=0, keepdims=True))
                oi_scr[pl.ds(g8, SUBLANES), sl] = jnp.concatenate(rows, axis=0)
                return carry

            lax.fori_loop(0, CHUNK // SUBLANES, body, 0)

    cum_last = cum[CHUNK - 1:CHUNK, :]
    q_dec = (q * jnp.exp(cum)).astype(BF16)
    k_tail = (k_in * jnp.exp(cum_last - cum)).astype(BF16)
    i16 = i_ref[...].astype(BF16)
    decay = jnp.exp(cum_last)
    states = [st_scr[h] for h in range(B_HEADS)]
    inter = [_dot_nt(q_dec[:, sl], st.astype(BF16)) for sl, st in zip(head_slices, states)]
    update = [_dot_tn(i16[:, sl], k_tail[:, sl]) for sl in head_slices]
    for h, sl in enumerate(head_slices):
        st_scr[h] = states[h] * decay[:, sl] + update[h]
        o = oi_scr[:, sl] + inter[h]
        ms = jnp.mean(o * o, axis=-1, keepdims=True)
        o_ref[:, sl] = (o * lax.rsqrt(ms + EPS) * g_ref[...]).astype(o_ref.dtype)

    sfin_ref[0] = st_scr[...]


def _hgrn(proj, lb, norm_g128, s0_t, bsz, nc):
    m = proj.shape[0]
    row = lambda g: pl.BlockSpec((CHUNK, GROUP_W), lambda b, c, g=g: (b * nc + c, g))
    st_spec = pl.BlockSpec((1, B_HEADS, LANES, LANES), lambda b, c: (b, 0, 0, 0))
    return pl.pallas_call(
        _hgrn_kernel,
        grid=(bsz, nc),
        in_specs=[row(G_BQ), row(G_BF), row(G_BI),
                  pl.BlockSpec((1, GROUP_W), lambda b, c: (0, 0)),
                  pl.BlockSpec((1, LANES), lambda b, c: (0, 0)),
                  st_spec],
        out_specs=[pl.BlockSpec((CHUNK, GROUP_W), lambda b, c: (b * nc + c, 0)), st_spec],
        out_shape=[jax.ShapeDtypeStruct((m, GROUP_W), BRANCH_DTYPE),
                   jax.ShapeDtypeStruct((bsz, B_HEADS, LANES, LANES), F32)],
        scratch_shapes=[pltpu.VMEM((B_HEADS, LANES, LANES), F32),
                        pltpu.VMEM((CHUNK, GROUP_W), F32),
                        pltpu.VMEM((CHUNK, GROUP_W), F32),
                        pltpu.VMEM((CHUNK, GROUP_W), F32)],
        compiler_params=_params(("parallel", "arbitrary")),
        name="hgrn",
    )(proj, proj, proj, lb, norm_g128, s0_t)


def _rwkprep_kernel(*refs, has_vres):
    (cr_ref, ck_ref, cv_ref, clo_ref, spr_ref, spk_ref, spv_ref, splo_ref,
     mur_ref, muk_ref, muv_ref, mulo_ref, w0_ref, w2h_ref, w2l_ref, a0_ref, a2h_ref, a2l_ref,
     kk_ref, ka_ref, rk_ref) = refs[:21]
    pos = 21
    if has_vres:
        vres_ref, vf_ref, v0_ref, vw2h_ref, vw2l_ref = refs[pos:pos + 5]
        pos += 5
    (at_ref, rt_ref, bh_ref, kh_ref, vc_ref, bonus_ref, gam_ref, nab_ref, arb_ref, g_ref, y0_ref,
     shr_ref, shk_ref, shv_ref, shlo_ref) = refs[pos:pos + 15]
    pr_scr, pk_scr, pv_scr, plo_scr = refs[pos + 15:]
    c = pl.program_id(1)

    @pl.when(c == 0)
    def _():
        pr_scr[...] = spr_ref[0]
        pk_scr[...] = spk_ref[0]
        pv_scr[...] = spv_ref[0]
        plo_scr[...] = splo_ref[0]

    def shifted(x_ref, prev_scr, mu_ref, last_ref):
        x = x_ref[...].astype(F32)
        row = lax.broadcasted_iota(jnp.int32, x.shape, 0)
        prev = jnp.where(row == 0, prev_scr[...], pltpu.roll(x, 1, 0))
        last = x[CHUNK - 1:CHUNK, :]
        prev_scr[...] = last
        last_ref[0] = last
        return x + (prev - x) * mu_ref[...]

    r = shifted(cr_ref, pr_scr, mur_ref, shr_ref)
    k0 = shifted(ck_ref, pk_scr, muk_ref, shk_ref)
    v = shifted(cv_ref, pv_scr, muv_ref, shv_ref)
    lo = shifted(clo_ref, plo_scr, mulo_ref, shlo_ref)

    def lowrank(x, wh_ref, wl_ref):
        xh, xl = _split2(x)
        return _dot(xh, wh_ref[...]) + _dot(xl, wh_ref[...]) + _dot(xh, wl_ref[...])

    w_in = w0_ref[...] + lowrank(jnp.tanh(lo), w2h_ref, w2l_ref)
    nw = -w_in
    softplus = jnp.maximum(nw, 0.0) + jnp.log(1.0 + jnp.exp(-jnp.abs(nw)))
    log_decay = -jnp.exp(-softplus - 0.5)
    a_sig = _sigmoid(a0_ref[...] + lowrank(lo, a2h_ref, a2l_ref))
    if has_vres:
        v_mix = _sigmoid(v0_ref[...] + lowrank(vres_ref[...].astype(F32), vw2h_ref, vw2l_ref))
        v = v + (vf_ref[...] - v) * v_mix
    vc_ref[...] = v

    ones = _seg_ones(LANES, C_DH)
    cum = _cumsum_rows(log_decay, _tri_incl(CHUNK))
    cum_last = cum[CHUNK - 1:CHUNK, :]
    gam_ref[0] = jnp.exp(cum_last)
    e_prev = jnp.exp(cum - log_decay)
    e_cum = jnp.exp(cum)
    e_inv = jnp.exp(-cum)
    e_tail = jnp.exp(cum_last - cum)

    lane = lax.broadcasted_iota(jnp.int32, (CHUNK, LANES), 1)
    par1 = lane >= C_DH
    tcol = lax.broadcasted_iota(jnp.int32, (CHUNK, LANES), 0)
    scol = lane % C_DH
    strict = jnp.concatenate([scol < tcol, scol < tcol], axis=1)
    incl = jnp.concatenate([scol <= tcol, scol <= tcol], axis=1)
    low_mask = jnp.concatenate([strict, incl], axis=0)

    pairs = [slice(p * LANES, (p + 1) * LANES) for p in range(C_WIDTH // LANES)]

    def head_sums(x):
        stacked = _segsum(jnp.concatenate([x[:, sl] for sl in pairs], axis=0), ones)
        return jnp.concatenate([stacked[p * CHUNK:(p + 1) * CHUNK] for p in range(len(pairs))], axis=1)

    kk = k0 * kk_ref[...]
    kk = kk / jnp.maximum(jnp.sqrt(head_sums(kk * kk)), 1e-12)
    k = k0 * (1.0 + (a_sig - 1.0) * ka_ref[...])
    b_vec = kk * a_sig
    bonus_ref[...] = head_sums(r * k * rk_ref[...]) * v
    a_t = -kk * e_prev
    r_t = r * e_cum
    a16, r16 = a_t.astype(BF16), r_t.astype(BF16)
    at_ref[...] = a16
    rt_ref[...] = r16
    bh_ref[...] = (b_vec * e_tail).astype(BF16)
    kh_ref[...] = (k * e_tail).astype(BF16)
    b16, k16, v16 = (b_vec * e_inv).astype(BF16), (k * e_inv).astype(BF16), v.astype(BF16)
    zero = jnp.zeros((CHUNK, LANES), BF16)

    def block_diag(x):
        return [jnp.where(par1, zero, x), jnp.where(par1, x, zero)]

    prods = [jnp.where(low_mask,
                       _dot_nt(jnp.concatenate([a16[:, sl], r16[:, sl]], axis=0),
                               jnp.concatenate(block_diag(b16[:, sl]) + block_diag(k16[:, sl]), axis=0)),
                       0.0) for sl in pairs]
    gys = [_dot(prod[:, LANES:].astype(BF16), jnp.concatenate(block_diag(v16[:, sl]), axis=0))
           for sl, prod in zip(pairs, prods)]
    for sl, prod, gy in zip(pairs, prods, gys):
        nab_ref[:, sl] = prod[:CHUNK, :LANES]
        arb_ref[:, sl] = prod[CHUNK:, :LANES].astype(BF16)
        g_ref[:, sl] = gy[:CHUNK].astype(BF16)
        y0_ref[:, sl] = gy[CHUNK:]


def _rwkprep(proj, shift_parts, params, vres, bsz, nc):
    m = proj.shape[0]
    has_vres = vres is not None
    row = lambda g: pl.BlockSpec((CHUNK, GROUP_W), lambda b, c, g=g: (b * nc + c, g))
    lo_spec = pl.BlockSpec((CHUNK, LANES), lambda b, c: (b * nc + c, COL_LO // LANES))
    st = lambda w: pl.BlockSpec((1, 1, w), lambda b, c: (b, 0, 0))
    vec = lambda w: pl.BlockSpec((1, w), lambda b, c: (0, 0))
    mat = lambda: pl.BlockSpec((LANES, GROUP_W), lambda b, c: (0, 0))
    out = pl.BlockSpec((CHUNK, GROUP_W), lambda b, c: (b * nc + c, 0))
    in_specs = [row(G_CR), row(G_CK), row(G_CV), lo_spec, st(GROUP_W), st(GROUP_W), st(GROUP_W), st(LANES),
                vec(GROUP_W), vec(GROUP_W), vec(GROUP_W), vec(LANES),
                vec(GROUP_W), mat(), mat(), vec(GROUP_W), mat(), mat(),
                vec(GROUP_W), vec(GROUP_W), vec(GROUP_W)]
    args = [proj, proj, proj, proj, *shift_parts,
            params["mu_r"], params["mu_k"], params["mu_v"], params["mu_lo"],
            params["w0"], params["w2h"], params["w2l"], params["a0"], params["a2h"], params["a2l"],
            params["k_k"], params["k_a"], params["r_k"]]
    if has_vres:
        in_specs += [pl.BlockSpec((CHUNK, LANES), lambda b, c: (b * nc + c, COL_VRES // LANES)),
                     out, vec(GROUP_W), mat(), mat()]
        args += [proj, vres["v_first"], vres["v0"], vres["w2h"], vres["w2l"]]
    big = jax.ShapeDtypeStruct((m, GROUP_W), F32)
    half = jax.ShapeDtypeStruct((m, GROUP_W), BF16)
    out_shape = [half] * 4 + [big] * 2 + [jax.ShapeDtypeStruct((bsz * nc, 1, GROUP_W), F32)] + [
        big, half, half, big] + [
        jax.ShapeDtypeStruct((bsz, 1, GROUP_W), F32)] * 3 + [jax.ShapeDtypeStruct((bsz, 1, LANES), F32)]
    out_specs = [out] * 6 + [pl.BlockSpec((1, 1, GROUP_W), lambda b, c: (b * nc + c, 0, 0))] + [out] * 4 + [
        st(GROUP_W)] * 3 + [st(LANES)]
    return pl.pallas_call(
        functools.partial(_rwkprep_kernel, has_vres=has_vres),
        grid=(bsz, nc),
        in_specs=in_specs,
        out_specs=out_specs,
        out_shape=out_shape,
        scratch_shapes=[pltpu.VMEM((1, GROUP_W), F32)] * 3 + [pltpu.VMEM((1, LANES), F32)],
        compiler_params=_params(("parallel", "arbitrary")),
        name="rwkprep",
    )(*args)


def _rwksolve_kernel(n_ref, out_ref, t_ref):
    col = lax.broadcasted_iota(jnp.int32, (CHUNK, LANES), 0)

    def group_body(tg, carry):
        t0 = pl.multiple_of(tg * SUBLANES, SUBLANES)
        for r in range(SUBLANES):
            t = t0 + r

            def s_body(sg, acc, t=t):
                s0 = pl.multiple_of(sg * SUBLANES, SUBLANES)
                coef = n_ref[t, pl.ds(s0, SUBLANES), :]
                for q in range(SUBLANES):
                    acc = acc + coef[q:q + 1] * t_ref[s0 + q]
                return acc

            acc = lax.fori_loop(0, tg, s_body, jnp.where(col == t, 1.0, 0.0).astype(F32))
            coef = n_ref[t, pl.ds(t0, SUBLANES), :]
            for q in range(r):
                acc = acc + coef[q:q + 1] * t_ref[t0 + q]
            t_ref[t] = acc
        return carry

    lax.fori_loop(0, CHUNK // SUBLANES, group_body, 0)
    out_ref[...] = t_ref[...].astype(out_ref.dtype)


def _rwksolve(n_bl):
    n_inst = n_bl.shape[-1]
    spec = pl.BlockSpec((CHUNK, CHUNK, LANES), lambda i: (0, 0, i))
    return pl.pallas_call(
        _rwksolve_kernel,
        grid=(n_inst // LANES,),
        in_specs=[spec],
        out_specs=spec,
        out_shape=jax.ShapeDtypeStruct(n_bl.shape, BF16),
        scratch_shapes=[pltpu.VMEM((CHUNK, CHUNK, LANES), F32)],
        compiler_params=_params(("parallel",)),
        name="rwksolve",
    )(n_bl)


def _rwkseq_kernel(tinv_ref, at_ref, g_ref, rt_ref, arb_ref, y0_ref, v_ref, bh_ref, kh_ref, gam_ref, bonus_ref,
                   lnw_ref, lnb_ref, s0_ref, o_ref, sfin_ref, st_scr):
    c = pl.program_id(1)

    @pl.when(c == 0)
    def _():
        st_scr[...] = s0_ref[0]

    ones = _seg_ones(LANES, C_DH)
    lane = lax.broadcasted_iota(jnp.int32, (CHUNK, LANES), 1)
    par1 = lane >= C_DH
    bd = _seg_mask(LANES, C_DH)
    pairs = [slice(p * LANES, (p + 1) * LANES) for p in range(C_WIDTH // LANES)]
    states = [st_scr[p] for p in range(len(pairs))]
    st16 = [st.astype(BF16) for st in states]
    zero = jnp.zeros((CHUNK, LANES), BF16)

    def block_diag(x):
        return jnp.concatenate([jnp.where(par1, zero, x), jnp.where(par1, x, zero)], axis=0)

    tinv16 = tinv_ref[...]
    at16, g16, rt16, arb16 = at_ref[...], g_ref[...], rt_ref[...], arb_ref[...]
    wu = [_dot(tinv16[:, sl], jnp.concatenate([block_diag(at16[:, sl]), block_diag(g16[:, sl])], axis=1))
          for sl in pairs]
    u = [_dot_nt(x[:, :LANES].astype(BF16), st) + x[:, LANES:] for x, st in zip(wu, st16)]
    y_state = [_dot_nt(rt16[:, sl], st) for sl, st in zip(pairs, st16)]
    u16 = [x.astype(BF16) for x in u]
    y_u = [_dot(arb16[:, sl], block_diag(x)) for sl, x in zip(pairs, u16)]
    v16, bh16, kh16 = v_ref[...].astype(BF16), bh_ref[...], kh_ref[...]
    upd = [_dot_tn(jnp.concatenate([x, v16[:, sl]], axis=0),
                   jnp.concatenate([bh16[:, sl], kh16[:, sl]], axis=0)) for sl, x in zip(pairs, u16)]
    for p, sl in enumerate(pairs):
        st_scr[p] = states[p] * gam_ref[0, :, sl] + jnp.where(bd, upd[p], 0.0)
    y = jnp.concatenate([a + b + y0_ref[:, sl] for sl, a, b in zip(pairs, y_state, y_u)], axis=0)
    d = y - _segsum(y, ones) * (1.0 / C_DH)
    var = _segsum(d * d, ones) * (1.0 / C_DH)
    dn = d * lax.rsqrt(var + C_GN_EPS)
    for p, sl in enumerate(pairs):
        o_ref[:, sl] = (dn[p * CHUNK:(p + 1) * CHUNK] * lnw_ref[:, sl] + lnb_ref[:, sl]
                        + bonus_ref[:, sl]).astype(o_ref.dtype)

    sfin_ref[0] = st_scr[...]


def _rwkseq(tinv, at, g_mat, rt, arb, y0, v, bh, kh, gam, bonus, ln_w, ln_b, s0_bd, bsz, nc):
    m = tinv.shape[0]
    row = pl.BlockSpec((CHUNK, GROUP_W), lambda b, c: (b * nc + c, 0))
    vec = pl.BlockSpec((1, GROUP_W), lambda b, c: (0, 0))
    st_spec = pl.BlockSpec((1, C_WIDTH // LANES, LANES, LANES), lambda b, c: (b, 0, 0, 0))
    return pl.pallas_call(
        _rwkseq_kernel,
        grid=(bsz, nc),
        in_specs=[row] * 9 + [pl.BlockSpec((1, 1, GROUP_W), lambda b, c: (b * nc + c, 0, 0)), row, vec, vec,
                  st_spec],
        out_specs=[row, st_spec],
        out_shape=[jax.ShapeDtypeStruct((m, GROUP_W), BRANCH_DTYPE),
                   jax.ShapeDtypeStruct((bsz, C_WIDTH // LANES, LANES, LANES), F32)],
        scratch_shapes=[pltpu.VMEM((C_WIDTH // LANES, LANES, LANES), F32)],
        compiler_params=_params(("parallel", "arbitrary")),
        name="rwkseq",
    )(tinv, at, g_mat, rt, arb, y0, v, bh, kh, gam, bonus, ln_w, ln_b, s0_bd)


def _merge_kernel(x_ref, oa_ref, ob_ref, oc_ref, ag_ref, bg_ref, cg_ref, ma_ref, mb_ref, mc_ref,
                  wa_ref, wb_ref, wc_ref, wo_ref, y_ref):
    def branch(o_ref, gate_ref, w_ref):
        return _dot((o_ref[...] * _silu(gate_ref[...].astype(F32))).astype(BF16), w_ref[...])

    merged = (_sigmoid(ma_ref[...].astype(F32)) * branch(oa_ref, ag_ref, wa_ref)
              + _sigmoid(mb_ref[...].astype(F32)) * branch(ob_ref, bg_ref, wb_ref)
              + _sigmoid(mc_ref[...].astype(F32)) * branch(oc_ref, cg_ref, wc_ref))
    y_ref[...] = x_ref[...] + _dot(merged.astype(BF16), wo_ref[...])


def _merge(x2d, o_a, o_b, o_c, proj, wa, wb, wc, wo):
    m = x2d.shape[0]
    tm = min(256, m)
    row = pl.BlockSpec((tm, GROUP_W), lambda i: (i, 0))
    grp = lambda g: pl.BlockSpec((tm, GROUP_W), lambda i, g=g: (i, g))
    wsp = pl.BlockSpec((GROUP_W, D_MODEL), lambda i: (0, 0))
    return pl.pallas_call(
        _merge_kernel,
        grid=(m // tm,),
        in_specs=[row, row, row, row, grp(G_AG), grp(G_BG), grp(G_CG), grp(G_MA), grp(G_MB), grp(G_MC),
                  wsp, wsp, wsp, wsp],
        out_specs=row,
        out_shape=jax.ShapeDtypeStruct((m, D_MODEL), F32),
        compiler_params=_params(("parallel",)),
        name="merge",
    )(x2d, o_a, o_b, o_c, proj, proj, proj, proj, proj, proj, wa, wb, wc, wo)


_C_OFF = 8 * GROUP_W
_CP_R = (_C_OFF, _C_OFF + C_WIDTH)
_CP_WLO = (_CP_R[1], _CP_R[1] + C_DECAY_RANK)
_CP_K = (_CP_WLO[1], _CP_WLO[1] + C_WIDTH)
_CP_V = (_CP_K[1], _CP_K[1] + C_WIDTH)
_CP_ALO = (_CP_V[1], _CP_V[1] + C_A_RANK)
_REST = _CP_ALO[1]


def _regroup_w_in(w, vres_w1):
    cols = [w[:, :_C_OFF], w[:, _CP_R[0]:_CP_R[1]], w[:, _CP_K[0]:_CP_K[1]], w[:, _CP_V[0]:_CP_V[1]],
            w[:, _REST:], w[:, _CP_WLO[0]:_CP_WLO[1]], w[:, _CP_ALO[0]:_CP_ALO[1]]]
    vres = jnp.zeros((D_MODEL, LANES), F32)
    if vres_w1 is not None:
        vres = vres.at[:, :C_VRES_RANK].set(vres_w1)
    cols.append(vres)
    cols.append(jnp.zeros((D_MODEL, N_PROJ_PAD - N_PROJ), F32))
    return jnp.concatenate(cols, axis=1).astype(BF16)


def _split_shift(s):
    o = _C_OFF
    part = lambda a: s[:, a[0] - o:a[1] - o]
    lo = jnp.concatenate([part(_CP_WLO), part(_CP_ALO)], axis=1)
    return [x[:, None, :] for x in (part(_CP_R), part(_CP_K), part(_CP_V), lo)]


def _join_shift(r, k, v, lo):
    r, k, v, lo = (x[:, 0, :] for x in (r, k, v, lo))
    return jnp.concatenate([r, lo[:, :C_DECAY_RANK], k, v, lo[:, C_DECAY_RANK:]], axis=1)


def _pad_rows(w, row0):
    out = jnp.zeros((LANES, w.shape[1]), F32).at[row0:row0 + w.shape[0]].set(w)
    hi = out.astype(BF16)
    return hi, (out - hi.astype(F32)).astype(BF16)


def _rope_tables(pos):
    half = A_ROT // 2
    inv_freq = ROPE_THETA ** (-(jnp.arange(half, dtype=F32) * (2.0 / A_ROT)))
    ang = pos.astype(F32)[:, None] * inv_freq[None, :]
    cos, sin = jnp.cos(ang), jnp.sin(ang)
    t = pos.shape[0]
    one = jnp.ones((t, A_DQK - A_ROT), F32)
    zero = jnp.zeros((t, A_DQK - A_ROT), F32)
    z8 = jnp.zeros((t, half), F32)
    c64 = jnp.concatenate([cos, cos, one], axis=1)
    s1 = jnp.concatenate([-sin, z8, zero], axis=1)
    s2 = jnp.concatenate([z8, sin, zero], axis=1)
    tile = lambda x: jnp.concatenate([x, x], axis=1)
    return tile(c64), tile(s1), tile(s2)


def _to_lanes(x, rows):
    return x.reshape(rows, CHUNK, C_HEADS, C_DH).transpose(1, 3, 0, 2).reshape(CHUNK, C_DH, rows * C_HEADS)


def _from_lanes(x, rows):
    return x.reshape(CHUNK, C_DH, rows, C_HEADS).transpose(2, 0, 3, 1).reshape(rows * CHUNK, C_WIDTH)


def _layer(l, x2d, bsz, t_len, q_off, P, lb, past, v_first, kv_stack):
    nc = t_len // CHUNK
    m = bsz * t_len
    w_in = _regroup_w_in(P["w_in"][l], P["c_vres_w1"][l - 1] if l > 0 else None)
    proj = _proj(x2d, P["norm_g"][l][None, :], w_in)

    pos = q_off + jnp.arange(t_len, dtype=jnp.int32)
    cos_t, sin1_t, sin2_t = _rope_tables(pos)
    tile2 = lambda g: jnp.concatenate([g, g])[None, :]
    q16t, k_stack, k16, v_stack, v16t = _qkprep(proj, tile2(P["a_qnorm_g"][l]), tile2(P["a_knorm_g"][l]),
                                                cos_t, sin1_t, sin2_t, t_len, l, P["w_in"].shape[0], kv_stack)
    k16 = k16.reshape(bsz, t_len, GROUP_W)
    if past is not None:
        pk, pv = past[0][l], past[1][l]
        p_len = pk.shape[1]
        k16 = jnp.concatenate([pk.reshape(bsz, p_len, GROUP_W).astype(BF16), k16], axis=1)
        pv_t = jnp.swapaxes(pv.reshape(bsz, p_len, GROUP_W).astype(BF16), 1, 2)
        v16t = jnp.concatenate([pv_t, v16t], axis=2)
    lam_init = 0.8 - 0.6 * math.exp(-0.3 * l)
    lp = P["a_lambda"][l].astype(F32)
    lam = jnp.exp(jnp.sum(lp[0] * lp[1])) - jnp.exp(jnp.sum(lp[2] * lp[3])) + lam_init
    bound = (8.0 * LOG2_E * 1.02) * jnp.max(jnp.abs(P["a_qnorm_g"][l])) * jnp.max(jnp.abs(P["a_knorm_g"][l]))
    o_a = _attn(q16t, k16, v16t, jnp.stack([lam, bound]).astype(F32), P["a_subln_g"][l][None, :], q_off,
                1.0 - lam_init, bounded=past is None)
    o_a = o_a.reshape(m, GROUP_W)

    if past is None:
        s_h = jnp.zeros((bsz, B_HEADS, LANES, LANES), F32)
    else:
        s_h = jnp.swapaxes(past[2][l].astype(F32), -1, -2)
    o_b, s_h_new = _hgrn(proj, lb[l][None, :], P["b_norm_g"][l][None, :], s_h, bsz, nc)
    s_h_new = jnp.swapaxes(s_h_new, -1, -2)

    if past is None:
        shift_prev = jnp.zeros((bsz, 3 * C_WIDTH + C_DECAY_RANK + C_A_RANK), F32)
        s_r = jnp.zeros((bsz, C_HEADS, C_DH, C_DH), F32)
    else:
        shift_prev, s_r = past[4][l], past[3][l].astype(F32)
    mu = _split_shift(P["c_shift_mu"][l][None, :])
    w2h, w2l = _pad_rows(P["c_w2"][l], 0)
    a2h, a2l = _pad_rows(P["c_a2"][l], C_DECAY_RANK)
    cparams = {"mu_r": mu[0][0], "mu_k": mu[1][0], "mu_v": mu[2][0], "mu_lo": mu[3][0],
               "w0": P["c_w0"][l][None, :], "w2h": w2h, "w2l": w2l,
               "a0": P["c_a0"][l][None, :], "a2h": a2h, "a2l": a2l,
               "k_k": P["c_k_k"][l][None, :], "k_a": P["c_k_a"][l][None, :],
               "r_k": P["c_r_k"][l].reshape(1, C_WIDTH)}
    vres = None
    if l > 0:
        vh, vl = _pad_rows(P["c_vres_w2"][l - 1], 0)
        vres = {"v_first": v_first, "v0": P["c_v0"][l - 1][None, :], "w2h": vh, "w2l": vl}
    (a_t, r_t, b_h, k_h, v_c, bonus, gam, nab, arb, g_mat, y0, sh_r, sh_k, sh_v, sh_lo) = _rwkprep(
        proj, _split_shift(shift_prev), cparams, vres, bsz, nc)
    rows = bsz * nc
    n_inst = rows * C_HEADS
    pad = (-n_inst) % LANES
    to_lanes = lambda x: jnp.pad(_to_lanes(x, rows), ((0, 0), (0, 0), (0, pad))) if pad else _to_lanes(x, rows)
    tinv = _from_lanes(_rwksolve(to_lanes(nab))[:, :, :n_inst], rows)
    eye2 = jnp.eye(2, dtype=F32)
    s_bd = jnp.einsum("bpqvk,qr->bpqvrk", s_r.reshape(bsz, C_HEADS // 2, 2, C_DH, C_DH), eye2)
    s_bd = s_bd.reshape(bsz, C_HEADS // 2, LANES, LANES)
    o_c, s_bd_new = _rwkseq(tinv, a_t, g_mat, r_t, arb, y0, v_c, b_h, k_h, gam, bonus,
                            P["c_ln_w"][l][None, :], P["c_ln_b"][l][None, :], s_bd, bsz, nc)
    s_new6 = s_bd_new.reshape(bsz, C_HEADS // 2, 2, C_DH, 2, C_DH)
    s_r_new = jnp.stack([s_new6[:, :, 0, :, 0, :], s_new6[:, :, 1, :, 1, :]], axis=2)
    s_r_new = s_r_new.reshape(bsz, C_HEADS, C_DH, C_DH)
    shift_new = _join_shift(sh_r, sh_k, sh_v, sh_lo)

    bf = lambda w: w.astype(BF16)
    y = _merge(x2d, o_a, o_b, o_c, proj, bf(P["w_out_a"][l]), bf(P["w_out_b"][l]), bf(P["w_out_c"][l]),
               bf(P["w_o"][l]))
    return y, (s_h_new, s_r_new, shift_new), v_c, (k_stack, v_stack)


def _run_trunk(x, q_off, P, lb, past):
    bsz, t_len, _ = x.shape
    depth = P["w_in"].shape[0]
    x2d = x.reshape(bsz * t_len, D_MODEL)
    outs = ([], [], [])
    v_first, kv_stack = None, None
    for l in range(depth):
        x2d, entries, v_c, kv_stack = _layer(l, x2d, bsz, t_len, q_off, P, lb, past, v_first, kv_stack)
        if l == 0:
            v_first = v_c
        for lst, e in zip(outs, entries):
            lst.append(e)
    kv_rows = [s.reshape(depth, bsz, t_len, A_HEADS, 2 * A_DQK) for s in kv_stack]
    return x2d.reshape(bsz, t_len, D_MODEL), kv_rows + [jnp.stack(lst) for lst in outs]


def kernel(x_prompt, x_sample, cache_attn_k, cache_attn_v, state_hgrn, state_rwkv, state_rwkv_shift,
           norm_g, w_in, a_qnorm_g, a_knorm_g, a_lambda, a_subln_g, b_lower, b_norm_g,
           c_shift_mu, c_w0, c_w2, c_a0, c_a2, c_k_k, c_k_a, c_r_k, c_ln_w, c_ln_b,
           c_vres_w1, c_vres_w2, c_v0, w_out_a, w_out_b, w_out_c, w_o):
    P = {"norm_g": norm_g, "w_in": w_in, "a_qnorm_g": a_qnorm_g, "a_knorm_g": a_knorm_g,
         "a_lambda": a_lambda, "a_subln_g": a_subln_g, "b_norm_g": b_norm_g,
         "c_shift_mu": c_shift_mu, "c_w0": c_w0, "c_w2": c_w2, "c_a0": c_a0, "c_a2": c_a2,
         "c_k_k": c_k_k, "c_k_a": c_k_a, "c_r_k": c_r_k, "c_ln_w": c_ln_w, "c_ln_b": c_ln_b,
         "c_vres_w1": c_vres_w1, "c_vres_w2": c_vres_w2, "c_v0": c_v0,
         "w_out_a": w_out_a, "w_out_b": w_out_b, "w_out_c": w_out_c, "w_o": w_o}
    sm = jax.nn.softmax(b_lower.astype(F32), axis=0)
    lb = jnp.cumsum(sm, axis=0) - sm[0:1]
    past_len = cache_attn_k.shape[2]
    y_p, (k_p, v_p, hg_p, rw_p, sh_p) = _run_trunk(x_prompt, 0, P, lb, None)
    y_s, (k_s, v_s, hg_s, rw_s, sh_s) = _run_trunk(
        x_sample, past_len, P, lb, (cache_attn_k, cache_attn_v, state_hgrn, state_rwkv, state_rwkv_shift))
    return (y_p, y_s, k_p, v_p, hg_p, rw_p, sh_p, k_s, v_s, hg_s, rw_s, sh_s)
```

```python
import functools
import math

import jax
import jax.numpy as jnp
from jax import lax
from jax.experimental import pallas as pl
from jax.experimental.pallas import tpu as pltpu

F32 = jnp.float32
BF16 = jnp.bfloat16
BRANCH_DTYPE = BF16

D_MODEL = 1024
CHUNK = 64
EPS = 1e-6
NEG_BIG = -1e30
LOG2_E = 1.4426950408889634
A_HEADS = 8
A_DQK = 64
A_ROT = 16
ROPE_THETA = 500000.0
B_HEADS = 8
C_HEADS = 16
C_DH = 64
C_WIDTH = 1024
C_DECAY_RANK = 64
C_A_RANK = 64
C_VRES_RANK = 32
C_GN_EPS = 64e-5
ATTN_BOUND_LIMIT = 60.0
HGRN_SAFE_SPAN = 60.0
LANES = 128
SUBLANES = 8
SEQ_SUB_CHUNKS = 8
GROUP_W = 1024
N_GROUPS = 15
COL_LO = N_GROUPS * GROUP_W
COL_VRES = COL_LO + LANES
N_PROJ = COL_VRES + LANES
PROJ_TN = 512
N_PROJ_PAD = -(-N_PROJ // PROJ_TN) * PROJ_TN
(G_AQ, G_AK, G_AV, G_AG, G_BQ, G_BF, G_BI, G_BG, G_CR, G_CK, G_CV, G_CG, G_MA, G_MB, G_MC) = range(N_GROUPS)
VMEM_LIMIT = 56 * 1024 * 1024


def _dot(a, b):
    return jnp.dot(a, b, preferred_element_type=F32)


def _dot_nt(a, b):
    return lax.dot_general(a, b, (((1,), (1,)), ((), ())), preferred_element_type=F32)


def _dot_tn(a, b):
    return lax.dot_general(a, b, (((0,), (0,)), ((), ())), preferred_element_type=F32)


def _split2(x):
    hi = x.astype(BF16)
    lo = (x - hi.astype(F32)).astype(BF16)
    return hi, lo


def _split3(x):
    hi = x.astype(BF16)
    r = x - hi.astype(F32)
    mid = r.astype(BF16)
    lo = (r - mid.astype(F32)).astype(BF16)
    return hi, mid, lo


def _sigmoid(x):
    return 1.0 / (1.0 + jnp.exp(-x))


def _silu(x):
    return x * _sigmoid(x)


def _seg_mask(n, seg):
    r = lax.broadcasted_iota(jnp.int32, (n, n), 0) // seg
    c = lax.broadcasted_iota(jnp.int32, (n, n), 1) // seg
    return r == c


def _seg_ones(n, seg):
    return _seg_mask(n, seg).astype(BF16)


def _segsum(x, ones_bf16):
    hi, lo = _split2(x)
    return _dot(hi, ones_bf16) + _dot(lo, ones_bf16)


def _tri_incl(n):
    r = lax.broadcasted_iota(jnp.int32, (n, n), 0)
    c = lax.broadcasted_iota(jnp.int32, (n, n), 1)
    return (c <= r).astype(BF16)


def _cumsum_rows(x, tri_bf16):
    hi, mid, lo = _split3(x)
    return _dot(tri_bf16, hi) + _dot(tri_bf16, mid) + _dot(tri_bf16, lo)


def _params(sem, vmem=None):
    return pltpu.CompilerParams(dimension_semantics=sem, vmem_limit_bytes=vmem or VMEM_LIMIT)


def _proj_kernel(x_ref, g_ref, w_ref, o_ref, h_scr):
    @pl.when(pl.program_id(1) == 0)
    def _():
        x = x_ref[...]
        ms = jnp.mean(x * x, axis=-1, keepdims=True)
        h_scr[...] = (x * lax.rsqrt(ms + EPS) * g_ref[...]).astype(BF16)

    o_ref[...] = _dot(h_scr[...], w_ref[...]).astype(o_ref.dtype)


def _proj(x2d, g, w_bf16):
    m = x2d.shape[0]
    tm = min(2048, m)
    n = w_bf16.shape[1]
    return pl.pallas_call(
        _proj_kernel,
        grid=(m // tm, n // PROJ_TN),
        in_specs=[pl.BlockSpec((tm, D_MODEL), lambda i, j: (i, 0)),
                  pl.BlockSpec((1, D_MODEL), lambda i, j: (0, 0)),
                  pl.BlockSpec((D_MODEL, PROJ_TN), lambda i, j: (0, j))],
        out_specs=pl.BlockSpec((tm, PROJ_TN), lambda i, j: (i, j)),
        out_shape=jax.ShapeDtypeStruct((m, n), BF16),
        scratch_shapes=[pltpu.VMEM((tm, D_MODEL), BF16)],
        compiler_params=_params(("parallel", "arbitrary")),
        name="proj",
    )(x2d, g, w_bf16)


def _qkprep_kernel(*refs):
    q_ref, k_ref, v_ref, qg_ref, kg_ref, c_ref, s1_ref, s2_ref = refs[:8]
    q16t_ref, k32_ref, k16_ref, v32_ref, v16t_ref = refs[-5:]
    ones = _seg_ones(LANES, A_DQK)
    cosv, sin1, sin2 = c_ref[...], s1_ref[...], s2_ref[...]

    def prep(x, gain):
        ss = _segsum(x * x, ones)
        y = x * lax.rsqrt(ss * (1.0 / A_DQK) + EPS) * gain
        return y * cosv + pltpu.roll(y, LANES - A_ROT // 2, 1) * sin1 + pltpu.roll(y, A_ROT // 2, 1) * sin2

    for c in range(GROUP_W // LANES):
        sl = slice(c * LANES, (c + 1) * LANES)
        q = prep(q_ref[:, sl].astype(F32), qg_ref[...])
        q16t_ref[0, sl, :] = (q * (A_DQK ** -0.5 * LOG2_E)).T.astype(BF16)
        k = prep(k_ref[:, sl].astype(F32), kg_ref[...])
        k32_ref[0, :, sl] = k
        k16_ref[:, sl] = k.astype(BF16)
        v = v_ref[:, sl].astype(F32)
        v32_ref[0, :, sl] = v
        v16t_ref[0, sl, :] = v.T.astype(BF16)


def _qkprep(proj, qg128, kg128, cos_t, sin1_t, sin2_t, t_len, layer, depth, kv_stack):
    m = proj.shape[0]
    tm = min(512, t_len)
    nt = t_len // tm
    row = lambda g: pl.BlockSpec((tm, GROUP_W), lambda i, g=g: (i, g))
    tab = pl.BlockSpec((tm, LANES), lambda i: (i % nt, 0))
    vec = pl.BlockSpec((1, LANES), lambda i: (0, 0))
    out = pl.BlockSpec((tm, GROUP_W), lambda i: (i, 0))
    out_t = pl.BlockSpec((1, GROUP_W, tm), lambda i: (i // nt, 0, i % nt))
    transposed = jax.ShapeDtypeStruct((m // t_len, GROUP_W, t_len), BF16)
    stack = jax.ShapeDtypeStruct((depth, m, GROUP_W), F32)
    out_stack = pl.BlockSpec((1, tm, GROUP_W), lambda i: (layer, i, 0))
    in_specs = [row(G_AQ), row(G_AK), row(G_AV), vec, vec, tab, tab, tab]
    args = [proj, proj, proj, qg128, kg128, cos_t, sin1_t, sin2_t]
    aliases = {}
    if kv_stack is not None:
        aliases = {len(args): 1, len(args) + 1: 3}
        in_specs += [pl.BlockSpec(memory_space=pl.ANY)] * 2
        args += list(kv_stack)
    return pl.pallas_call(
        _qkprep_kernel,
        grid=(m // tm,),
        in_specs=in_specs,
        out_specs=[out_t, out_stack, out, out_stack, out_t],
        out_shape=[transposed, stack, jax.ShapeDtypeStruct((m, GROUP_W), BF16), stack, transposed],
        input_output_aliases=aliases,
        compiler_params=_params(("parallel",)),
        name="qkprep",
    )(*args)


def _attn_kernel(sc_ref, qt_ref, k_ref, vt_ref, g_ref, o_ref, m1, l1, a1, m2, l2, a2,
                 *, tq, tk, q_off, nk, out_scale, bounded):
    qi = pl.program_id(2)
    q_first = q_off + qi * tq
    first_chunk_end = (q_first // CHUNK) * CHUNK + CHUNK
    last_vis = ((q_first + tq - 1) // CHUNK) * CHUNK + CHUNK - 1
    n_blocks = jnp.minimum(nk, last_vis // tk + 1)
    n_full = jnp.minimum(n_blocks, first_chunk_end // tk)

    for m, l, a in ((m1, l1, a1), (m2, l2, a2)):
        m[...] = jnp.full(m.shape, NEG_BIG, F32)
        l[...] = jnp.zeros(l.shape, F32)
        a[...] = jnp.zeros(a.shape, F32)

    qt = qt_ref[0]
    dim = lax.broadcasted_iota(jnp.int32, qt.shape, 0)
    zero = jnp.zeros_like(qt)
    q_halves = (jnp.where(dim < A_DQK, qt, zero), jnp.where(dim >= A_DQK, qt, zero))

    def scores(kj, masked):
        if nk == 1:
            k0, k, vt = 0, k_ref[0], vt_ref[0]
        else:
            k0 = pl.multiple_of(kj * tk, tk)
            k = k_ref[0, pl.ds(k0, tk), :]
            vt = vt_ref[0, :, pl.ds(k0, tk)]
        vis = None
        if masked:
            k_chunk = (k0 + lax.broadcasted_iota(jnp.int32, (tk, 1), 0)) // CHUNK
            q_chunk = (q_first + lax.broadcasted_iota(jnp.int32, (1, tq), 1)) // CHUNK
            vis = k_chunk <= q_chunk

        def score(qh):
            s = _dot(k, qh)
            return jnp.where(vis, s, NEG_BIG) if masked else s

        return score, vt

    def key_partial_sums(pr):
        return jnp.sum(pr.reshape(tk // SUBLANES, SUBLANES, tq), axis=0)

    def online_step(kj, masked):
        score, vt = scores(kj, masked)
        for qh, (m, l, a) in zip(q_halves, ((m1, l1, a1), (m2, l2, a2))):
            s = score(qh)
            m_prev = m[...]
            m_new = jnp.maximum(m_prev, jnp.max(s, axis=0, keepdims=True))
            pr = jnp.exp2(s - m_new)
            alpha = jnp.exp2(m_prev - m_new)
            l[...] = alpha * l[...] + key_partial_sums(pr)
            a[...] = alpha * a[...] + _dot(vt, pr.astype(BF16))
            m[...] = m_new

    def bounded_step(kj, masked):
        score, vt = scores(kj, masked)
        for qh, (m, l, a) in zip(q_halves, ((m1, l1, a1), (m2, l2, a2))):
            pr = jnp.exp2(score(qh) - bound)
            l[...] += key_partial_sums(pr)
            a[...] += _dot(vt, pr.astype(BF16))

    def run(step):
        def full_body(kj, carry):
            step(kj, False)
            return carry

        def masked_body(kj, carry):
            step(kj, True)
            return carry

        lax.fori_loop(0, n_full, full_body, 0)
        lax.fori_loop(n_full, n_blocks, masked_body, 0)

    lam = sc_ref[0]
    bound = sc_ref[1]
    if bounded:
        in_range = bound <= ATTN_BOUND_LIMIT

        @pl.when(in_range)
        def _():
            run(bounded_step)

        @pl.when(jnp.logical_not(in_range))
        def _():
            run(online_step)
    else:
        run(online_step)

    l1_tot = jnp.sum(l1[...], axis=0, keepdims=True)
    l2_tot = jnp.sum(l2[...], axis=0, keepdims=True)
    o = (a1[...] / l1_tot - lam * (a2[...] / l2_tot)).T
    ms = jnp.mean(o * o, axis=-1, keepdims=True)
    o_ref[0] = (o * lax.rsqrt(ms + EPS) * g_ref[...] * out_scale).astype(o_ref.dtype)


def _attn(q16t, k16, v16t, scalars, subln_g, q_off, out_scale, bounded):
    b, _, tq_len = q16t.shape
    tk_len = k16.shape[1]
    tq = min(1024, tq_len)
    tk = 512 if tk_len > 2048 else tk_len
    assert tq_len % tq == 0 and tk_len % tk == 0
    nq, nk = tq_len // tq, tk_len // tk
    kern = functools.partial(_attn_kernel, tq=tq, tk=tk, q_off=q_off, nk=nk, out_scale=out_scale,
                             bounded=bounded)
    run_max = pltpu.VMEM((1, tq), F32)
    key_sum = pltpu.VMEM((SUBLANES, tq), F32)
    acc = pltpu.VMEM((LANES, tq), F32)
    return pl.pallas_call(
        kern,
        grid=(b, A_HEADS, nq),
        in_specs=[pl.BlockSpec(memory_space=pltpu.SMEM),
                  pl.BlockSpec((1, LANES, tq), lambda bi, h, qi: (bi, h, qi)),
                  pl.BlockSpec((1, tk_len, LANES), lambda bi, h, qi: (bi, 0, h)),
                  pl.BlockSpec((1, LANES, tk_len), lambda bi, h, qi: (bi, h, 0)),
                  pl.BlockSpec((1, LANES), lambda bi, h, qi: (0, 0))],
        out_specs=pl.BlockSpec((1, tq, LANES), lambda bi, h, qi: (bi, qi, h)),
        out_shape=jax.ShapeDtypeStruct((b, tq_len, GROUP_W), BRANCH_DTYPE),
        scratch_shapes=[run_max, key_sum, acc, run_max, key_sum, acc],
        compiler_params=_params(("parallel", "parallel", "parallel")),
        name="attn",
    )(scalars, q16t, k16, v16t, subln_g)


def _hgrn_kernel(q_ref, f_ref, i_ref, lb_ref, g_ref, s0_ref, o_ref, sfin_ref,
                 st_scr, cum_scr, qk_scr, kin_scr, oi_scr, *, n_sub):
    c = pl.program_id(1)

    @pl.when(c == 0)
    def _():
        st_scr[...] = s0_ref[0]

    sub_rows = [slice(j * CHUNK, (j + 1) * CHUNK) for j in range(n_sub)]
    head_slices = [slice(h * LANES, (h + 1) * LANES) for h in range(B_HEADS)]

    def per_chunk_row(x, r):
        return jnp.concatenate([jnp.broadcast_to(x[rows, :][r:r + 1], (CHUNK, x.shape[1])) for rows in sub_rows],
                               axis=0)

    z = f_ref[...].astype(F32)
    lb = lb_ref[...]
    log_f = jnp.log(lb + (1.0 - lb) * _sigmoid(z))
    k_in = (1.0 - lb) * _sigmoid(-z)
    q = _silu(q_ref[...].astype(F32))
    tri = _tri_incl(CHUNK)
    cum = jnp.concatenate([_cumsum_rows(log_f[rows, :], tri) for rows in sub_rows], axis=0)
    row = lax.broadcasted_iota(jnp.int32, (CHUNK, 1), 0)

    rel = cum - per_chunk_row(cum, CHUNK // 2 - 1)
    safe = jnp.max(jnp.abs(rel)) <= HGRN_SAFE_SPAN

    @pl.when(safe)
    def _():
        qe = (q * jnp.exp(rel)).astype(BF16)
        ke = (k_in * jnp.exp(-rel)).astype(BF16)
        causal = lax.broadcasted_iota(jnp.int32, (CHUNK, CHUNK), 1) <= lax.broadcasted_iota(
            jnp.int32, (CHUNK, CHUNK), 0)
        tiles = [(rows, sl) for rows in sub_rows for sl in head_slices]
        scores = [jnp.where(causal, _dot_nt(qe[rows, sl], ke[rows, sl]), 0.0).astype(BF16) for rows, sl in tiles]
        for (rows, sl), sc in zip(tiles, scores):
            oi_scr[rows, sl] = _dot(sc, i_ref[rows, sl])

    @pl.when(jnp.logical_not(safe))
    def _():
        cum_scr[...] = cum
        qk_scr[...] = q
        kin_scr[...] = k_in

        def chunk_body(j, carry):
            r0 = pl.multiple_of(j * CHUNK, CHUNK)
            for sl in head_slices:
                cum_h = cum_scr[pl.ds(r0, CHUNK), sl]
                kin_h = kin_scr[pl.ds(r0, CHUNK), sl]
                i_h = i_ref[pl.ds(r0, CHUNK), sl].astype(F32)

                def body(g, inner, sl=sl, cum_h=cum_h, kin_h=kin_h, i_h=i_h):
                    g8 = pl.multiple_of(g * SUBLANES, SUBLANES)
                    c_tile = cum_scr[pl.ds(r0 + g8, SUBLANES), sl]
                    q_tile = qk_scr[pl.ds(r0 + g8, SUBLANES), sl]
                    out_rows = []
                    for r in range(SUBLANES):
                        dec = jnp.exp(jnp.minimum(c_tile[r:r + 1] - cum_h, 0.0))
                        col = jnp.sum(dec * (kin_h * q_tile[r:r + 1]), axis=1, keepdims=True)
                        col = jnp.where(row <= g8 + r, col, 0.0)
                        out_rows.append(jnp.sum(col * i_h, axis=0, keepdims=True))
                    oi_scr[pl.ds(r0 + g8, SUBLANES), sl] = jnp.concatenate(out_rows, axis=0)
                    return inner

                lax.fori_loop(0, CHUNK // SUBLANES, body, 0)
            return carry

        lax.fori_loop(0, n_sub, chunk_body, 0)

    cum_last = per_chunk_row(cum, CHUNK - 1)
    q_dec = (q * jnp.exp(cum)).astype(BF16)
    k_tail = (k_in * jnp.exp(cum_last - cum)).astype(BF16)
    decay = jnp.exp(cum_last)
    update = [[_dot_tn(i_ref[rows, sl], k_tail[rows, sl]) for sl in head_slices] for rows in sub_rows]
    states = [st_scr[h] for h in range(B_HEADS)]
    for j, rows in enumerate(sub_rows):
        inter = [_dot_nt(q_dec[rows, sl], st.astype(BF16)) for sl, st in zip(head_slices, states)]
        states = [st * decay[rows, sl][:1] + upd for st, sl, upd in zip(states, head_slices, update[j])]
        for sl, x in zip(head_slices, inter):
            o = oi_scr[rows, sl] + x
            ms = jnp.mean(o * o, axis=-1, keepdims=True)
            o_ref[rows, sl] = (o * lax.rsqrt(ms + EPS) * g_ref[...]).astype(o_ref.dtype)
    for h, st in enumerate(states):
        st_scr[h] = st

    sfin_ref[0] = st_scr[...]


def _hgrn(proj, lb, norm_g128, s0_t, bsz, nc):
    m = proj.shape[0]
    n_sub = _sub_chunks(nc)
    steps = nc // n_sub
    rows = n_sub * CHUNK
    row = lambda g: pl.BlockSpec((rows, GROUP_W), lambda b, c, g=g: (b * steps + c, g))
    st_spec = pl.BlockSpec((1, B_HEADS, LANES, LANES), lambda b, c: (b, 0, 0, 0))
    scratch = pltpu.VMEM((rows, GROUP_W), F32)
    return pl.pallas_call(
        functools.partial(_hgrn_kernel, n_sub=n_sub),
        grid=(bsz, steps),
        in_specs=[row(G_BQ), row(G_BF), row(G_BI),
                  pl.BlockSpec((1, GROUP_W), lambda b, c: (0, 0)),
                  pl.BlockSpec((1, LANES), lambda b, c: (0, 0)),
                  st_spec],
        out_specs=[pl.BlockSpec((rows, GROUP_W), lambda b, c: (b * steps + c, 0)), st_spec],
        out_shape=[jax.ShapeDtypeStruct((m, GROUP_W), BRANCH_DTYPE),
                   jax.ShapeDtypeStruct((bsz, B_HEADS, LANES, LANES), F32)],
        scratch_shapes=[pltpu.VMEM((B_HEADS, LANES, LANES), F32), scratch, scratch, scratch, scratch],
        compiler_params=_params(("parallel", "arbitrary")),
        name="hgrn",
    )(proj, proj, proj, lb, norm_g128, s0_t)


def _rwkprep_kernel(*refs, has_vres):
    (cr_ref, ck_ref, cv_ref, clo_ref, spr_ref, spk_ref, spv_ref, splo_ref,
     mur_ref, muk_ref, muv_ref, mulo_ref, w0_ref, w2h_ref, w2l_ref, a0_ref, a2h_ref, a2l_ref,
     kk_ref, ka_ref, rk_ref) = refs[:21]
    pos = 21
    if has_vres:
        vres_ref, vf_ref, v0_ref, vw2h_ref, vw2l_ref = refs[pos:pos + 5]
        pos += 5
    (at_ref, rt_ref, bh_ref, kh_ref, vc_ref, bonus_ref, gam_ref, nab_ref, arb_ref, g_ref, y0_ref,
     shr_ref, shk_ref, shv_ref, shlo_ref) = refs[pos:pos + 15]
    pr_scr, pk_scr, pv_scr, plo_scr = refs[pos + 15:]
    c = pl.program_id(1)

    @pl.when(c == 0)
    def _():
        pr_scr[...] = spr_ref[0]
        pk_scr[...] = spk_ref[0]
        pv_scr[...] = spv_ref[0]
        plo_scr[...] = splo_ref[0]

    def shifted(x_ref, prev_scr, mu_ref, last_ref):
        x = x_ref[...].astype(F32)
        row = lax.broadcasted_iota(jnp.int32, x.shape, 0)
        prev = jnp.where(row == 0, prev_scr[...], pltpu.roll(x, 1, 0))
        last = x[CHUNK - 1:CHUNK, :]
        prev_scr[...] = last
        last_ref[0] = last
        return x + (prev - x) * mu_ref[...]

    r = shifted(cr_ref, pr_scr, mur_ref, shr_ref)
    k0 = shifted(ck_ref, pk_scr, muk_ref, shk_ref)
    v = shifted(cv_ref, pv_scr, muv_ref, shv_ref)
    lo = shifted(clo_ref, plo_scr, mulo_ref, shlo_ref)

    def lowrank(x, wh_ref, wl_ref):
        xh, xl = _split2(x)
        return _dot(xh, wh_ref[...]) + _dot(xl, wh_ref[...]) + _dot(xh, wl_ref[...])

    w_in = w0_ref[...] + lowrank(jnp.tanh(lo), w2h_ref, w2l_ref)
    nw = -w_in
    softplus = jnp.maximum(nw, 0.0) + jnp.log(1.0 + jnp.exp(-jnp.abs(nw)))
    log_decay = -jnp.exp(-softplus - 0.5)
    a_sig = _sigmoid(a0_ref[...] + lowrank(lo, a2h_ref, a2l_ref))
    if has_vres:
        v_mix = _sigmoid(v0_ref[...] + lowrank(vres_ref[...].astype(F32), vw2h_ref, vw2l_ref))
        v = v + (vf_ref[...] - v) * v_mix
    vc_ref[...] = v

    ones = _seg_ones(LANES, C_DH)
    cum = _cumsum_rows(log_decay, _tri_incl(CHUNK))
    cum_last = cum[CHUNK - 1:CHUNK, :]
    gam_ref[0] = jnp.exp(cum_last)
    e_prev = jnp.exp(cum - log_decay)
    e_cum = jnp.exp(cum)
    e_inv = jnp.exp(-cum)
    e_tail = jnp.exp(cum_last - cum)

    lane = lax.broadcasted_iota(jnp.int32, (CHUNK, LANES), 1)
    par1 = lane >= C_DH
    tcol = lax.broadcasted_iota(jnp.int32, (CHUNK, LANES), 0)
    scol = lane % C_DH
    strict = jnp.concatenate([scol < tcol, scol < tcol], axis=1)
    incl = jnp.concatenate([scol <= tcol, scol <= tcol], axis=1)
    low_mask = jnp.concatenate([strict, incl], axis=0)

    pairs = [slice(p * LANES, (p + 1) * LANES) for p in range(C_WIDTH // LANES)]

    def head_sums(x):
        stacked = _segsum(jnp.concatenate([x[:, sl] for sl in pairs], axis=0), ones)
        return jnp.concatenate([stacked[p * CHUNK:(p + 1) * CHUNK] for p in range(len(pairs))], axis=1)

    kk = k0 * kk_ref[...]
    kk = kk / jnp.maximum(jnp.sqrt(head_sums(kk * kk)), 1e-12)
    k = k0 * (1.0 + (a_sig - 1.0) * ka_ref[...])
    b_vec = kk * a_sig
    bonus_ref[...] = head_sums(r * k * rk_ref[...]) * v
    a_t = -kk * e_prev
    r_t = r * e_cum
    a16, r16 = a_t.astype(BF16), r_t.astype(BF16)
    at_ref[...] = a16
    rt_ref[...] = r16
    bh_ref[...] = (b_vec * e_tail).astype(BF16)
    kh_ref[...] = (k * e_tail).astype(BF16)
    b16, k16, v16 = (b_vec * e_inv).astype(BF16), (k * e_inv).astype(BF16), v.astype(BF16)
    zero = jnp.zeros((CHUNK, LANES), BF16)

    def block_diag(x):
        return [jnp.where(par1, zero, x), jnp.where(par1, x, zero)]

    prods = [jnp.where(low_mask,
                       _dot_nt(jnp.concatenate([a16[:, sl], r16[:, sl]], axis=0),
                               jnp.concatenate(block_diag(b16[:, sl]) + block_diag(k16[:, sl]), axis=0)),
                       0.0) for sl in pairs]
    gys = [_dot(prod[:, LANES:].astype(BF16), jnp.concatenate(block_diag(v16[:, sl]), axis=0))
           for sl, prod in zip(pairs, prods)]
    for sl, prod, gy in zip(pairs, prods, gys):
        nab_ref[:, sl] = prod[:CHUNK, :LANES]
        arb_ref[:, sl] = prod[CHUNK:, :LANES].astype(BF16)
        g_ref[:, sl] = gy[:CHUNK].astype(BF16)
        y0_ref[:, sl] = gy[CHUNK:]


def _rwkprep(proj, shift_parts, params, vres, bsz, nc):
    m = proj.shape[0]
    has_vres = vres is not None
    row = lambda g: pl.BlockSpec((CHUNK, GROUP_W), lambda b, c, g=g: (b * nc + c, g))
    lo_spec = pl.BlockSpec((CHUNK, LANES), lambda b, c: (b * nc + c, COL_LO // LANES))
    st = lambda w: pl.BlockSpec((1, 1, w), lambda b, c: (b, 0, 0))
    vec = lambda w: pl.BlockSpec((1, w), lambda b, c: (0, 0))
    mat = lambda: pl.BlockSpec((LANES, GROUP_W), lambda b, c: (0, 0))
    out = pl.BlockSpec((CHUNK, GROUP_W), lambda b, c: (b * nc + c, 0))
    in_specs = [row(G_CR), row(G_CK), row(G_CV), lo_spec, st(GROUP_W), st(GROUP_W), st(GROUP_W), st(LANES),
                vec(GROUP_W), vec(GROUP_W), vec(GROUP_W), vec(LANES),
                vec(GROUP_W), mat(), mat(), vec(GROUP_W), mat(), mat(),
                vec(GROUP_W), vec(GROUP_W), vec(GROUP_W)]
    args = [proj, proj, proj, proj, *shift_parts,
            params["mu_r"], params["mu_k"], params["mu_v"], params["mu_lo"],
            params["w0"], params["w2h"], params["w2l"], params["a0"], params["a2h"], params["a2l"],
            params["k_k"], params["k_a"], params["r_k"]]
    if has_vres:
        in_specs += [pl.BlockSpec((CHUNK, LANES), lambda b, c: (b * nc + c, COL_VRES // LANES)),
                     out, vec(GROUP_W), mat(), mat()]
        args += [proj, vres["v_first"], vres["v0"], vres["w2h"], vres["w2l"]]
    big = jax.ShapeDtypeStruct((m, GROUP_W), F32)
    half = jax.ShapeDtypeStruct((m, GROUP_W), BF16)
    out_shape = [half] * 4 + [big] * 2 + [jax.ShapeDtypeStruct((bsz * nc, 1, GROUP_W), F32)] + [
        big, half, half, big] + [
        jax.ShapeDtypeStruct((bsz, 1, GROUP_W), F32)] * 3 + [jax.ShapeDtypeStruct((bsz, 1, LANES), F32)]
    out_specs = [out] * 6 + [pl.BlockSpec((1, 1, GROUP_W), lambda b, c: (b * nc + c, 0, 0))] + [out] * 4 + [
        st(GROUP_W)] * 3 + [st(LANES)]
    return pl.pallas_call(
        functools.partial(_rwkprep_kernel, has_vres=has_vres),
        grid=(bsz, nc),
        in_specs=in_specs,
        out_specs=out_specs,
        out_shape=out_shape,
        scratch_shapes=[pltpu.VMEM((1, GROUP_W), F32)] * 3 + [pltpu.VMEM((1, LANES), F32)],
        compiler_params=_params(("parallel", "arbitrary")),
        name="rwkprep",
    )(*args)


def _rwksolve_kernel(n_ref, out_ref, t_ref):
    col = lax.broadcasted_iota(jnp.int32, (CHUNK, LANES), 0)

    def group_body(tg, carry):
        t0 = pl.multiple_of(tg * SUBLANES, SUBLANES)
        for r in range(SUBLANES):
            t = t0 + r

            def s_body(sg, acc, t=t):
                s0 = pl.multiple_of(sg * SUBLANES, SUBLANES)
                coef = n_ref[t, pl.ds(s0, SUBLANES), :]
                for q in range(SUBLANES):
                    acc = acc + coef[q:q + 1] * t_ref[s0 + q]
                return acc

            acc = lax.fori_loop(0, tg, s_body, jnp.where(col == t, 1.0, 0.0).astype(F32))
            coef = n_ref[t, pl.ds(t0, SUBLANES), :]
            for q in range(r):
                acc = acc + coef[q:q + 1] * t_ref[t0 + q]
            t_ref[t] = acc
        return carry

    lax.fori_loop(0, CHUNK // SUBLANES, group_body, 0)
    out_ref[...] = t_ref[...].astype(out_ref.dtype)


def _rwksolve(n_bl):
    n_inst = n_bl.shape[-1]
    spec = pl.BlockSpec((CHUNK, CHUNK, LANES), lambda i: (0, 0, i))
    return pl.pallas_call(
        _rwksolve_kernel,
        grid=(n_inst // LANES,),
        in_specs=[spec],
        out_specs=spec,
        out_shape=jax.ShapeDtypeStruct(n_bl.shape, BF16),
        scratch_shapes=[pltpu.VMEM((CHUNK, CHUNK, LANES), F32)],
        compiler_params=_params(("parallel",)),
        name="rwksolve",
    )(n_bl)


def _rwkseq_kernel(tinv_ref, at_ref, g_ref, rt_ref, arb_ref, y0_ref, v_ref, bh_ref, kh_ref, gam_ref, bonus_ref,
                   lnw_ref, lnb_ref, s0_ref, o_ref, sfin_ref, st_scr, *, n_sub):
    c = pl.program_id(1)

    @pl.when(c == 0)
    def _():
        st_scr[...] = s0_ref[0]

    ones = _seg_ones(LANES, C_DH)
    lane = lax.broadcasted_iota(jnp.int32, (CHUNK, LANES), 1)
    par1 = lane >= C_DH
    bd = _seg_mask(LANES, C_DH)
    pairs = [slice(p * LANES, (p + 1) * LANES) for p in range(C_WIDTH // LANES)]
    zero = jnp.zeros((CHUNK, LANES), BF16)

    def block_diag(x):
        return jnp.concatenate([jnp.where(par1, zero, x), jnp.where(par1, x, zero)], axis=0)

    def apply_inverse(rows):
        return [_dot(tinv_ref[rows, sl],
                     jnp.concatenate([block_diag(at_ref[rows, sl]), block_diag(g_ref[rows, sl])], axis=1))
                for sl in pairs]

    states = [st_scr[p] for p in range(len(pairs))]
    sub_rows = [slice(j * CHUNK, (j + 1) * CHUNK) for j in range(n_sub)]
    wu = apply_inverse(sub_rows[0])
    y_parts = []
    for j, rows in enumerate(sub_rows):
        st16 = [st.astype(BF16) for st in states]
        u = [_dot_nt(x[:, :LANES].astype(BF16), st) + x[:, LANES:] for x, st in zip(wu, st16)]
        y_state = [_dot_nt(rt_ref[rows, sl], st) for sl, st in zip(pairs, st16)]
        if j + 1 < n_sub:
            wu = apply_inverse(sub_rows[j + 1])
        u16 = [x.astype(BF16) for x in u]
        v16 = v_ref[rows, :].astype(BF16)
        upd = [_dot_tn(jnp.concatenate([x, v16[:, sl]], axis=0),
                       jnp.concatenate([bh_ref[rows, sl], kh_ref[rows, sl]], axis=0)) for sl, x in zip(pairs, u16)]
        y_u = [_dot(arb_ref[rows, sl], block_diag(x)) for sl, x in zip(pairs, u16)]
        states = [st * gam_ref[j, :, sl] + jnp.where(bd, x, 0.0) for st, sl, x in zip(states, pairs, upd)]
        y_parts += [a + b + y0_ref[rows, sl] for sl, a, b in zip(pairs, y_state, y_u)]
    for p, st in enumerate(states):
        st_scr[p] = st
    y = jnp.concatenate(y_parts, axis=0)
    d = y - _segsum(y, ones) * (1.0 / C_DH)
    var = _segsum(d * d, ones) * (1.0 / C_DH)
    dn = d * lax.rsqrt(var + C_GN_EPS)
    for j, rows in enumerate(sub_rows):
        for p, sl in enumerate(pairs):
            tile = dn[(j * len(pairs) + p) * CHUNK:(j * len(pairs) + p + 1) * CHUNK]
            o_ref[rows, sl] = (tile * lnw_ref[:, sl] + lnb_ref[:, sl] + bonus_ref[rows, sl]).astype(o_ref.dtype)

    sfin_ref[0] = st_scr[...]


def _sub_chunks(nc):
    return SEQ_SUB_CHUNKS if nc % SEQ_SUB_CHUNKS == 0 else 1


def _rwkseq(tinv, at, g_mat, rt, arb, y0, v, bh, kh, gam, bonus, ln_w, ln_b, s0_bd, bsz, nc):
    m = tinv.shape[0]
    n_sub = _sub_chunks(nc)
    steps = nc // n_sub
    row = pl.BlockSpec((n_sub * CHUNK, GROUP_W), lambda b, c: (b * steps + c, 0))
    vec = pl.BlockSpec((1, GROUP_W), lambda b, c: (0, 0))
    st_spec = pl.BlockSpec((1, C_WIDTH // LANES, LANES, LANES), lambda b, c: (b, 0, 0, 0))
    return pl.pallas_call(
        functools.partial(_rwkseq_kernel, n_sub=n_sub),
        grid=(bsz, steps),
        in_specs=[row] * 9 + [pl.BlockSpec((n_sub, 1, GROUP_W), lambda b, c: (b * steps + c, 0, 0)), row, vec, vec,
                  st_spec],
        out_specs=[row, st_spec],
        out_shape=[jax.ShapeDtypeStruct((m, GROUP_W), BRANCH_DTYPE),
                   jax.ShapeDtypeStruct((bsz, C_WIDTH // LANES, LANES, LANES), F32)],
        scratch_shapes=[pltpu.VMEM((C_WIDTH // LANES, LANES, LANES), F32)],
        compiler_params=_params(("parallel", "arbitrary")),
        name="rwkseq",
    )(tinv, at, g_mat, rt, arb, y0, v, bh, kh, gam, bonus, ln_w, ln_b, s0_bd)


def _merge_kernel(x_ref, oa_ref, ob_ref, oc_ref, ag_ref, bg_ref, cg_ref, ma_ref, mb_ref, mc_ref,
                  wa_ref, wb_ref, wc_ref, wo_ref, y_ref):
    def branch(o_ref, gate_ref, w_ref):
        return _dot((o_ref[...] * _silu(gate_ref[...].astype(F32))).astype(BF16), w_ref[...])

    merged = (_sigmoid(ma_ref[...].astype(F32)) * branch(oa_ref, ag_ref, wa_ref)
              + _sigmoid(mb_ref[...].astype(F32)) * branch(ob_ref, bg_ref, wb_ref)
              + _sigmoid(mc_ref[...].astype(F32)) * branch(oc_ref, cg_ref, wc_ref))
    y_ref[...] = x_ref[...] + _dot(merged.astype(BF16), wo_ref[...])


def _merge(x2d, o_a, o_b, o_c, proj, wa, wb, wc, wo):
    m = x2d.shape[0]
    tm = min(256, m)
    row = pl.BlockSpec((tm, GROUP_W), lambda i: (i, 0))
    grp = lambda g: pl.BlockSpec((tm, GROUP_W), lambda i, g=g: (i, g))
    wsp = pl.BlockSpec((GROUP_W, D_MODEL), lambda i: (0, 0))
    return pl.pallas_call(
        _merge_kernel,
        grid=(m // tm,),
        in_specs=[row, row, row, row, grp(G_AG), grp(G_BG), grp(G_CG), grp(G_MA), grp(G_MB), grp(G_MC),
                  wsp, wsp, wsp, wsp],
        out_specs=row,
        out_shape=jax.ShapeDtypeStruct((m, D_MODEL), F32),
        compiler_params=_params(("parallel",)),
        name="merge",
    )(x2d, o_a, o_b, o_c, proj, proj, proj, proj, proj, proj, wa, wb, wc, wo)


_C_OFF = 8 * GROUP_W
_CP_R = (_C_OFF, _C_OFF + C_WIDTH)
_CP_WLO = (_CP_R[1], _CP_R[1] + C_DECAY_RANK)
_CP_K = (_CP_WLO[1], _CP_WLO[1] + C_WIDTH)
_CP_V = (_CP_K[1], _CP_K[1] + C_WIDTH)
_CP_ALO = (_CP_V[1], _CP_V[1] + C_A_RANK)
_REST = _CP_ALO[1]


def _regroup_w_in(w, vres_w1):
    cols = [w[:, :_C_OFF], w[:, _CP_R[0]:_CP_R[1]], w[:, _CP_K[0]:_CP_K[1]], w[:, _CP_V[0]:_CP_V[1]],
            w[:, _REST:], w[:, _CP_WLO[0]:_CP_WLO[1]], w[:, _CP_ALO[0]:_CP_ALO[1]]]
    vres = jnp.zeros((D_MODEL, LANES), F32)
    if vres_w1 is not None:
        vres = vres.at[:, :C_VRES_RANK].set(vres_w1)
    cols.append(vres)
    cols.append(jnp.zeros((D_MODEL, N_PROJ_PAD - N_PROJ), F32))
    return jnp.concatenate(cols, axis=1).astype(BF16)


def _split_shift(s):
    o = _C_OFF
    part = lambda a: s[:, a[0] - o:a[1] - o]
    lo = jnp.concatenate([part(_CP_WLO), part(_CP_ALO)], axis=1)
    return [x[:, None, :] for x in (part(_CP_R), part(_CP_K), part(_CP_V), lo)]


def _join_shift(r, k, v, lo):
    r, k, v, lo = (x[:, 0, :] for x in (r, k, v, lo))
    return jnp.concatenate([r, lo[:, :C_DECAY_RANK], k, v, lo[:, C_DECAY_RANK:]], axis=1)


def _pad_rows(w, row0):
    out = jnp.zeros((LANES, w.shape[1]), F32).at[row0:row0 + w.shape[0]].set(w)
    hi = out.astype(BF16)
    return hi, (out - hi.astype(F32)).astype(BF16)


def _rope_tables(pos):
    half = A_ROT // 2
    inv_freq = ROPE_THETA ** (-(jnp.arange(half, dtype=F32) * (2.0 / A_ROT)))
    ang = pos.astype(F32)[:, None] * inv_freq[None, :]
    cos, sin = jnp.cos(ang), jnp.sin(ang)
    t = pos.shape[0]
    one = jnp.ones((t, A_DQK - A_ROT), F32)
    zero = jnp.zeros((t, A_DQK - A_ROT), F32)
    z8 = jnp.zeros((t, half), F32)
    c64 = jnp.concatenate([cos, cos, one], axis=1)
    s1 = jnp.concatenate([-sin, z8, zero], axis=1)
    s2 = jnp.concatenate([z8, sin, zero], axis=1)
    tile = lambda x: jnp.concatenate([x, x], axis=1)
    return tile(c64), tile(s1), tile(s2)


def _to_lanes(x, rows):
    return x.reshape(rows, CHUNK, C_HEADS, C_DH).transpose(1, 3, 0, 2).reshape(CHUNK, C_DH, rows * C_HEADS)


def _from_lanes(x, rows):
    return x.reshape(CHUNK, C_DH, rows, C_HEADS).transpose(2, 0, 3, 1).reshape(rows * CHUNK, C_WIDTH)


def _layer(l, x2d, bsz, t_len, q_off, P, lb, past, v_first, kv_stack):
    nc = t_len // CHUNK
    m = bsz * t_len
    w_in = _regroup_w_in(P["w_in"][l], P["c_vres_w1"][l - 1] if l > 0 else None)
    proj = _proj(x2d, P["norm_g"][l][None, :], w_in)

    pos = q_off + jnp.arange(t_len, dtype=jnp.int32)
    cos_t, sin1_t, sin2_t = _rope_tables(pos)
    tile2 = lambda g: jnp.concatenate([g, g])[None, :]
    q16t, k_stack, k16, v_stack, v16t = _qkprep(proj, tile2(P["a_qnorm_g"][l]), tile2(P["a_knorm_g"][l]),
                                                cos_t, sin1_t, sin2_t, t_len, l, P["w_in"].shape[0], kv_stack)
    k16 = k16.reshape(bsz, t_len, GROUP_W)
    if past is not None:
        pk, pv = past[0][l], past[1][l]
        p_len = pk.shape[1]
        k16 = jnp.concatenate([pk.reshape(bsz, p_len, GROUP_W).astype(BF16), k16], axis=1)
        pv_t = jnp.swapaxes(pv.reshape(bsz, p_len, GROUP_W).astype(BF16), 1, 2)
        v16t = jnp.concatenate([pv_t, v16t], axis=2)
    lam_init = 0.8 - 0.6 * math.exp(-0.3 * l)
    lp = P["a_lambda"][l].astype(F32)
    lam = jnp.exp(jnp.sum(lp[0] * lp[1])) - jnp.exp(jnp.sum(lp[2] * lp[3])) + lam_init
    bound = (8.0 * LOG2_E * 1.02) * jnp.max(jnp.abs(P["a_qnorm_g"][l])) * jnp.max(jnp.abs(P["a_knorm_g"][l]))
    o_a = _attn(q16t, k16, v16t, jnp.stack([lam, bound]).astype(F32), P["a_subln_g"][l][None, :], q_off,
                1.0 - lam_init, bounded=past is None)
    o_a = o_a.reshape(m, GROUP_W)

    if past is None:
        s_h = jnp.zeros((bsz, B_HEADS, LANES, LANES), F32)
    else:
        s_h = jnp.swapaxes(past[2][l].astype(F32), -1, -2)
    o_b, s_h_new = _hgrn(proj, lb[l][None, :], P["b_norm_g"][l][None, :], s_h, bsz, nc)
    s_h_new = jnp.swapaxes(s_h_new, -1, -2)

    if past is None:
        shift_prev = jnp.zeros((bsz, 3 * C_WIDTH + C_DECAY_RANK + C_A_RANK), F32)
        s_r = jnp.zeros((bsz, C_HEADS, C_DH, C_DH), F32)
    else:
        shift_prev, s_r = past[4][l], past[3][l].astype(F32)
    mu = _split_shift(P["c_shift_mu"][l][None, :])
    w2h, w2l = _pad_rows(P["c_w2"][l], 0)
    a2h, a2l = _pad_rows(P["c_a2"][l], C_DECAY_RANK)
    cparams = {"mu_r": mu[0][0], "mu_k": mu[1][0], "mu_v": mu[2][0], "mu_lo": mu[3][0],
               "w0": P["c_w0"][l][None, :], "w2h": w2h, "w2l": w2l,
               "a0": P["c_a0"][l][None, :], "a2h": a2h, "a2l": a2l,
               "k_k": P["c_k_k"][l][None, :], "k_a": P["c_k_a"][l][None, :],
               "r_k": P["c_r_k"][l].reshape(1, C_WIDTH)}
    vres = None
    if l > 0:
        vh, vl = _pad_rows(P["c_vres_w2"][l - 1], 0)
        vres = {"v_first": v_first, "v0": P["c_v0"][l - 1][None, :], "w2h": vh, "w2l": vl}
    (a_t, r_t, b_h, k_h, v_c, bonus, gam, nab, arb, g_mat, y0, sh_r, sh_k, sh_v, sh_lo) = _rwkprep(
        proj, _split_shift(shift_prev), cparams, vres, bsz, nc)
    rows = bsz * nc
    n_inst = rows * C_HEADS
    pad = (-n_inst) % LANES
    to_lanes = lambda x: jnp.pad(_to_lanes(x, rows), ((0, 0), (0, 0), (0, pad))) if pad else _to_lanes(x, rows)
    tinv = _from_lanes(_rwksolve(to_lanes(nab))[:, :, :n_inst], rows)
    eye2 = jnp.eye(2, dtype=F32)
    s_bd = jnp.einsum("bpqvk,qr->bpqvrk", s_r.reshape(bsz, C_HEADS // 2, 2, C_DH, C_DH), eye2)
    s_bd = s_bd.reshape(bsz, C_HEADS // 2, LANES, LANES)
    o_c, s_bd_new = _rwkseq(tinv, a_t, g_mat, r_t, arb, y0, v_c, b_h, k_h, gam, bonus,
                            P["c_ln_w"][l][None, :], P["c_ln_b"][l][None, :], s_bd, bsz, nc)
    s_new6 = s_bd_new.reshape(bsz, C_HEADS // 2, 2, C_DH, 2, C_DH)
    s_r_new = jnp.stack([s_new6[:, :, 0, :, 0, :], s_new6[:, :, 1, :, 1, :]], axis=2)
    s_r_new = s_r_new.reshape(bsz, C_HEADS, C_DH, C_DH)
    shift_new = _join_shift(sh_r, sh_k, sh_v, sh_lo)

    bf = lambda w: w.astype(BF16)
    y = _merge(x2d, o_a, o_b, o_c, proj, bf(P["w_out_a"][l]), bf(P["w_out_b"][l]), bf(P["w_out_c"][l]),
               bf(P["w_o"][l]))
    return y, (s_h_new, s_r_new, shift_new), v_c, (k_stack, v_stack)


def _run_trunk(x, q_off, P, lb, past):
    bsz, t_len, _ = x.shape
    depth = P["w_in"].shape[0]
    x2d = x.reshape(bsz * t_len, D_MODEL)
    outs = ([], [], [])
    v_first, kv_stack = None, None
    for l in range(depth):
        x2d, entries, v_c, kv_stack = _layer(l, x2d, bsz, t_len, q_off, P, lb, past, v_first, kv_stack)
        if l == 0:
            v_first = v_c
        for lst, e in zip(outs, entries):
            lst.append(e)
    kv_rows = [s.reshape(depth, bsz, t_len, A_HEADS, 2 * A_DQK) for s in kv_stack]
    return x2d.reshape(bsz, t_len, D_MODEL), kv_rows + [jnp.stack(lst) for lst in outs]


def kernel(x_prompt, x_sample, cache_attn_k, cache_attn_v, state_hgrn, state_rwkv, state_rwkv_shift,
           norm_g, w_in, a_qnorm_g, a_knorm_g, a_lambda, a_subln_g, b_lower, b_norm_g,
           c_shift_mu, c_w0, c_w2, c_a0, c_a2, c_k_k, c_k_a, c_r_k, c_ln_w, c_ln_b,
           c_vres_w1, c_vres_w2, c_v0, w_out_a, w_out_b, w_out_c, w_o):
    P = {"norm_g": norm_g, "w_in": w_in, "a_qnorm_g": a_qnorm_g, "a_knorm_g": a_knorm_g,
         "a_lambda": a_lambda, "a_subln_g": a_subln_g, "b_norm_g": b_norm_g,
         "c_shift_mu": c_shift_mu, "c_w0": c_w0, "c_w2": c_w2, "c_a0": c_a0, "c_a2": c_a2,
         "c_k_k": c_k_k, "c_k_a": c_k_a, "c_r_k": c_r_k, "c_ln_w": c_ln_w, "c_ln_b": c_ln_b,
         "c_vres_w1": c_vres_w1, "c_vres_w2": c_vres_w2, "c_v0": c_v0,
         "w_out_a": w_out_a, "w_out_b": w_out_b, "w_out_c": w_out_c, "w_o": w_o}
    sm = jax.nn.softmax(b_lower.astype(F32), axis=0)
    lb = jnp.cumsum(sm, axis=0) - sm[0:1]
    past_len = cache_attn_k.shape[2]
    y_p, (k_p, v_p, hg_p, rw_p, sh_p) = _run_trunk(x_prompt, 0, P, lb, None)
    y_s, (k_s, v_s, hg_s, rw_s, sh_s) = _run_trunk(
        x_sample, past_len, P, lb, (cache_attn_k, cache_attn_v, state_hgrn, state_rwkv, state_rwkv_shift))
    return (y_p, y_s, k_p, v_p, hg_p, rw_p, sh_p, k_s, v_s, hg_s, rw_s, sh_s)
```

```python
import functools
import math

import jax
import jax.numpy as jnp
from jax import lax
from jax.experimental import pallas as pl
from jax.experimental.pallas import tpu as pltpu

F32 = jnp.float32
BF16 = jnp.bfloat16
BRANCH_DTYPE = BF16

D_MODEL = 1024
CHUNK = 64
EPS = 1e-6
NEG_BIG = -1e30
LOG2_E = 1.4426950408889634
A_HEADS = 8
A_DQK = 64
A_ROT = 16
ROPE_THETA = 500000.0
B_HEADS = 8
C_HEADS = 16
C_DH = 64
C_WIDTH = 1024
C_DECAY_RANK = 64
C_A_RANK = 64
C_VRES_RANK = 32
C_GN_EPS = 64e-5
ATTN_BOUND_LIMIT = 60.0
HGRN_SAFE_SPAN = 60.0
LANES = 128
SUBLANES = 8
SEQ_SUB_CHUNKS = 8
PREP_SUB_CHUNKS = 4
GROUP_W = 1024
N_GROUPS = 15
COL_LO = N_GROUPS * GROUP_W
COL_VRES = COL_LO + LANES
N_PROJ = COL_VRES + LANES
PROJ_TN = 512
N_PROJ_PAD = -(-N_PROJ // PROJ_TN) * PROJ_TN
(G_AQ, G_AK, G_AV, G_AG, G_BQ, G_BF, G_BI, G_BG, G_CR, G_CK, G_CV, G_CG, G_MA, G_MB, G_MC) = range(N_GROUPS)
VMEM_LIMIT = 56 * 1024 * 1024


def _dot(a, b):
    return jnp.dot(a, b, preferred_element_type=F32)


def _dot_nt(a, b):
    return lax.dot_general(a, b, (((1,), (1,)), ((), ())), preferred_element_type=F32)


def _dot_tn(a, b):
    return lax.dot_general(a, b, (((0,), (0,)), ((), ())), preferred_element_type=F32)


def _split2(x):
    hi = x.astype(BF16)
    lo = (x - hi.astype(F32)).astype(BF16)
    return hi, lo


def _split3(x):
    hi = x.astype(BF16)
    r = x - hi.astype(F32)
    mid = r.astype(BF16)
    lo = (r - mid.astype(F32)).astype(BF16)
    return hi, mid, lo


def _sigmoid(x):
    return 1.0 / (1.0 + jnp.exp(-x))


def _silu(x):
    return x * _sigmoid(x)


def _seg_mask(n, seg):
    r = lax.broadcasted_iota(jnp.int32, (n, n), 0) // seg
    c = lax.broadcasted_iota(jnp.int32, (n, n), 1) // seg
    return r == c


def _seg_ones(n, seg):
    return _seg_mask(n, seg).astype(BF16)


def _segsum(x, ones_bf16):
    hi, lo = _split2(x)
    return _dot(hi, ones_bf16) + _dot(lo, ones_bf16)


def _tri_incl(n):
    r = lax.broadcasted_iota(jnp.int32, (n, n), 0)
    c = lax.broadcasted_iota(jnp.int32, (n, n), 1)
    return (c <= r).astype(BF16)


def _cumsum_rows(x, tri_bf16):
    hi, mid, lo = _split3(x)
    return _dot(tri_bf16, hi) + _dot(tri_bf16, mid) + _dot(tri_bf16, lo)


def _params(sem, vmem=None):
    return pltpu.CompilerParams(dimension_semantics=sem, vmem_limit_bytes=vmem or VMEM_LIMIT)


def _proj_kernel(x_ref, g_ref, w_ref, o_ref, h_scr):
    @pl.when(pl.program_id(1) == 0)
    def _():
        x = x_ref[...]
        ms = jnp.mean(x * x, axis=-1, keepdims=True)
        h_scr[...] = (x * lax.rsqrt(ms + EPS) * g_ref[...]).astype(BF16)

    o_ref[...] = _dot(h_scr[...], w_ref[...]).astype(o_ref.dtype)


def _proj(x2d, g, w_bf16):
    m = x2d.shape[0]
    tm = min(2048, m)
    n = w_bf16.shape[1]
    return pl.pallas_call(
        _proj_kernel,
        grid=(m // tm, n // PROJ_TN),
        in_specs=[pl.BlockSpec((tm, D_MODEL), lambda i, j: (i, 0)),
                  pl.BlockSpec((1, D_MODEL), lambda i, j: (0, 0)),
                  pl.BlockSpec((D_MODEL, PROJ_TN), lambda i, j: (0, j))],
        out_specs=pl.BlockSpec((tm, PROJ_TN), lambda i, j: (i, j)),
        out_shape=jax.ShapeDtypeStruct((m, n), BF16),
        scratch_shapes=[pltpu.VMEM((tm, D_MODEL), BF16)],
        compiler_params=_params(("parallel", "arbitrary")),
        name="proj",
    )(x2d, g, w_bf16)


def _qkprep_kernel(*refs):
    q_ref, k_ref, v_ref, qg_ref, kg_ref, c_ref, s1_ref, s2_ref = refs[:8]
    q16t_ref, k32_ref, k16_ref, v32_ref, v16t_ref = refs[-5:]
    ones = _seg_ones(LANES, A_DQK)
    cosv, sin1, sin2 = c_ref[...], s1_ref[...], s2_ref[...]

    def prep(x, gain):
        ss = _segsum(x * x, ones)
        y = x * lax.rsqrt(ss * (1.0 / A_DQK) + EPS) * gain
        return y * cosv + pltpu.roll(y, LANES - A_ROT // 2, 1) * sin1 + pltpu.roll(y, A_ROT // 2, 1) * sin2

    for c in range(GROUP_W // LANES):
        sl = slice(c * LANES, (c + 1) * LANES)
        q = prep(q_ref[:, sl].astype(F32), qg_ref[...])
        q16t_ref[0, sl, :] = (q * (A_DQK ** -0.5 * LOG2_E)).T.astype(BF16)
        k = prep(k_ref[:, sl].astype(F32), kg_ref[...])
        k32_ref[0, :, sl] = k
        k16_ref[:, sl] = k.astype(BF16)
        v = v_ref[:, sl].astype(F32)
        v32_ref[0, :, sl] = v
        v16t_ref[0, sl, :] = v.T.astype(BF16)


def _qkprep(proj, qg128, kg128, cos_t, sin1_t, sin2_t, t_len, layer, depth, kv_stack):
    m = proj.shape[0]
    tm = min(512, t_len)
    nt = t_len // tm
    row = lambda g: pl.BlockSpec((tm, GROUP_W), lambda i, g=g: (i, g))
    tab = pl.BlockSpec((tm, LANES), lambda i: (i % nt, 0))
    vec = pl.BlockSpec((1, LANES), lambda i: (0, 0))
    out = pl.BlockSpec((tm, GROUP_W), lambda i: (i, 0))
    out_t = pl.BlockSpec((1, GROUP_W, tm), lambda i: (i // nt, 0, i % nt))
    transposed = jax.ShapeDtypeStruct((m // t_len, GROUP_W, t_len), BF16)
    stack = jax.ShapeDtypeStruct((depth, m, GROUP_W), F32)
    out_stack = pl.BlockSpec((1, tm, GROUP_W), lambda i: (layer, i, 0))
    in_specs = [row(G_AQ), row(G_AK), row(G_AV), vec, vec, tab, tab, tab]
    args = [proj, proj, proj, qg128, kg128, cos_t, sin1_t, sin2_t]
    aliases = {}
    if kv_stack is not None:
        aliases = {len(args): 1, len(args) + 1: 3}
        in_specs += [pl.BlockSpec(memory_space=pl.ANY)] * 2
        args += list(kv_stack)
    return pl.pallas_call(
        _qkprep_kernel,
        grid=(m // tm,),
        in_specs=in_specs,
        out_specs=[out_t, out_stack, out, out_stack, out_t],
        out_shape=[transposed, stack, jax.ShapeDtypeStruct((m, GROUP_W), BF16), stack, transposed],
        input_output_aliases=aliases,
        compiler_params=_params(("parallel",)),
        name="qkprep",
    )(*args)


def _attn_kernel(sc_ref, qt_ref, k_ref, vt_ref, g_ref, o_ref, m1, l1, a1, m2, l2, a2,
                 *, tq, tk, q_off, nk, out_scale, bounded):
    qi = pl.program_id(2)
    q_first = q_off + qi * tq
    first_chunk_end = (q_first // CHUNK) * CHUNK + CHUNK
    last_vis = ((q_first + tq - 1) // CHUNK) * CHUNK + CHUNK - 1
    n_blocks = jnp.minimum(nk, last_vis // tk + 1)
    n_full = jnp.minimum(n_blocks, first_chunk_end // tk)

    for m, l, a in ((m1, l1, a1), (m2, l2, a2)):
        m[...] = jnp.full(m.shape, NEG_BIG, F32)
        l[...] = jnp.zeros(l.shape, F32)
        a[...] = jnp.zeros(a.shape, F32)

    qt = qt_ref[0]
    dim = lax.broadcasted_iota(jnp.int32, qt.shape, 0)
    zero = jnp.zeros_like(qt)
    q_halves = (jnp.where(dim < A_DQK, qt, zero), jnp.where(dim >= A_DQK, qt, zero))

    def scores(kj, masked, q_lo=0):
        if nk == 1:
            k0, k, vt = 0, k_ref[0], vt_ref[0]
        else:
            k0 = pl.multiple_of(kj * tk, tk)
            k = k_ref[0, pl.ds(k0, tk), :]
            vt = vt_ref[0, :, pl.ds(k0, tk)]
        vis = None
        if masked:
            k_chunk = (k0 + lax.broadcasted_iota(jnp.int32, (tk, 1), 0)) // CHUNK
            q_chunk = (q_first + q_lo + lax.broadcasted_iota(jnp.int32, (1, tq - q_lo), 1)) // CHUNK
            vis = k_chunk <= q_chunk

        def score(qh):
            s = _dot(k, qh[:, q_lo:])
            return jnp.where(vis, s, NEG_BIG) if masked else s

        return score, vt

    def key_partial_sums(pr):
        return jnp.sum(pr.reshape(tk // SUBLANES, SUBLANES, pr.shape[1]), axis=0)

    def online_step(kj, masked):
        score, vt = scores(kj, masked)
        for qh, (m, l, a) in zip(q_halves, ((m1, l1, a1), (m2, l2, a2))):
            s = score(qh)
            m_prev = m[...]
            m_new = jnp.maximum(m_prev, jnp.max(s, axis=0, keepdims=True))
            pr = jnp.exp2(s - m_new)
            alpha = jnp.exp2(m_prev - m_new)
            l[...] = alpha * l[...] + key_partial_sums(pr)
            a[...] = alpha * a[...] + _dot(vt, pr.astype(BF16))
            m[...] = m_new

    def bounded_step(kj, masked, q_lo=0):
        score, vt = scores(kj, masked, q_lo)
        for qh, (m, l, a) in zip(q_halves, ((m1, l1, a1), (m2, l2, a2))):
            pr = jnp.exp2(score(qh) - bound)
            l[:, q_lo:] += key_partial_sums(pr)
            a[:, q_lo:] += _dot(vt, pr.astype(BF16))

    def run(step, split_diagonal=False):
        def full_body(kj, carry):
            step(kj, False)
            return carry

        def masked_body(kj, carry):
            step(kj, True)
            return carry

        lax.fori_loop(0, n_full, full_body, 0)
        if split_diagonal:
            step(n_full, True)
            step(n_full + 1, True, tk)
        else:
            lax.fori_loop(n_full, n_blocks, masked_body, 0)

    lam = sc_ref[0]
    bound = sc_ref[1]
    if bounded:
        in_range = bound <= ATTN_BOUND_LIMIT

        @pl.when(in_range)
        def _():
            run(bounded_step, split_diagonal=(tq == 2 * tk and q_off % tq == 0 and nk > 1))

        @pl.when(jnp.logical_not(in_range))
        def _():
            run(online_step)
    else:
        run(online_step)

    l1_tot = jnp.sum(l1[...], axis=0, keepdims=True)
    l2_tot = jnp.sum(l2[...], axis=0, keepdims=True)
    o = (a1[...] / l1_tot - lam * (a2[...] / l2_tot)).T
    ms = jnp.mean(o * o, axis=-1, keepdims=True)
    o_ref[0] = (o * lax.rsqrt(ms + EPS) * g_ref[...] * out_scale).astype(o_ref.dtype)


def _attn(q16t, k16, v16t, scalars, subln_g, q_off, out_scale, bounded):
    b, _, tq_len = q16t.shape
    tk_len = k16.shape[1]
    tq = min(1024, tq_len)
    tk = 512 if tk_len > 2048 else tk_len
    assert tq_len % tq == 0 and tk_len % tk == 0
    nq, nk = tq_len // tq, tk_len // tk
    kern = functools.partial(_attn_kernel, tq=tq, tk=tk, q_off=q_off, nk=nk, out_scale=out_scale,
                             bounded=bounded)
    run_max = pltpu.VMEM((1, tq), F32)
    key_sum = pltpu.VMEM((SUBLANES, tq), F32)
    acc = pltpu.VMEM((LANES, tq), F32)
    return pl.pallas_call(
        kern,
        grid=(b, A_HEADS, nq),
        in_specs=[pl.BlockSpec(memory_space=pltpu.SMEM),
                  pl.BlockSpec((1, LANES, tq), lambda bi, h, qi: (bi, h, qi)),
                  pl.BlockSpec((1, tk_len, LANES), lambda bi, h, qi: (bi, 0, h)),
                  pl.BlockSpec((1, LANES, tk_len), lambda bi, h, qi: (bi, h, 0)),
                  pl.BlockSpec((1, LANES), lambda bi, h, qi: (0, 0))],
        out_specs=pl.BlockSpec((1, tq, LANES), lambda bi, h, qi: (bi, qi, h)),
        out_shape=jax.ShapeDtypeStruct((b, tq_len, GROUP_W), BRANCH_DTYPE),
        scratch_shapes=[run_max, key_sum, acc, run_max, key_sum, acc],
        compiler_params=_params(("parallel", "parallel", "parallel")),
        name="attn",
    )(scalars, q16t, k16, v16t, subln_g)


def _hgrn_kernel(q_ref, f_ref, i_ref, lb_ref, g_ref, s0_ref, o_ref, sfin_ref,
                 st_scr, cum_scr, qk_scr, kin_scr, oi_scr, *, n_sub):
    c = pl.program_id(1)

    @pl.when(c == 0)
    def _():
        st_scr[...] = s0_ref[0]

    sub_rows = [slice(j * CHUNK, (j + 1) * CHUNK) for j in range(n_sub)]
    head_slices = [slice(h * LANES, (h + 1) * LANES) for h in range(B_HEADS)]

    def per_chunk_row(x, r):
        return jnp.concatenate([jnp.broadcast_to(x[rows, :][r:r + 1], (CHUNK, x.shape[1])) for rows in sub_rows],
                               axis=0)

    z = f_ref[...].astype(F32)
    lb = lb_ref[...]
    log_f = jnp.log(lb + (1.0 - lb) * _sigmoid(z))
    k_in = (1.0 - lb) * _sigmoid(-z)
    q = _silu(q_ref[...].astype(F32))
    tri = _tri_incl(CHUNK)
    cum = jnp.concatenate([_cumsum_rows(log_f[rows, :], tri) for rows in sub_rows], axis=0)
    row = lax.broadcasted_iota(jnp.int32, (CHUNK, 1), 0)

    rel = cum - per_chunk_row(cum, CHUNK // 2 - 1)
    safe = jnp.max(jnp.abs(rel)) <= HGRN_SAFE_SPAN

    @pl.when(safe)
    def _():
        qe = (q * jnp.exp(rel)).astype(BF16)
        ke = (k_in * jnp.exp(-rel)).astype(BF16)
        causal = lax.broadcasted_iota(jnp.int32, (CHUNK, CHUNK), 1) <= lax.broadcasted_iota(
            jnp.int32, (CHUNK, CHUNK), 0)
        tiles = [(rows, sl) for rows in sub_rows for sl in head_slices]
        scores = [jnp.where(causal, _dot_nt(qe[rows, sl], ke[rows, sl]), 0.0).astype(BF16) for rows, sl in tiles]
        for (rows, sl), sc in zip(tiles, scores):
            oi_scr[rows, sl] = _dot(sc, i_ref[rows, sl])

    @pl.when(jnp.logical_not(safe))
    def _():
        cum_scr[...] = cum
        qk_scr[...] = q
        kin_scr[...] = k_in

        def chunk_body(j, carry):
            r0 = pl.multiple_of(j * CHUNK, CHUNK)
            for sl in head_slices:
                cum_h = cum_scr[pl.ds(r0, CHUNK), sl]
                kin_h = kin_scr[pl.ds(r0, CHUNK), sl]
                i_h = i_ref[pl.ds(r0, CHUNK), sl].astype(F32)

                def body(g, inner, sl=sl, cum_h=cum_h, kin_h=kin_h, i_h=i_h):
                    g8 = pl.multiple_of(g * SUBLANES, SUBLANES)
                    c_tile = cum_scr[pl.ds(r0 + g8, SUBLANES), sl]
                    q_tile = qk_scr[pl.ds(r0 + g8, SUBLANES), sl]
                    out_rows = []
                    for r in range(SUBLANES):
                        dec = jnp.exp(jnp.minimum(c_tile[r:r + 1] - cum_h, 0.0))
                        col = jnp.sum(dec * (kin_h * q_tile[r:r + 1]), axis=1, keepdims=True)
                        col = jnp.where(row <= g8 + r, col, 0.0)
                        out_rows.append(jnp.sum(col * i_h, axis=0, keepdims=True))
                    oi_scr[pl.ds(r0 + g8, SUBLANES), sl] = jnp.concatenate(out_rows, axis=0)
                    return inner

                lax.fori_loop(0, CHUNK // SUBLANES, body, 0)
            return carry

        lax.fori_loop(0, n_sub, chunk_body, 0)

    cum_last = per_chunk_row(cum, CHUNK - 1)
    q_dec = (q * jnp.exp(cum)).astype(BF16)
    k_tail = (k_in * jnp.exp(cum_last - cum)).astype(BF16)
    decay = jnp.exp(cum_last)
    update = [[_dot_tn(i_ref[rows, sl], k_tail[rows, sl]) for sl in head_slices] for rows in sub_rows]
    states = [st_scr[h] for h in range(B_HEADS)]
    for j, rows in enumerate(sub_rows):
        inter = [_dot_nt(q_dec[rows, sl], st.astype(BF16)) for sl, st in zip(head_slices, states)]
        states = [st * decay[rows, sl][:1] + upd for st, sl, upd in zip(states, head_slices, update[j])]
        for sl, x in zip(head_slices, inter):
            o = oi_scr[rows, sl] + x
            ms = jnp.mean(o * o, axis=-1, keepdims=True)
            o_ref[rows, sl] = (o * lax.rsqrt(ms + EPS) * g_ref[...]).astype(o_ref.dtype)
    for h, st in enumerate(states):
        st_scr[h] = st

    sfin_ref[0] = st_scr[...]


def _hgrn(proj, lb, norm_g128, s0_t, bsz, nc):
    m = proj.shape[0]
    n_sub = _sub_chunks(nc)
    steps = nc // n_sub
    rows = n_sub * CHUNK
    row = lambda g: pl.BlockSpec((rows, GROUP_W), lambda b, c, g=g: (b * steps + c, g))
    st_spec = pl.BlockSpec((1, B_HEADS, LANES, LANES), lambda b, c: (b, 0, 0, 0))
    scratch = pltpu.VMEM((rows, GROUP_W), F32)
    return pl.pallas_call(
        functools.partial(_hgrn_kernel, n_sub=n_sub),
        grid=(bsz, steps),
        in_specs=[row(G_BQ), row(G_BF), row(G_BI),
                  pl.BlockSpec((1, GROUP_W), lambda b, c: (0, 0)),
                  pl.BlockSpec((1, LANES), lambda b, c: (0, 0)),
                  st_spec],
        out_specs=[pl.BlockSpec((rows, GROUP_W), lambda b, c: (b * steps + c, 0)), st_spec],
        out_shape=[jax.ShapeDtypeStruct((m, GROUP_W), BRANCH_DTYPE),
                   jax.ShapeDtypeStruct((bsz, B_HEADS, LANES, LANES), F32)],
        scratch_shapes=[pltpu.VMEM((B_HEADS, LANES, LANES), F32), scratch, scratch, scratch, scratch],
        compiler_params=_params(("parallel", "arbitrary")),
        name="hgrn",
    )(proj, proj, proj, lb, norm_g128, s0_t)


def _rwkprep_kernel(*refs, has_vres, n_sub):
    (cr_ref, ck_ref, cv_ref, clo_ref, spr_ref, spk_ref, spv_ref, splo_ref,
     mur_ref, muk_ref, muv_ref, mulo_ref, w0_ref, w2h_ref, w2l_ref, a0_ref, a2h_ref, a2l_ref,
     kk_ref, ka_ref, rk_ref) = refs[:21]
    pos = 21
    if has_vres:
        vres_ref, vf_ref, v0_ref, vw2h_ref, vw2l_ref = refs[pos:pos + 5]
        pos += 5
    (at_ref, rt_ref, bh_ref, kh_ref, vc_ref, bonus_ref, gam_ref, nab_ref, arb_ref, g_ref, y0_ref,
     shr_ref, shk_ref, shv_ref, shlo_ref) = refs[pos:pos + 15]
    pr_scr, pk_scr, pv_scr, plo_scr = refs[pos + 15:]
    c = pl.program_id(1)

    @pl.when(c == 0)
    def _():
        pr_scr[...] = spr_ref[0]
        pk_scr[...] = spk_ref[0]
        pv_scr[...] = spv_ref[0]
        plo_scr[...] = splo_ref[0]

    def shifted(x_ref, prev_scr, mu_ref, last_ref):
        x = x_ref[...].astype(F32)
        row = lax.broadcasted_iota(jnp.int32, x.shape, 0)
        prev = jnp.where(row == 0, prev_scr[...], pltpu.roll(x, 1, 0))
        last = x[x.shape[0] - 1:, :]
        prev_scr[...] = last
        last_ref[0] = last
        return x + (prev - x) * mu_ref[...]

    r = shifted(cr_ref, pr_scr, mur_ref, shr_ref)
    k0 = shifted(ck_ref, pk_scr, muk_ref, shk_ref)
    v = shifted(cv_ref, pv_scr, muv_ref, shv_ref)
    lo = shifted(clo_ref, plo_scr, mulo_ref, shlo_ref)

    def lowrank(x, wh_ref, wl_ref):
        xh, xl = _split2(x)
        return _dot(xh, wh_ref[...]) + _dot(xl, wh_ref[...]) + _dot(xh, wl_ref[...])

    w_in = w0_ref[...] + lowrank(jnp.tanh(lo), w2h_ref, w2l_ref)
    nw = -w_in
    softplus = jnp.maximum(nw, 0.0) + jnp.log(1.0 + jnp.exp(-jnp.abs(nw)))
    log_decay = -jnp.exp(-softplus - 0.5)
    a_sig = _sigmoid(a0_ref[...] + lowrank(lo, a2h_ref, a2l_ref))
    if has_vres:
        v_mix = _sigmoid(v0_ref[...] + lowrank(vres_ref[...].astype(F32), vw2h_ref, vw2l_ref))
        v = v + (vf_ref[...] - v) * v_mix
    vc_ref[...] = v

    ones = _seg_ones(LANES, C_DH)
    sub_rows = [slice(j * CHUNK, (j + 1) * CHUNK) for j in range(n_sub)]
    tri = _tri_incl(CHUNK)
    cum = jnp.concatenate([_cumsum_rows(log_decay[rows, :], tri) for rows in sub_rows], axis=0)
    for j, rows in enumerate(sub_rows):
        gam_ref[j] = jnp.exp(cum[rows, :][CHUNK - 1:CHUNK])
    cum_last = jnp.concatenate([jnp.broadcast_to(cum[rows, :][CHUNK - 1:CHUNK], (CHUNK, C_WIDTH))
                                for rows in sub_rows], axis=0)
    e_prev = jnp.exp(cum - log_decay)
    e_cum = jnp.exp(cum)
    e_inv = jnp.exp(-cum)
    e_tail = jnp.exp(cum_last - cum)

    lane = lax.broadcasted_iota(jnp.int32, (CHUNK, LANES), 1)
    par1 = lane >= C_DH
    tcol = lax.broadcasted_iota(jnp.int32, (CHUNK, LANES), 0)
    scol = lane % C_DH
    strict = jnp.concatenate([scol < tcol, scol < tcol], axis=1)
    incl = jnp.concatenate([scol <= tcol, scol <= tcol], axis=1)
    low_mask = jnp.concatenate([strict, incl], axis=0)

    pairs = [slice(p * LANES, (p + 1) * LANES) for p in range(C_WIDTH // LANES)]

    def head_sums(x):
        n = x.shape[0]
        stacked = _segsum(jnp.concatenate([x[:, sl] for sl in pairs], axis=0), ones)
        return jnp.concatenate([stacked[p * n:(p + 1) * n] for p in range(len(pairs))], axis=1)

    kk = k0 * kk_ref[...]
    kk = kk / jnp.maximum(jnp.sqrt(head_sums(kk * kk)), 1e-12)
    k = k0 * (1.0 + (a_sig - 1.0) * ka_ref[...])
    b_vec = kk * a_sig
    bonus_ref[...] = head_sums(r * k * rk_ref[...]) * v
    a_t = -kk * e_prev
    r_t = r * e_cum
    a16, r16 = a_t.astype(BF16), r_t.astype(BF16)
    at_ref[...] = a16
    rt_ref[...] = r16
    bh_ref[...] = (b_vec * e_tail).astype(BF16)
    kh_ref[...] = (k * e_tail).astype(BF16)
    b16, k16, v16 = (b_vec * e_inv).astype(BF16), (k * e_inv).astype(BF16), v.astype(BF16)
    zero = jnp.zeros((CHUNK, LANES), BF16)

    def block_diag(x):
        return [jnp.where(par1, zero, x), jnp.where(par1, x, zero)]

    tiles = [(rows, sl) for rows in sub_rows for sl in pairs]
    prods = [jnp.where(low_mask,
                       _dot_nt(jnp.concatenate([a16[rows, sl], r16[rows, sl]], axis=0),
                               jnp.concatenate(block_diag(b16[rows, sl]) + block_diag(k16[rows, sl]), axis=0)),
                       0.0) for rows, sl in tiles]
    gys = [_dot(prod[:, LANES:].astype(BF16), jnp.concatenate(block_diag(v16[rows, sl]), axis=0))
           for (rows, sl), prod in zip(tiles, prods)]
    for (rows, sl), prod, gy in zip(tiles, prods, gys):
        nab_ref[rows, sl] = prod[:CHUNK, :LANES]
        arb_ref[rows, sl] = prod[CHUNK:, :LANES].astype(BF16)
        g_ref[rows, sl] = gy[:CHUNK].astype(BF16)
        y0_ref[rows, sl] = gy[CHUNK:]


def _rwkprep(proj, shift_parts, params, vres, bsz, nc):
    m = proj.shape[0]
    has_vres = vres is not None
    n_sub = PREP_SUB_CHUNKS if nc % PREP_SUB_CHUNKS == 0 else 1
    steps = nc // n_sub
    rows = n_sub * CHUNK
    row = lambda g: pl.BlockSpec((rows, GROUP_W), lambda b, c, g=g: (b * steps + c, g))
    lo_spec = pl.BlockSpec((rows, LANES), lambda b, c: (b * steps + c, COL_LO // LANES))
    st = lambda w: pl.BlockSpec((1, 1, w), lambda b, c: (b, 0, 0))
    vec = lambda w: pl.BlockSpec((1, w), lambda b, c: (0, 0))
    mat = lambda: pl.BlockSpec((LANES, GROUP_W), lambda b, c: (0, 0))
    out = pl.BlockSpec((rows, GROUP_W), lambda b, c: (b * steps + c, 0))
    in_specs = [row(G_CR), row(G_CK), row(G_CV), lo_spec, st(GROUP_W), st(GROUP_W), st(GROUP_W), st(LANES),
                vec(GROUP_W), vec(GROUP_W), vec(GROUP_W), vec(LANES),
                vec(GROUP_W), mat(), mat(), vec(GROUP_W), mat(), mat(),
                vec(GROUP_W), vec(GROUP_W), vec(GROUP_W)]
    args = [proj, proj, proj, proj, *shift_parts,
            params["mu_r"], params["mu_k"], params["mu_v"], params["mu_lo"],
            params["w0"], params["w2h"], params["w2l"], params["a0"], params["a2h"], params["a2l"],
            params["k_k"], params["k_a"], params["r_k"]]
    if has_vres:
        in_specs += [pl.BlockSpec((rows, LANES), lambda b, c: (b * steps + c, COL_VRES // LANES)),
                     out, vec(GROUP_W), mat(), mat()]
        args += [proj, vres["v_first"], vres["v0"], vres["w2h"], vres["w2l"]]
    big = jax.ShapeDtypeStruct((m, GROUP_W), F32)
    half = jax.ShapeDtypeStruct((m, GROUP_W), BF16)
    out_shape = [half] * 4 + [big] * 2 + [jax.ShapeDtypeStruct((bsz * nc, 1, GROUP_W), F32)] + [
        big, half, half, big] + [
        jax.ShapeDtypeStruct((bsz, 1, GROUP_W), F32)] * 3 + [jax.ShapeDtypeStruct((bsz, 1, LANES), F32)]
    out_specs = [out] * 6 + [pl.BlockSpec((n_sub, 1, GROUP_W), lambda b, c: (b * steps + c, 0, 0))] + [out] * 4 + [
        st(GROUP_W)] * 3 + [st(LANES)]
    return pl.pallas_call(
        functools.partial(_rwkprep_kernel, has_vres=has_vres, n_sub=n_sub),
        grid=(bsz, steps),
        in_specs=in_specs,
        out_specs=out_specs,
        out_shape=out_shape,
        scratch_shapes=[pltpu.VMEM((1, GROUP_W), F32)] * 3 + [pltpu.VMEM((1, LANES), F32)],
        compiler_params=_params(("parallel", "arbitrary")),
        name="rwkprep",
    )(*args)


def _rwksolve_kernel(n_ref, out_ref, t_ref):
    col = lax.broadcasted_iota(jnp.int32, (CHUNK, LANES), 0)

    def group_body(tg, carry):
        t0 = pl.multiple_of(tg * SUBLANES, SUBLANES)
        for r in range(SUBLANES):
            t = t0 + r

            def s_body(sg, acc, t=t):
                s0 = pl.multiple_of(sg * SUBLANES, SUBLANES)
                coef = n_ref[t, pl.ds(s0, SUBLANES), :]
                for q in range(SUBLANES):
                    acc = acc + coef[q:q + 1] * t_ref[s0 + q]
                return acc

            acc = lax.fori_loop(0, tg, s_body, jnp.where(col == t, 1.0, 0.0).astype(F32))
            coef = n_ref[t, pl.ds(t0, SUBLANES), :]
            for q in range(r):
                acc = acc + coef[q:q + 1] * t_ref[t0 + q]
            t_ref[t] = acc
        return carry

    lax.fori_loop(0, CHUNK // SUBLANES, group_body, 0)
    out_ref[...] = t_ref[...].astype(out_ref.dtype)


def _rwksolve(n_bl):
    n_inst = n_bl.shape[-1]
    spec = pl.BlockSpec((CHUNK, CHUNK, LANES), lambda i: (0, 0, i))
    return pl.pallas_call(
        _rwksolve_kernel,
        grid=(n_inst // LANES,),
        in_specs=[spec],
        out_specs=spec,
        out_shape=jax.ShapeDtypeStruct(n_bl.shape, BF16),
        scratch_shapes=[pltpu.VMEM((CHUNK, CHUNK, LANES), F32)],
        compiler_params=_params(("parallel",)),
        name="rwksolve",
    )(n_bl)


def _rwkseq_kernel(tinv_ref, at_ref, g_ref, rt_ref, arb_ref, y0_ref, v_ref, bh_ref, kh_ref, gam_ref, bonus_ref,
                   lnw_ref, lnb_ref, s0_ref, o_ref, sfin_ref, st_scr, *, n_sub):
    c = pl.program_id(1)

    @pl.when(c == 0)
    def _():
        st_scr[...] = s0_ref[0]

    ones = _seg_ones(LANES, C_DH)
    lane = lax.broadcasted_iota(jnp.int32, (CHUNK, LANES), 1)
    par1 = lane >= C_DH
    bd = _seg_mask(LANES, C_DH)
    pairs = [slice(p * LANES, (p + 1) * LANES) for p in range(C_WIDTH // LANES)]
    zero = jnp.zeros((CHUNK, LANES), BF16)

    def block_diag(x):
        return jnp.concatenate([jnp.where(par1, zero, x), jnp.where(par1, x, zero)], axis=0)

    def apply_inverse(rows):
        return [_dot(tinv_ref[rows, sl],
                     jnp.concatenate([block_diag(at_ref[rows, sl]), block_diag(g_ref[rows, sl])], axis=1))
                for sl in pairs]

    states = [st_scr[p] for p in range(len(pairs))]
    sub_rows = [slice(j * CHUNK, (j + 1) * CHUNK) for j in range(n_sub)]
    wu = apply_inverse(sub_rows[0])
    y_parts = []
    for j, rows in enumerate(sub_rows):
        st16 = [st.astype(BF16) for st in states]
        u = [_dot_nt(x[:, :LANES].astype(BF16), st) + x[:, LANES:] for x, st in zip(wu, st16)]
        y_state = [_dot_nt(rt_ref[rows, sl], st) for sl, st in zip(pairs, st16)]
        if j + 1 < n_sub:
            wu = apply_inverse(sub_rows[j + 1])
        u16 = [x.astype(BF16) for x in u]
        v16 = v_ref[rows, :].astype(BF16)
        upd = [_dot_tn(jnp.concatenate([x, v16[:, sl]], axis=0),
                       jnp.concatenate([bh_ref[rows, sl], kh_ref[rows, sl]], axis=0)) for sl, x in zip(pairs, u16)]
        y_u = [_dot(arb_ref[rows, sl], block_diag(x)) for sl, x in zip(pairs, u16)]
        states = [st * gam_ref[j, :, sl] + jnp.where(bd, x, 0.0) for st, sl, x in zip(states, pairs, upd)]
        y_parts += [a + b + y0_ref[rows, sl] for sl, a, b in zip(pairs, y_state, y_u)]
    for p, st in enumerate(states):
        st_scr[p] = st
    y = jnp.concatenate(y_parts, axis=0)
    d = y - _segsum(y, ones) * (1.0 / C_DH)
    var = _segsum(d * d, ones) * (1.0 / C_DH)
    dn = d * lax.rsqrt(var + C_GN_EPS)
    for j, rows in enumerate(sub_rows):
        for p, sl in enumerate(pairs):
            tile = dn[(j * len(pairs) + p) * CHUNK:(j * len(pairs) + p + 1) * CHUNK]
            o_ref[rows, sl] = (tile * lnw_ref[:, sl] + lnb_ref[:, sl] + bonus_ref[rows, sl]).astype(o_ref.dtype)

    sfin_ref[0] = st_scr[...]


def _sub_chunks(nc):
    return SEQ_SUB_CHUNKS if nc % SEQ_SUB_CHUNKS == 0 else 1


def _rwkseq(tinv, at, g_mat, rt, arb, y0, v, bh, kh, gam, bonus, ln_w, ln_b, s0_bd, bsz, nc):
    m = tinv.shape[0]
    n_sub = _sub_chunks(nc)
    steps = nc // n_sub
    row = pl.BlockSpec((n_sub * CHUNK, GROUP_W), lambda b, c: (b * steps + c, 0))
    vec = pl.BlockSpec((1, GROUP_W), lambda b, c: (0, 0))
    st_spec = pl.BlockSpec((1, C_WIDTH // LANES, LANES, LANES), lambda b, c: (b, 0, 0, 0))
    return pl.pallas_call(
        functools.partial(_rwkseq_kernel, n_sub=n_sub),
        grid=(bsz, steps),
        in_specs=[row] * 9 + [pl.BlockSpec((n_sub, 1, GROUP_W), lambda b, c: (b * steps + c, 0, 0)), row, vec, vec,
                  st_spec],
        out_specs=[row, st_spec],
        out_shape=[jax.ShapeDtypeStruct((m, GROUP_W), BRANCH_DTYPE),
                   jax.ShapeDtypeStruct((bsz, C_WIDTH // LANES, LANES, LANES), F32)],
        scratch_shapes=[pltpu.VMEM((C_WIDTH // LANES, LANES, LANES), F32)],
        compiler_params=_params(("parallel", "arbitrary")),
        name="rwkseq",
    )(tinv, at, g_mat, rt, arb, y0, v, bh, kh, gam, bonus, ln_w, ln_b, s0_bd)


def _merge_kernel(x_ref, oa_ref, ob_ref, oc_ref, ag_ref, bg_ref, cg_ref, ma_ref, mb_ref, mc_ref,
                  wa_ref, wb_ref, wc_ref, wo_ref, y_ref):
    def branch(o_ref, gate_ref, w_ref):
        return _dot((o_ref[...] * _silu(gate_ref[...].astype(F32))).astype(BF16), w_ref[...])

    merged = (_sigmoid(ma_ref[...].astype(F32)) * branch(oa_ref, ag_ref, wa_ref)
              + _sigmoid(mb_ref[...].astype(F32)) * branch(ob_ref, bg_ref, wb_ref)
              + _sigmoid(mc_ref[...].astype(F32)) * branch(oc_ref, cg_ref, wc_ref))
    y_ref[...] = x_ref[...] + _dot(merged.astype(BF16), wo_ref[...])


def _merge(x2d, o_a, o_b, o_c, proj, wa, wb, wc, wo):
    m = x2d.shape[0]
    tm = min(256, m)
    row = pl.BlockSpec((tm, GROUP_W), lambda i: (i, 0))
    grp = lambda g: pl.BlockSpec((tm, GROUP_W), lambda i, g=g: (i, g))
    wsp = pl.BlockSpec((GROUP_W, D_MODEL), lambda i: (0, 0))
    return pl.pallas_call(
        _merge_kernel,
        grid=(m // tm,),
        in_specs=[row, row, row, row, grp(G_AG), grp(G_BG), grp(G_CG), grp(G_MA), grp(G_MB), grp(G_MC),
                  wsp, wsp, wsp, wsp],
        out_specs=row,
        out_shape=jax.ShapeDtypeStruct((m, D_MODEL), F32),
        compiler_params=_params(("parallel",)),
        name="merge",
    )(x2d, o_a, o_b, o_c, proj, proj, proj, proj, proj, proj, wa, wb, wc, wo)


_C_OFF = 8 * GROUP_W
_CP_R = (_C_OFF, _C_OFF + C_WIDTH)
_CP_WLO = (_CP_R[1], _CP_R[1] + C_DECAY_RANK)
_CP_K = (_CP_WLO[1], _CP_WLO[1] + C_WIDTH)
_CP_V = (_CP_K[1], _CP_K[1] + C_WIDTH)
_CP_ALO = (_CP_V[1], _CP_V[1] + C_A_RANK)
_REST = _CP_ALO[1]


def _regroup_w_in(w, vres_w1):
    cols = [w[:, :_C_OFF], w[:, _CP_R[0]:_CP_R[1]], w[:, _CP_K[0]:_CP_K[1]], w[:, _CP_V[0]:_CP_V[1]],
            w[:, _REST:], w[:, _CP_WLO[0]:_CP_WLO[1]], w[:, _CP_ALO[0]:_CP_ALO[1]]]
    vres = jnp.zeros((D_MODEL, LANES), F32)
    if vres_w1 is not None:
        vres = vres.at[:, :C_VRES_RANK].set(vres_w1)
    cols.append(vres)
    cols.append(jnp.zeros((D_MODEL, N_PROJ_PAD - N_PROJ), F32))
    return jnp.concatenate(cols, axis=1).astype(BF16)


def _split_shift(s):
    o = _C_OFF
    part = lambda a: s[:, a[0] - o:a[1] - o]
    lo = jnp.concatenate([part(_CP_WLO), part(_CP_ALO)], axis=1)
    return [x[:, None, :] for x in (part(_CP_R), part(_CP_K), part(_CP_V), lo)]


def _join_shift(r, k, v, lo):
    r, k, v, lo = (x[:, 0, :] for x in (r, k, v, lo))
    return jnp.concatenate([r, lo[:, :C_DECAY_RANK], k, v, lo[:, C_DECAY_RANK:]], axis=1)


def _pad_rows(w, row0):
    out = jnp.zeros((LANES, w.shape[1]), F32).at[row0:row0 + w.shape[0]].set(w)
    hi = out.astype(BF16)
    return hi, (out - hi.astype(F32)).astype(BF16)


def _rope_tables(pos):
    half = A_ROT // 2
    inv_freq = ROPE_THETA ** (-(jnp.arange(half, dtype=F32) * (2.0 / A_ROT)))
    ang = pos.astype(F32)[:, None] * inv_freq[None, :]
    cos, sin = jnp.cos(ang), jnp.sin(ang)
    t = pos.shape[0]
    one = jnp.ones((t, A_DQK - A_ROT), F32)
    zero = jnp.zeros((t, A_DQK - A_ROT), F32)
    z8 = jnp.zeros((t, half), F32)
    c64 = jnp.concatenate([cos, cos, one], axis=1)
    s1 = jnp.concatenate([-sin, z8, zero], axis=1)
    s2 = jnp.concatenate([z8, sin, zero], axis=1)
    tile = lambda x: jnp.concatenate([x, x], axis=1)
    return tile(c64), tile(s1), tile(s2)


def _to_lanes(x, rows):
    return x.reshape(rows, CHUNK, C_HEADS, C_DH).transpose(1, 3, 0, 2).reshape(CHUNK, C_DH, rows * C_HEADS)


def _from_lanes(x, rows):
    return x.reshape(CHUNK, C_DH, rows, C_HEADS).transpose(2, 0, 3, 1).reshape(rows * CHUNK, C_WIDTH)


def _layer(l, x2d, bsz, t_len, q_off, P, lb, past, v_first, kv_stack):
    nc = t_len // CHUNK
    m = bsz * t_len
    w_in = _regroup_w_in(P["w_in"][l], P["c_vres_w1"][l - 1] if l > 0 else None)
    proj = _proj(x2d, P["norm_g"][l][None, :], w_in)

    pos = q_off + jnp.arange(t_len, dtype=jnp.int32)
    cos_t, sin1_t, sin2_t = _rope_tables(pos)
    tile2 = lambda g: jnp.concatenate([g, g])[None, :]
    q16t, k_stack, k16, v_stack, v16t = _qkprep(proj, tile2(P["a_qnorm_g"][l]), tile2(P["a_knorm_g"][l]),
                                                cos_t, sin1_t, sin2_t, t_len, l, P["w_in"].shape[0], kv_stack)
    k16 = k16.reshape(bsz, t_len, GROUP_W)
    if past is not None:
        pk, pv = past[0][l], past[1][l]
        p_len = pk.shape[1]
        k16 = jnp.concatenate([pk.reshape(bsz, p_len, GROUP_W).astype(BF16), k16], axis=1)
        pv_t = jnp.swapaxes(pv.reshape(bsz, p_len, GROUP_W).astype(BF16), 1, 2)
        v16t = jnp.concatenate([pv_t, v16t], axis=2)
    lam_init = 0.8 - 0.6 * math.exp(-0.3 * l)
    lp = P["a_lambda"][l].astype(F32)
    lam = jnp.exp(jnp.sum(lp[0] * lp[1])) - jnp.exp(jnp.sum(lp[2] * lp[3])) + lam_init
    bound = (8.0 * LOG2_E * 1.02) * jnp.max(jnp.abs(P["a_qnorm_g"][l])) * jnp.max(jnp.abs(P["a_knorm_g"][l]))
    o_a = _attn(q16t, k16, v16t, jnp.stack([lam, bound]).astype(F32), P["a_subln_g"][l][None, :], q_off,
                1.0 - lam_init, bounded=past is None)
    o_a = o_a.reshape(m, GROUP_W)

    if past is None:
        s_h = jnp.zeros((bsz, B_HEADS, LANES, LANES), F32)
    else:
        s_h = jnp.swapaxes(past[2][l].astype(F32), -1, -2)
    o_b, s_h_new = _hgrn(proj, lb[l][None, :], P["b_norm_g"][l][None, :], s_h, bsz, nc)
    s_h_new = jnp.swapaxes(s_h_new, -1, -2)

    if past is None:
        shift_prev = jnp.zeros((bsz, 3 * C_WIDTH + C_DECAY_RANK + C_A_RANK), F32)
        s_r = jnp.zeros((bsz, C_HEADS, C_DH, C_DH), F32)
    else:
        shift_prev, s_r = past[4][l], past[3][l].astype(F32)
    mu = _split_shift(P["c_shift_mu"][l][None, :])
    w2h, w2l = _pad_rows(P["c_w2"][l], 0)
    a2h, a2l = _pad_rows(P["c_a2"][l], C_DECAY_RANK)
    cparams = {"mu_r": mu[0][0], "mu_k": mu[1][0], "mu_v": mu[2][0], "mu_lo": mu[3][0],
               "w0": P["c_w0"][l][None, :], "w2h": w2h, "w2l": w2l,
               "a0": P["c_a0"][l][None, :], "a2h": a2h, "a2l": a2l,
               "k_k": P["c_k_k"][l][None, :], "k_a": P["c_k_a"][l][None, :],
               "r_k": P["c_r_k"][l].reshape(1, C_WIDTH)}
    vres = None
    if l > 0:
        vh, vl = _pad_rows(P["c_vres_w2"][l - 1], 0)
        vres = {"v_first": v_first, "v0": P["c_v0"][l - 1][None, :], "w2h": vh, "w2l": vl}
    (a_t, r_t, b_h, k_h, v_c, bonus, gam, nab, arb, g_mat, y0, sh_r, sh_k, sh_v, sh_lo) = _rwkprep(
        proj, _split_shift(shift_prev), cparams, vres, bsz, nc)
    rows = bsz * nc
    n_inst = rows * C_HEADS
    pad = (-n_inst) % LANES
    to_lanes = lambda x: jnp.pad(_to_lanes(x, rows), ((0, 0), (0, 0), (0, pad))) if pad else _to_lanes(x, rows)
    tinv = _from_lanes(_rwksolve(to_lanes(nab))[:, :, :n_inst], rows)
    eye2 = jnp.eye(2, dtype=F32)
    s_bd = jnp.einsum("bpqvk,qr->bpqvrk", s_r.reshape(bsz, C_HEADS // 2, 2, C_DH, C_DH), eye2)
    s_bd = s_bd.reshape(bsz, C_HEADS // 2, LANES, LANES)
    o_c, s_bd_new = _rwkseq(tinv, a_t, g_mat, r_t, arb, y0, v_c, b_h, k_h, gam, bonus,
                            P["c_ln_w"][l][None, :], P["c_ln_b"][l][None, :], s_bd, bsz, nc)
    s_new6 = s_bd_new.reshape(bsz, C_HEADS // 2, 2, C_DH, 2, C_DH)
    s_r_new = jnp.stack([s_new6[:, :, 0, :, 0, :], s_new6[:, :, 1, :, 1, :]], axis=2)
    s_r_new = s_r_new.reshape(bsz, C_HEADS, C_DH, C_DH)
    shift_new = _join_shift(sh_r, sh_k, sh_v, sh_lo)

    bf = lambda w: w.astype(BF16)
    y = _merge(x2d, o_a, o_b, o_c, proj, bf(P["w_out_a"][l]), bf(P["w_out_b"][l]), bf(P["w_out_c"][l]),
               bf(P["w_o"][l]))
    return y, (s_h_new, s_r_new, shift_new), v_c, (k_stack, v_stack)


def _run_trunk(x, q_off, P, lb, past):
    bsz, t_len, _ = x.shape
    depth = P["w_in"].shape[0]
    x2d = x.reshape(bsz * t_len, D_MODEL)
    outs = ([], [], [])
    v_first, kv_stack = None, None
    for l in range(depth):
        x2d, entries, v_c, kv_stack = _layer(l, x2d, bsz, t_len, q_off, P, lb, past, v_first, kv_stack)
        if l == 0:
            v_first = v_c
        for lst, e in zip(outs, entries):
            lst.append(e)
    kv_rows = [s.reshape(depth, bsz, t_len, A_HEADS, 2 * A_DQK) for s in kv_stack]
    return x2d.reshape(bsz, t_len, D_MODEL), kv_rows + [jnp.stack(lst) for lst in outs]


def kernel(x_prompt, x_sample, cache_attn_k, cache_attn_v, state_hgrn, state_rwkv, state_rwkv_shift,
           norm_g, w_in, a_qnorm_g, a_knorm_g, a_lambda, a_subln_g, b_lower, b_norm_g,
           c_shift_mu, c_w0, c_w2, c_a0, c_a2, c_k_k, c_k_a, c_r_k, c_ln_w, c_ln_b,
           c_vres_w1, c_vres_w2, c_v0, w_out_a, w_out_b, w_out_c, w_o):
    P = {"norm_g": norm_g, "w_in": w_in, "a_qnorm_g": a_qnorm_g, "a_knorm_g": a_knorm_g,
         "a_lambda": a_lambda, "a_subln_g": a_subln_g, "b_norm_g": b_norm_g,
         "c_shift_mu": c_shift_mu, "c_w0": c_w0, "c_w2": c_w2, "c_a0": c_a0, "c_a2": c_a2,
         "c_k_k": c_k_k, "c_k_a": c_k_a, "c_r_k": c_r_k, "c_ln_w": c_ln_w, "c_ln_b": c_ln_b,
         "c_vres_w1": c_vres_w1, "c_vres_w2": c_vres_w2, "c_v0": c_v0,
         "w_out_a": w_out_a, "w_out_b": w_out_b, "w_out_c": w_out_c, "w_o": w_o}
    sm = jax.nn.softmax(b_lower.astype(F32), axis=0)
    lb = jnp.cumsum(sm, axis=0) - sm[0:1]
    past_len = cache_attn_k.shape[2]
    y_p, (k_p, v_p, hg_p, rw_p, sh_p) = _run_trunk(x_prompt, 0, P, lb, None)
    y_s, (k_s, v_s, hg_s, rw_s, sh_s) = _run_trunk(
        x_sample, past_len, P, lb, (cache_attn_k, cache_attn_v, state_hgrn, state_rwkv, state_rwkv_shift))
    return (y_p, y_s, k_p, v_p, hg_p, rw_p, sh_p, k_s, v_s, hg_s, rw_s, sh_s)
```

```python
import functools
import math

import jax
import jax.numpy as jnp
from jax import lax
from jax.experimental import pallas as pl
from jax.experimental.pallas import tpu as pltpu

F32 = jnp.float32
BF16 = jnp.bfloat16
BRANCH_DTYPE = BF16

D_MODEL = 1024
CHUNK = 64
EPS = 1e-6
NEG_BIG = -1e30
LOG2_E = 1.4426950408889634
A_HEADS = 8
A_DQK = 64
A_ROT = 16
ROPE_THETA = 500000.0
B_HEADS = 8
C_HEADS = 16
C_DH = 64
C_WIDTH = 1024
C_DECAY_RANK = 64
C_A_RANK = 64
C_VRES_RANK = 32
C_GN_EPS = 64e-5
ATTN_BOUND_LIMIT = 60.0
HGRN_SAFE_SPAN = 60.0
LANES = 128
SUBLANES = 8
BF16_ROWS = 16
SEQ_SUB_CHUNKS = 8
PREP_SUB_CHUNKS = 4
GROUP_W = 1024
N_GROUPS = 15
COL_LO = N_GROUPS * GROUP_W
COL_VRES = COL_LO + LANES
N_PROJ = COL_VRES + LANES
PROJ_TN = 512
N_PROJ_PAD = -(-N_PROJ // PROJ_TN) * PROJ_TN
(G_AQ, G_AK, G_AV, G_AG, G_BQ, G_BF, G_BI, G_BG, G_CR, G_CK, G_CV, G_CG, G_MA, G_MB, G_MC) = range(N_GROUPS)
VMEM_LIMIT = 56 * 1024 * 1024


def _dot(a, b):
    return jnp.dot(a, b, preferred_element_type=F32)


def _dot_nt(a, b):
    return lax.dot_general(a, b, (((1,), (1,)), ((), ())), preferred_element_type=F32)


def _dot_tn(a, b):
    return lax.dot_general(a, b, (((0,), (0,)), ((), ())), preferred_element_type=F32)


def _split2(x):
    hi = x.astype(BF16)
    lo = (x - hi.astype(F32)).astype(BF16)
    return hi, lo


def _split3(x):
    hi = x.astype(BF16)
    r = x - hi.astype(F32)
    mid = r.astype(BF16)
    lo = (r - mid.astype(F32)).astype(BF16)
    return hi, mid, lo


def _sigmoid(x):
    return 1.0 / (1.0 + jnp.exp(-x))


def _silu(x):
    return x * _sigmoid(x)


def _seg_mask(n, seg):
    r = lax.broadcasted_iota(jnp.int32, (n, n), 0) // seg
    c = lax.broadcasted_iota(jnp.int32, (n, n), 1) // seg
    return r == c


def _seg_ones(n, seg):
    return _seg_mask(n, seg).astype(BF16)


def _segsum(x, ones_bf16):
    hi, lo = _split2(x)
    return _dot(hi, ones_bf16) + _dot(lo, ones_bf16)


def _tri_incl(n):
    r = lax.broadcasted_iota(jnp.int32, (n, n), 0)
    c = lax.broadcasted_iota(jnp.int32, (n, n), 1)
    return (c <= r).astype(BF16)


def _cumsum_rows(x, tri_bf16):
    hi, mid, lo = _split3(x)
    return _dot(tri_bf16, hi) + _dot(tri_bf16, mid) + _dot(tri_bf16, lo)


def _params(sem, vmem=None):
    return pltpu.CompilerParams(dimension_semantics=sem, vmem_limit_bytes=vmem or VMEM_LIMIT)


def _proj_kernel(x_ref, g_ref, w_ref, o_ref, h_scr):
    @pl.when(pl.program_id(1) == 0)
    def _():
        x = x_ref[...]
        ms = jnp.mean(x * x, axis=-1, keepdims=True)
        h_scr[...] = (x * lax.rsqrt(ms + EPS) * g_ref[...]).astype(BF16)

    o_ref[...] = _dot(h_scr[...], w_ref[...]).astype(o_ref.dtype)


def _proj(x2d, g, w_bf16):
    m = x2d.shape[0]
    tm = min(2048, m)
    n = w_bf16.shape[1]
    return pl.pallas_call(
        _proj_kernel,
        grid=(m // tm, n // PROJ_TN),
        in_specs=[pl.BlockSpec((tm, D_MODEL), lambda i, j: (i, 0)),
                  pl.BlockSpec((1, D_MODEL), lambda i, j: (0, 0)),
                  pl.BlockSpec((D_MODEL, PROJ_TN), lambda i, j: (0, j))],
        out_specs=pl.BlockSpec((tm, PROJ_TN), lambda i, j: (i, j)),
        out_shape=jax.ShapeDtypeStruct((m, n), BF16),
        scratch_shapes=[pltpu.VMEM((tm, D_MODEL), BF16)],
        compiler_params=_params(("parallel", "arbitrary")),
        name="proj",
    )(x2d, g, w_bf16)


def _qkprep_kernel(*refs):
    q_ref, k_ref, v_ref, qg_ref, kg_ref, c_ref, s1_ref, s2_ref = refs[:8]
    q16t_ref, k32_ref, k16_ref, v32_ref, v16t_ref = refs[-5:]
    ones = _seg_ones(LANES, A_DQK)
    cosv, sin1, sin2 = c_ref[...], s1_ref[...], s2_ref[...]

    def prep(x, gain):
        ss = _segsum(x * x, ones)
        y = x * lax.rsqrt(ss * (1.0 / A_DQK) + EPS) * gain
        return y * cosv + pltpu.roll(y, LANES - A_ROT // 2, 1) * sin1 + pltpu.roll(y, A_ROT // 2, 1) * sin2

    for c in range(GROUP_W // LANES):
        sl = slice(c * LANES, (c + 1) * LANES)
        q = prep(q_ref[:, sl].astype(F32), qg_ref[...])
        q16t_ref[0, sl, :] = (q * (A_DQK ** -0.5 * LOG2_E)).T.astype(BF16)
        k = prep(k_ref[:, sl].astype(F32), kg_ref[...])
        k32_ref[0, :, sl] = k
        k16_ref[:, sl] = k.astype(BF16)
        v = v_ref[:, sl].astype(F32)
        v32_ref[0, :, sl] = v
        v16t_ref[0, sl, :] = v.T.astype(BF16)


def _qkprep(proj, qg128, kg128, cos_t, sin1_t, sin2_t, t_len, layer, depth, kv_stack):
    m = proj.shape[0]
    tm = min(512, t_len)
    nt = t_len // tm
    row = lambda g: pl.BlockSpec((tm, GROUP_W), lambda i, g=g: (i, g))
    tab = pl.BlockSpec((tm, LANES), lambda i: (i % nt, 0))
    vec = pl.BlockSpec((1, LANES), lambda i: (0, 0))
    out = pl.BlockSpec((tm, GROUP_W), lambda i: (i, 0))
    out_t = pl.BlockSpec((1, GROUP_W, tm), lambda i: (i // nt, 0, i % nt))
    transposed = jax.ShapeDtypeStruct((m // t_len, GROUP_W, t_len), BF16)
    stack = jax.ShapeDtypeStruct((depth, m, GROUP_W), F32)
    out_stack = pl.BlockSpec((1, tm, GROUP_W), lambda i: (layer, i, 0))
    in_specs = [row(G_AQ), row(G_AK), row(G_AV), vec, vec, tab, tab, tab]
    args = [proj, proj, proj, qg128, kg128, cos_t, sin1_t, sin2_t]
    aliases = {}
    if kv_stack is not None:
        aliases = {len(args): 1, len(args) + 1: 3}
        in_specs += [pl.BlockSpec(memory_space=pl.ANY)] * 2
        args += list(kv_stack)
    return pl.pallas_call(
        _qkprep_kernel,
        grid=(m // tm,),
        in_specs=in_specs,
        out_specs=[out_t, out_stack, out, out_stack, out_t],
        out_shape=[transposed, stack, jax.ShapeDtypeStruct((m, GROUP_W), BF16), stack, transposed],
        input_output_aliases=aliases,
        compiler_params=_params(("parallel",)),
        name="qkprep",
    )(*args)


def _attn_kernel(sc_ref, qt_ref, k_ref, vt_ref, g_ref, o_ref, m1, l1, a1, m2, l2, a2,
                 *, tq, tk, q_off, nk, out_scale, bounded):
    qi = pl.program_id(2)
    q_first = q_off + qi * tq
    first_chunk_end = (q_first // CHUNK) * CHUNK + CHUNK
    last_vis = ((q_first + tq - 1) // CHUNK) * CHUNK + CHUNK - 1
    n_blocks = jnp.minimum(nk, last_vis // tk + 1)
    n_full = jnp.minimum(n_blocks, first_chunk_end // tk)

    for m, l, a in ((m1, l1, a1), (m2, l2, a2)):
        m[...] = jnp.full(m.shape, NEG_BIG, F32)
        l[...] = jnp.zeros(l.shape, F32)
        a[...] = jnp.zeros(a.shape, F32)

    qt = qt_ref[0]
    dim = lax.broadcasted_iota(jnp.int32, qt.shape, 0)
    zero = jnp.zeros_like(qt)
    q_halves = (jnp.where(dim < A_DQK, qt, zero), jnp.where(dim >= A_DQK, qt, zero))

    def scores(kj, masked, q_lo=0):
        if nk == 1:
            k0, k, vt = 0, k_ref[0], vt_ref[0]
        else:
            k0 = pl.multiple_of(kj * tk, tk)
            k = k_ref[0, pl.ds(k0, tk), :]
            vt = vt_ref[0, :, pl.ds(k0, tk)]
        vis = None
        if masked:
            k_chunk = (k0 + lax.broadcasted_iota(jnp.int32, (tk, 1), 0)) // CHUNK
            q_chunk = (q_first + q_lo + lax.broadcasted_iota(jnp.int32, (1, tq - q_lo), 1)) // CHUNK
            vis = k_chunk <= q_chunk

        def score(qh):
            s = _dot(k, qh[:, q_lo:])
            return jnp.where(vis, s, NEG_BIG) if masked else s

        return score, vt

    def key_partial_sums(pr):
        return jnp.sum(pr.reshape(tk // SUBLANES, SUBLANES, pr.shape[1]), axis=0)

    def online_step(kj, masked):
        score, vt = scores(kj, masked)
        for qh, (m, l, a) in zip(q_halves, ((m1, l1, a1), (m2, l2, a2))):
            s = score(qh)
            m_prev = m[...]
            m_new = jnp.maximum(m_prev, jnp.max(s, axis=0, keepdims=True))
            pr = jnp.exp2(s - m_new)
            alpha = jnp.exp2(m_prev - m_new)
            l[...] = alpha * l[...] + key_partial_sums(pr)
            a[...] = alpha * a[...] + _dot(vt, pr.astype(BF16))
            m[...] = m_new

    def bounded_step(kj, masked, q_lo=0):
        score, vt = scores(kj, masked, q_lo)
        for qh, (m, l, a) in zip(q_halves, ((m1, l1, a1), (m2, l2, a2))):
            pr = jnp.exp2(score(qh) - bound)
            l[:, q_lo:] += key_partial_sums(pr)
            a[:, q_lo:] += _dot(vt, pr.astype(BF16))

    def run(step, split_diagonal=False):
        def full_body(kj, carry):
            step(kj, False)
            return carry

        def masked_body(kj, carry):
            step(kj, True)
            return carry

        lax.fori_loop(0, n_full, full_body, 0)
        if split_diagonal:
            step(n_full, True)
            step(n_full + 1, True, tk)
        else:
            lax.fori_loop(n_full, n_blocks, masked_body, 0)

    lam = sc_ref[0]
    bound = sc_ref[1]
    if bounded:
        in_range = bound <= ATTN_BOUND_LIMIT

        @pl.when(in_range)
        def _():
            run(bounded_step, split_diagonal=(tq == 2 * tk and q_off % tq == 0 and nk > 1))

        @pl.when(jnp.logical_not(in_range))
        def _():
            run(online_step)
    else:
        run(online_step)

    l1_tot = jnp.sum(l1[...], axis=0, keepdims=True)
    l2_tot = jnp.sum(l2[...], axis=0, keepdims=True)
    o = (a1[...] / l1_tot - lam * (a2[...] / l2_tot)).T
    ms = jnp.mean(o * o, axis=-1, keepdims=True)
    o_ref[0] = (o * lax.rsqrt(ms + EPS) * g_ref[...] * out_scale).astype(o_ref.dtype)


def _attn(q16t, k16, v16t, scalars, subln_g, q_off, out_scale, bounded):
    b, _, tq_len = q16t.shape
    tk_len = k16.shape[1]
    tq = min(1024, tq_len)
    tk = 512 if tk_len > 2048 else tk_len
    assert tq_len % tq == 0 and tk_len % tk == 0
    nq, nk = tq_len // tq, tk_len // tk
    kern = functools.partial(_attn_kernel, tq=tq, tk=tk, q_off=q_off, nk=nk, out_scale=out_scale,
                             bounded=bounded)
    run_max = pltpu.VMEM((1, tq), F32)
    key_sum = pltpu.VMEM((SUBLANES, tq), F32)
    acc = pltpu.VMEM((LANES, tq), F32)
    return pl.pallas_call(
        kern,
        grid=(b, A_HEADS, nq),
        in_specs=[pl.BlockSpec(memory_space=pltpu.SMEM),
                  pl.BlockSpec((1, LANES, tq), lambda bi, h, qi: (bi, h, qi)),
                  pl.BlockSpec((1, tk_len, LANES), lambda bi, h, qi: (bi, 0, h)),
                  pl.BlockSpec((1, LANES, tk_len), lambda bi, h, qi: (bi, h, 0)),
                  pl.BlockSpec((1, LANES), lambda bi, h, qi: (0, 0))],
        out_specs=pl.BlockSpec((1, tq, LANES), lambda bi, h, qi: (bi, qi, h)),
        out_shape=jax.ShapeDtypeStruct((b, tq_len, GROUP_W), BRANCH_DTYPE),
        scratch_shapes=[run_max, key_sum, acc, run_max, key_sum, acc],
        compiler_params=_params(("parallel", "parallel", "parallel")),
        name="attn",
    )(scalars, q16t, k16, v16t, subln_g)


def _hgrn_kernel(q_ref, f_ref, i_ref, lb_ref, g_ref, s0_ref, o_ref, sfin_ref,
                 st_scr, cum_scr, qk_scr, kin_scr, oi_scr, *, n_sub):
    c = pl.program_id(1)

    @pl.when(c == 0)
    def _():
        st_scr[...] = s0_ref[0]

    sub_rows = [slice(j * CHUNK, (j + 1) * CHUNK) for j in range(n_sub)]
    head_slices = [slice(h * LANES, (h + 1) * LANES) for h in range(B_HEADS)]

    def per_chunk_row(x, r):
        return jnp.concatenate([jnp.broadcast_to(x[rows, :][r:r + 1], (CHUNK, x.shape[1])) for rows in sub_rows],
                               axis=0)

    z = f_ref[...].astype(F32)
    lb = lb_ref[...]
    log_f = jnp.log(lb + (1.0 - lb) * _sigmoid(z))
    k_in = (1.0 - lb) * _sigmoid(-z)
    q = _silu(q_ref[...].astype(F32))
    tri = _tri_incl(CHUNK)
    cum = jnp.concatenate([_cumsum_rows(log_f[rows, :], tri) for rows in sub_rows], axis=0)
    row = lax.broadcasted_iota(jnp.int32, (CHUNK, 1), 0)

    rel = cum - per_chunk_row(cum, CHUNK // 2 - 1)
    safe = jnp.max(jnp.abs(rel)) <= HGRN_SAFE_SPAN

    @pl.when(safe)
    def _():
        qe = (q * jnp.exp(rel)).astype(BF16)
        ke = (k_in * jnp.exp(-rel)).astype(BF16)
        causal = lax.broadcasted_iota(jnp.int32, (CHUNK, CHUNK), 1) <= lax.broadcasted_iota(
            jnp.int32, (CHUNK, CHUNK), 0)
        tiles = [(rows, sl) for rows in sub_rows for sl in head_slices]
        scores = [jnp.where(causal, _dot_nt(qe[rows, sl], ke[rows, sl]), 0.0).astype(BF16) for rows, sl in tiles]
        for (rows, sl), sc in zip(tiles, scores):
            oi_scr[rows, sl] = _dot(sc, i_ref[rows, sl])

    @pl.when(jnp.logical_not(safe))
    def _():
        cum_scr[...] = cum
        qk_scr[...] = q
        kin_scr[...] = k_in

        def chunk_body(j, carry):
            r0 = pl.multiple_of(j * CHUNK, CHUNK)
            for sl in head_slices:
                cum_h = cum_scr[pl.ds(r0, CHUNK), sl]
                kin_h = kin_scr[pl.ds(r0, CHUNK), sl]
                i_h = i_ref[pl.ds(r0, CHUNK), sl].astype(F32)

                def body(g, inner, sl=sl, cum_h=cum_h, kin_h=kin_h, i_h=i_h):
                    g8 = pl.multiple_of(g * SUBLANES, SUBLANES)
                    c_tile = cum_scr[pl.ds(r0 + g8, SUBLANES), sl]
                    q_tile = qk_scr[pl.ds(r0 + g8, SUBLANES), sl]
                    out_rows = []
                    for r in range(SUBLANES):
                        dec = jnp.exp(jnp.minimum(c_tile[r:r + 1] - cum_h, 0.0))
                        col = jnp.sum(dec * (kin_h * q_tile[r:r + 1]), axis=1, keepdims=True)
                        col = jnp.where(row <= g8 + r, col, 0.0)
                        out_rows.append(jnp.sum(col * i_h, axis=0, keepdims=True))
                    oi_scr[pl.ds(r0 + g8, SUBLANES), sl] = jnp.concatenate(out_rows, axis=0)
                    return inner

                lax.fori_loop(0, CHUNK // SUBLANES, body, 0)
            return carry

        lax.fori_loop(0, n_sub, chunk_body, 0)

    cum_last = per_chunk_row(cum, CHUNK - 1)
    q_dec = (q * jnp.exp(cum)).astype(BF16)
    k_tail = (k_in * jnp.exp(cum_last - cum)).astype(BF16)
    decay = jnp.exp(cum_last)
    update = [[_dot_tn(i_ref[rows, sl], k_tail[rows, sl]) for sl in head_slices] for rows in sub_rows]
    states = [st_scr[h] for h in range(B_HEADS)]
    for j, rows in enumerate(sub_rows):
        inter = [_dot_nt(q_dec[rows, sl], st.astype(BF16)) for sl, st in zip(head_slices, states)]
        states = [st * decay[rows, sl][:1] + upd for st, sl, upd in zip(states, head_slices, update[j])]
        for sl, x in zip(head_slices, inter):
            o = oi_scr[rows, sl] + x
            ms = jnp.mean(o * o, axis=-1, keepdims=True)
            o_ref[rows, sl] = (o * lax.rsqrt(ms + EPS) * g_ref[...]).astype(o_ref.dtype)
    for h, st in enumerate(states):
        st_scr[h] = st

    sfin_ref[0] = st_scr[...]


def _hgrn(proj, lb, norm_g128, s0_t, bsz, nc):
    m = proj.shape[0]
    n_sub = _sub_chunks(nc)
    steps = nc // n_sub
    rows = n_sub * CHUNK
    row = lambda g: pl.BlockSpec((rows, GROUP_W), lambda b, c, g=g: (b * steps + c, g))
    st_spec = pl.BlockSpec((1, B_HEADS, LANES, LANES), lambda b, c: (b, 0, 0, 0))
    scratch = pltpu.VMEM((rows, GROUP_W), F32)
    return pl.pallas_call(
        functools.partial(_hgrn_kernel, n_sub=n_sub),
        grid=(bsz, steps),
        in_specs=[row(G_BQ), row(G_BF), row(G_BI),
                  pl.BlockSpec((1, GROUP_W), lambda b, c: (0, 0)),
                  pl.BlockSpec((1, LANES), lambda b, c: (0, 0)),
                  st_spec],
        out_specs=[pl.BlockSpec((rows, GROUP_W), lambda b, c: (b * steps + c, 0)), st_spec],
        out_shape=[jax.ShapeDtypeStruct((m, GROUP_W), BRANCH_DTYPE),
                   jax.ShapeDtypeStruct((bsz, B_HEADS, LANES, LANES), F32)],
        scratch_shapes=[pltpu.VMEM((B_HEADS, LANES, LANES), F32), scratch, scratch, scratch, scratch],
        compiler_params=_params(("parallel", "arbitrary")),
        name="hgrn",
    )(proj, proj, proj, lb, norm_g128, s0_t)


def _rwkprep_kernel(*refs, has_vres, n_sub):
    (cr_ref, ck_ref, cv_ref, clo_ref, spr_ref, spk_ref, spv_ref, splo_ref,
     mur_ref, muk_ref, muv_ref, mulo_ref, w0_ref, w2h_ref, w2l_ref, a0_ref, a2h_ref, a2l_ref,
     kk_ref, ka_ref, rk_ref) = refs[:21]
    pos = 21
    if has_vres:
        vres_ref, vf_ref, v0_ref, vw2h_ref, vw2l_ref = refs[pos:pos + 5]
        pos += 5
    (at_ref, rt_ref, bh_ref, kh_ref, vc_ref, bonus_ref, gam_ref, nab_ref, arb_ref, g_ref, y0_ref,
     shr_ref, shk_ref, shv_ref, shlo_ref) = refs[pos:pos + 15]
    pr_scr, pk_scr, pv_scr, plo_scr = refs[pos + 15:]
    c = pl.program_id(1)

    @pl.when(c == 0)
    def _():
        pr_scr[...] = spr_ref[0]
        pk_scr[...] = spk_ref[0]
        pv_scr[...] = spv_ref[0]
        plo_scr[...] = splo_ref[0]

    def shifted(x_ref, prev_scr, mu_ref, last_ref):
        x = x_ref[...].astype(F32)
        row = lax.broadcasted_iota(jnp.int32, x.shape, 0)
        prev = jnp.where(row == 0, prev_scr[...], pltpu.roll(x, 1, 0))
        last = x[x.shape[0] - 1:, :]
        prev_scr[...] = last
        last_ref[0] = last
        return x + (prev - x) * mu_ref[...]

    r = shifted(cr_ref, pr_scr, mur_ref, shr_ref)
    k0 = shifted(ck_ref, pk_scr, muk_ref, shk_ref)
    v = shifted(cv_ref, pv_scr, muv_ref, shv_ref)
    lo = shifted(clo_ref, plo_scr, mulo_ref, shlo_ref)

    def lowrank(x, wh_ref, wl_ref):
        xh, xl = _split2(x)
        return _dot(xh, wh_ref[...]) + _dot(xl, wh_ref[...]) + _dot(xh, wl_ref[...])

    w_in = w0_ref[...] + lowrank(jnp.tanh(lo), w2h_ref, w2l_ref)
    nw = -w_in
    softplus = jnp.maximum(nw, 0.0) + jnp.log(1.0 + jnp.exp(-jnp.abs(nw)))
    log_decay = -jnp.exp(-softplus - 0.5)
    a_sig = _sigmoid(a0_ref[...] + lowrank(lo, a2h_ref, a2l_ref))
    if has_vres:
        v_mix = _sigmoid(v0_ref[...] + lowrank(vres_ref[...].astype(F32), vw2h_ref, vw2l_ref))
        v = v + (vf_ref[...] - v) * v_mix
    vc_ref[...] = v

    ones = _seg_ones(LANES, C_DH)
    sub_rows = [slice(j * CHUNK, (j + 1) * CHUNK) for j in range(n_sub)]
    tri = _tri_incl(CHUNK)
    cum = jnp.concatenate([_cumsum_rows(log_decay[rows, :], tri) for rows in sub_rows], axis=0)
    for j, rows in enumerate(sub_rows):
        gam_ref[j] = jnp.exp(cum[rows, :][CHUNK - 1:CHUNK])
    cum_last = jnp.concatenate([jnp.broadcast_to(cum[rows, :][CHUNK - 1:CHUNK], (CHUNK, C_WIDTH))
                                for rows in sub_rows], axis=0)
    e_prev = jnp.exp(cum - log_decay)
    e_cum = jnp.exp(cum)
    e_inv = jnp.exp(-cum)
    e_tail = jnp.exp(cum_last - cum)

    lane = lax.broadcasted_iota(jnp.int32, (CHUNK, LANES), 1)
    par1 = lane >= C_DH
    tcol = lax.broadcasted_iota(jnp.int32, (CHUNK, LANES), 0)
    scol = lane % C_DH
    strict = jnp.concatenate([scol < tcol, scol < tcol], axis=1)
    incl = jnp.concatenate([scol <= tcol, scol <= tcol], axis=1)
    low_mask = jnp.concatenate([strict, incl], axis=0)

    pairs = [slice(p * LANES, (p + 1) * LANES) for p in range(C_WIDTH // LANES)]

    def head_sums(x):
        n = x.shape[0]
        stacked = _segsum(jnp.concatenate([x[:, sl] for sl in pairs], axis=0), ones)
        return jnp.concatenate([stacked[p * n:(p + 1) * n] for p in range(len(pairs))], axis=1)

    kk = k0 * kk_ref[...]
    kk = kk / jnp.maximum(jnp.sqrt(head_sums(kk * kk)), 1e-12)
    k = k0 * (1.0 + (a_sig - 1.0) * ka_ref[...])
    b_vec = kk * a_sig
    bonus_ref[...] = head_sums(r * k * rk_ref[...]) * v
    a_t = -kk * e_prev
    r_t = r * e_cum
    a16, r16 = a_t.astype(BF16), r_t.astype(BF16)
    at_ref[...] = a16
    rt_ref[...] = r16
    bh_ref[...] = (b_vec * e_tail).astype(BF16)
    kh_ref[...] = (k * e_tail).astype(BF16)
    b16, k16, v16 = (b_vec * e_inv).astype(BF16), (k * e_inv).astype(BF16), v.astype(BF16)
    zero = jnp.zeros((CHUNK, LANES), BF16)

    def block_diag(x):
        return [jnp.where(par1, zero, x), jnp.where(par1, x, zero)]

    tiles = [(rows, sl) for rows in sub_rows for sl in pairs]
    prods = [jnp.where(low_mask,
                       _dot_nt(jnp.concatenate([a16[rows, sl], r16[rows, sl]], axis=0),
                               jnp.concatenate(block_diag(b16[rows, sl]) + block_diag(k16[rows, sl]), axis=0)),
                       0.0) for rows, sl in tiles]
    gys = [_dot(prod[:, LANES:].astype(BF16), jnp.concatenate(block_diag(v16[rows, sl]), axis=0))
           for (rows, sl), prod in zip(tiles, prods)]
    for (rows, sl), prod, gy in zip(tiles, prods, gys):
        nab_ref[rows, sl] = prod[:CHUNK, :LANES].astype(BF16)
        arb_ref[rows, sl] = prod[CHUNK:, :LANES].astype(BF16)
        g_ref[rows, sl] = gy[:CHUNK].astype(BF16)
        y0_ref[rows, sl] = gy[CHUNK:]


def _rwkprep(proj, shift_parts, params, vres, bsz, nc):
    m = proj.shape[0]
    has_vres = vres is not None
    n_sub = PREP_SUB_CHUNKS if nc % PREP_SUB_CHUNKS == 0 else 1
    steps = nc // n_sub
    rows = n_sub * CHUNK
    row = lambda g: pl.BlockSpec((rows, GROUP_W), lambda b, c, g=g: (b * steps + c, g))
    lo_spec = pl.BlockSpec((rows, LANES), lambda b, c: (b * steps + c, COL_LO // LANES))
    st = lambda w: pl.BlockSpec((1, 1, w), lambda b, c: (b, 0, 0))
    vec = lambda w: pl.BlockSpec((1, w), lambda b, c: (0, 0))
    mat = lambda: pl.BlockSpec((LANES, GROUP_W), lambda b, c: (0, 0))
    out = pl.BlockSpec((rows, GROUP_W), lambda b, c: (b * steps + c, 0))
    in_specs = [row(G_CR), row(G_CK), row(G_CV), lo_spec, st(GROUP_W), st(GROUP_W), st(GROUP_W), st(LANES),
                vec(GROUP_W), vec(GROUP_W), vec(GROUP_W), vec(LANES),
                vec(GROUP_W), mat(), mat(), vec(GROUP_W), mat(), mat(),
                vec(GROUP_W), vec(GROUP_W), vec(GROUP_W)]
    args = [proj, proj, proj, proj, *shift_parts,
            params["mu_r"], params["mu_k"], params["mu_v"], params["mu_lo"],
            params["w0"], params["w2h"], params["w2l"], params["a0"], params["a2h"], params["a2l"],
            params["k_k"], params["k_a"], params["r_k"]]
    if has_vres:
        in_specs += [pl.BlockSpec((rows, LANES), lambda b, c: (b * steps + c, COL_VRES // LANES)),
                     out, vec(GROUP_W), mat(), mat()]
        args += [proj, vres["v_first"], vres["v0"], vres["w2h"], vres["w2l"]]
    big = jax.ShapeDtypeStruct((m, GROUP_W), F32)
    half = jax.ShapeDtypeStruct((m, GROUP_W), BF16)
    out_shape = [half] * 4 + [big] * 2 + [jax.ShapeDtypeStruct((bsz * nc, 1, GROUP_W), F32)] + [
        half, half, half, big] + [
        jax.ShapeDtypeStruct((bsz, 1, GROUP_W), F32)] * 3 + [jax.ShapeDtypeStruct((bsz, 1, LANES), F32)]
    out_specs = [out] * 6 + [pl.BlockSpec((n_sub, 1, GROUP_W), lambda b, c: (b * steps + c, 0, 0))] + [out] * 4 + [
        st(GROUP_W)] * 3 + [st(LANES)]
    return pl.pallas_call(
        functools.partial(_rwkprep_kernel, has_vres=has_vres, n_sub=n_sub),
        grid=(bsz, steps),
        in_specs=in_specs,
        out_specs=out_specs,
        out_shape=out_shape,
        scratch_shapes=[pltpu.VMEM((1, GROUP_W), F32)] * 3 + [pltpu.VMEM((1, LANES), F32)],
        compiler_params=_params(("parallel", "arbitrary")),
        name="rwkprep",
    )(*args)


def _rwksolve_kernel(n_ref, out_ref, t_ref):
    col = lax.broadcasted_iota(jnp.int32, (CHUNK, LANES), 0)

    def group_body(tg, carry):
        t0 = pl.multiple_of(tg * BF16_ROWS, BF16_ROWS)
        for r in range(BF16_ROWS):
            t = t0 + r

            def s_body(sg, acc, t=t):
                s0 = pl.multiple_of(sg * BF16_ROWS, BF16_ROWS)
                coef = n_ref[t, pl.ds(s0, BF16_ROWS), :].astype(F32)
                for q in range(BF16_ROWS):
                    acc = acc + coef[q:q + 1] * t_ref[s0 + q]
                return acc

            acc = lax.fori_loop(0, tg, s_body, jnp.where(col == t, 1.0, 0.0).astype(F32))
            coef = n_ref[t, pl.ds(t0, BF16_ROWS), :].astype(F32)
            for q in range(r):
                acc = acc + coef[q:q + 1] * t_ref[t0 + q]
            t_ref[t] = acc
        return carry

    lax.fori_loop(0, CHUNK // BF16_ROWS, group_body, 0)
    out_ref[...] = t_ref[...].astype(out_ref.dtype)


def _rwksolve(n_bl):
    n_inst = n_bl.shape[-1]
    spec = pl.BlockSpec((CHUNK, CHUNK, LANES), lambda i: (0, 0, i))
    return pl.pallas_call(
        _rwksolve_kernel,
        grid=(n_inst // LANES,),
        in_specs=[spec],
        out_specs=spec,
        out_shape=jax.ShapeDtypeStruct(n_bl.shape, BF16),
        scratch_shapes=[pltpu.VMEM((CHUNK, CHUNK, LANES), F32)],
        compiler_params=_params(("parallel",)),
        name="rwksolve",
    )(n_bl)


def _rwkseq_kernel(tinv_ref, at_ref, g_ref, rt_ref, arb_ref, y0_ref, v_ref, bh_ref, kh_ref, gam_ref, bonus_ref,
                   lnw_ref, lnb_ref, s0_ref, o_ref, sfin_ref, st_scr, *, n_sub):
    c = pl.program_id(1)

    @pl.when(c == 0)
    def _():
        st_scr[...] = s0_ref[0]

    ones = _seg_ones(LANES, C_DH)
    lane = lax.broadcasted_iota(jnp.int32, (CHUNK, LANES), 1)
    par1 = lane >= C_DH
    bd = _seg_mask(LANES, C_DH)
    pairs = [slice(p * LANES, (p + 1) * LANES) for p in range(C_WIDTH // LANES)]
    zero = jnp.zeros((CHUNK, LANES), BF16)

    def block_diag(x):
        return jnp.concatenate([jnp.where(par1, zero, x), jnp.where(par1, x, zero)], axis=0)

    def apply_inverse(rows):
        return [_dot(tinv_ref[rows, sl],
                     jnp.concatenate([block_diag(at_ref[rows, sl]), block_diag(g_ref[rows, sl])], axis=1))
                for sl in pairs]

    states = [st_scr[p] for p in range(len(pairs))]
    sub_rows = [slice(j * CHUNK, (j + 1) * CHUNK) for j in range(n_sub)]
    wu = apply_inverse(sub_rows[0])
    y_parts = []
    for j, rows in enumerate(sub_rows):
        st16 = [st.astype(BF16) for st in states]
        u = [_dot_nt(x[:, :LANES].astype(BF16), st) + x[:, LANES:] for x, st in zip(wu, st16)]
        y_state = [_dot_nt(rt_ref[rows, sl], st) for sl, st in zip(pairs, st16)]
        if j + 1 < n_sub:
            wu = apply_inverse(sub_rows[j + 1])
        u16 = [x.astype(BF16) for x in u]
        v16 = v_ref[rows, :].astype(BF16)
        upd = [_dot_tn(jnp.concatenate([x, v16[:, sl]], axis=0),
                       jnp.concatenate([bh_ref[rows, sl], kh_ref[rows, sl]], axis=0)) for sl, x in zip(pairs, u16)]
        y_u = [_dot(arb_ref[rows, sl], block_diag(x)) for sl, x in zip(pairs, u16)]
        states = [st * gam_ref[j, :, sl] + jnp.where(bd, x, 0.0) for st, sl, x in zip(states, pairs, upd)]
        y_parts += [a + b + y0_ref[rows, sl] for sl, a, b in zip(pairs, y_state, y_u)]
    for p, st in enumerate(states):
        st_scr[p] = st
    y = jnp.concatenate(y_parts, axis=0)
    d = y - _segsum(y, ones) * (1.0 / C_DH)
    var = _segsum(d * d, ones) * (1.0 / C_DH)
    dn = d * lax.rsqrt(var + C_GN_EPS)
    for j, rows in enumerate(sub_rows):
        for p, sl in enumerate(pairs):
            tile = dn[(j * len(pairs) + p) * CHUNK:(j * len(pairs) + p + 1) * CHUNK]
            o_ref[rows, sl] = (tile * lnw_ref[:, sl] + lnb_ref[:, sl] + bonus_ref[rows, sl]).astype(o_ref.dtype)

    sfin_ref[0] = st_scr[...]


def _sub_chunks(nc):
    return SEQ_SUB_CHUNKS if nc % SEQ_SUB_CHUNKS == 0 else 1


def _rwkseq(tinv, at, g_mat, rt, arb, y0, v, bh, kh, gam, bonus, ln_w, ln_b, s0_bd, bsz, nc):
    m = tinv.shape[0]
    n_sub = _sub_chunks(nc)
    steps = nc // n_sub
    row = pl.BlockSpec((n_sub * CHUNK, GROUP_W), lambda b, c: (b * steps + c, 0))
    vec = pl.BlockSpec((1, GROUP_W), lambda b, c: (0, 0))
    st_spec = pl.BlockSpec((1, C_WIDTH // LANES, LANES, LANES), lambda b, c: (b, 0, 0, 0))
    return pl.pallas_call(
        functools.partial(_rwkseq_kernel, n_sub=n_sub),
        grid=(bsz, steps),
        in_specs=[row] * 9 + [pl.BlockSpec((n_sub, 1, GROUP_W), lambda b, c: (b * steps + c, 0, 0)), row, vec, vec,
                  st_spec],
        out_specs=[row, st_spec],
        out_shape=[jax.ShapeDtypeStruct((m, GROUP_W), BRANCH_DTYPE),
                   jax.ShapeDtypeStruct((bsz, C_WIDTH // LANES, LANES, LANES), F32)],
        scratch_shapes=[pltpu.VMEM((C_WIDTH // LANES, LANES, LANES), F32)],
        compiler_params=_params(("parallel", "arbitrary")),
        name="rwkseq",
    )(tinv, at, g_mat, rt, arb, y0, v, bh, kh, gam, bonus, ln_w, ln_b, s0_bd)


def _merge_kernel(x_ref, oa_ref, ob_ref, oc_ref, ag_ref, bg_ref, cg_ref, ma_ref, mb_ref, mc_ref,
                  wa_ref, wb_ref, wc_ref, wo_ref, y_ref):
    def branch(o_ref, gate_ref, w_ref):
        return _dot((o_ref[...] * _silu(gate_ref[...].astype(F32))).astype(BF16), w_ref[...])

    merged = (_sigmoid(ma_ref[...].astype(F32)) * branch(oa_ref, ag_ref, wa_ref)
              + _sigmoid(mb_ref[...].astype(F32)) * branch(ob_ref, bg_ref, wb_ref)
              + _sigmoid(mc_ref[...].astype(F32)) * branch(oc_ref, cg_ref, wc_ref))
    y_ref[...] = x_ref[...] + _dot(merged.astype(BF16), wo_ref[...])


def _merge(x2d, o_a, o_b, o_c, proj, wa, wb, wc, wo):
    m = x2d.shape[0]
    tm = min(256, m)
    row = pl.BlockSpec((tm, GROUP_W), lambda i: (i, 0))
    grp = lambda g: pl.BlockSpec((tm, GROUP_W), lambda i, g=g: (i, g))
    wsp = pl.BlockSpec((GROUP_W, D_MODEL), lambda i: (0, 0))
    return pl.pallas_call(
        _merge_kernel,
        grid=(m // tm,),
        in_specs=[row, row, row, row, grp(G_AG), grp(G_BG), grp(G_CG), grp(G_MA), grp(G_MB), grp(G_MC),
                  wsp, wsp, wsp, wsp],
        out_specs=row,
        out_shape=jax.ShapeDtypeStruct((m, D_MODEL), F32),
        compiler_params=_params(("parallel",)),
        name="merge",
    )(x2d, o_a, o_b, o_c, proj, proj, proj, proj, proj, proj, wa, wb, wc, wo)


_C_OFF = 8 * GROUP_W
_CP_R = (_C_OFF, _C_OFF + C_WIDTH)
_CP_WLO = (_CP_R[1], _CP_R[1] + C_DECAY_RANK)
_CP_K = (_CP_WLO[1], _CP_WLO[1] + C_WIDTH)
_CP_V = (_CP_K[1], _CP_K[1] + C_WIDTH)
_CP_ALO = (_CP_V[1], _CP_V[1] + C_A_RANK)
_REST = _CP_ALO[1]


def _regroup_w_in(w, vres_w1):
    cols = [w[:, :_C_OFF], w[:, _CP_R[0]:_CP_R[1]], w[:, _CP_K[0]:_CP_K[1]], w[:, _CP_V[0]:_CP_V[1]],
            w[:, _REST:], w[:, _CP_WLO[0]:_CP_WLO[1]], w[:, _CP_ALO[0]:_CP_ALO[1]]]
    vres = jnp.zeros((D_MODEL, LANES), F32)
    if vres_w1 is not None:
        vres = vres.at[:, :C_VRES_RANK].set(vres_w1)
    cols.append(vres)
    cols.append(jnp.zeros((D_MODEL, N_PROJ_PAD - N_PROJ), F32))
    return jnp.concatenate(cols, axis=1).astype(BF16)


def _split_shift(s):
    o = _C_OFF
    part = lambda a: s[:, a[0] - o:a[1] - o]
    lo = jnp.concatenate([part(_CP_WLO), part(_CP_ALO)], axis=1)
    return [x[:, None, :] for x in (part(_CP_R), part(_CP_K), part(_CP_V), lo)]


def _join_shift(r, k, v, lo):
    r, k, v, lo = (x[:, 0, :] for x in (r, k, v, lo))
    return jnp.concatenate([r, lo[:, :C_DECAY_RANK], k, v, lo[:, C_DECAY_RANK:]], axis=1)


def _pad_rows(w, row0):
    out = jnp.zeros((LANES, w.shape[1]), F32).at[row0:row0 + w.shape[0]].set(w)
    hi = out.astype(BF16)
    return hi, (out - hi.astype(F32)).astype(BF16)


def _rope_tables(pos):
    half = A_ROT // 2
    inv_freq = ROPE_THETA ** (-(jnp.arange(half, dtype=F32) * (2.0 / A_ROT)))
    ang = pos.astype(F32)[:, None] * inv_freq[None, :]
    cos, sin = jnp.cos(ang), jnp.sin(ang)
    t = pos.shape[0]
    one = jnp.ones((t, A_DQK - A_ROT), F32)
    zero = jnp.zeros((t, A_DQK - A_ROT), F32)
    z8 = jnp.zeros((t, half), F32)
    c64 = jnp.concatenate([cos, cos, one], axis=1)
    s1 = jnp.concatenate([-sin, z8, zero], axis=1)
    s2 = jnp.concatenate([z8, sin, zero], axis=1)
    tile = lambda x: jnp.concatenate([x, x], axis=1)
    return tile(c64), tile(s1), tile(s2)


def _to_lanes(x, rows):
    return x.reshape(rows, CHUNK, C_HEADS, C_DH).transpose(1, 3, 0, 2).reshape(CHUNK, C_DH, rows * C_HEADS)


def _from_lanes(x, rows):
    return x.reshape(CHUNK, C_DH, rows, C_HEADS).transpose(2, 0, 3, 1).reshape(rows * CHUNK, C_WIDTH)


def _layer(l, x2d, bsz, t_len, q_off, P, lb, past, v_first, kv_stack):
    nc = t_len // CHUNK
    m = bsz * t_len
    w_in = _regroup_w_in(P["w_in"][l], P["c_vres_w1"][l - 1] if l > 0 else None)
    proj = _proj(x2d, P["norm_g"][l][None, :], w_in)

    pos = q_off + jnp.arange(t_len, dtype=jnp.int32)
    cos_t, sin1_t, sin2_t = _rope_tables(pos)
    tile2 = lambda g: jnp.concatenate([g, g])[None, :]
    q16t, k_stack, k16, v_stack, v16t = _qkprep(proj, tile2(P["a_qnorm_g"][l]), tile2(P["a_knorm_g"][l]),
                                                cos_t, sin1_t, sin2_t, t_len, l, P["w_in"].shape[0], kv_stack)
    k16 = k16.reshape(bsz, t_len, GROUP_W)
    if past is not None:
        pk, pv = past[0][l], past[1][l]
        p_len = pk.shape[1]
        k16 = jnp.concatenate([pk.reshape(bsz, p_len, GROUP_W).astype(BF16), k16], axis=1)
        pv_t = jnp.swapaxes(pv.reshape(bsz, p_len, GROUP_W).astype(BF16), 1, 2)
        v16t = jnp.concatenate([pv_t, v16t], axis=2)
    lam_init = 0.8 - 0.6 * math.exp(-0.3 * l)
    lp = P["a_lambda"][l].astype(F32)
    lam = jnp.exp(jnp.sum(lp[0] * lp[1])) - jnp.exp(jnp.sum(lp[2] * lp[3])) + lam_init
    bound = (8.0 * LOG2_E * 1.02) * jnp.max(jnp.abs(P["a_qnorm_g"][l])) * jnp.max(jnp.abs(P["a_knorm_g"][l]))
    o_a = _attn(q16t, k16, v16t, jnp.stack([lam, bound]).astype(F32), P["a_subln_g"][l][None, :], q_off,
                1.0 - lam_init, bounded=past is None)
    o_a = o_a.reshape(m, GROUP_W)

    if past is None:
        s_h = jnp.zeros((bsz, B_HEADS, LANES, LANES), F32)
    else:
        s_h = jnp.swapaxes(past[2][l].astype(F32), -1, -2)
    o_b, s_h_new = _hgrn(proj, lb[l][None, :], P["b_norm_g"][l][None, :], s_h, bsz, nc)
    s_h_new = jnp.swapaxes(s_h_new, -1, -2)

    if past is None:
        shift_prev = jnp.zeros((bsz, 3 * C_WIDTH + C_DECAY_RANK + C_A_RANK), F32)
        s_r = jnp.zeros((bsz, C_HEADS, C_DH, C_DH), F32)
    else:
        shift_prev, s_r = past[4][l], past[3][l].astype(F32)
    mu = _split_shift(P["c_shift_mu"][l][None, :])
    w2h, w2l = _pad_rows(P["c_w2"][l], 0)
    a2h, a2l = _pad_rows(P["c_a2"][l], C_DECAY_RANK)
    cparams = {"mu_r": mu[0][0], "mu_k": mu[1][0], "mu_v": mu[2][0], "mu_lo": mu[3][0],
               "w0": P["c_w0"][l][None, :], "w2h": w2h, "w2l": w2l,
               "a0": P["c_a0"][l][None, :], "a2h": a2h, "a2l": a2l,
               "k_k": P["c_k_k"][l][None, :], "k_a": P["c_k_a"][l][None, :],
               "r_k": P["c_r_k"][l].reshape(1, C_WIDTH)}
    vres = None
    if l > 0:
        vh, vl = _pad_rows(P["c_vres_w2"][l - 1], 0)
        vres = {"v_first": v_first, "v0": P["c_v0"][l - 1][None, :], "w2h": vh, "w2l": vl}
    (a_t, r_t, b_h, k_h, v_c, bonus, gam, nab, arb, g_mat, y0, sh_r, sh_k, sh_v, sh_lo) = _rwkprep(
        proj, _split_shift(shift_prev), cparams, vres, bsz, nc)
    rows = bsz * nc
    n_inst = rows * C_HEADS
    pad = (-n_inst) % LANES
    to_lanes = lambda x: jnp.pad(_to_lanes(x, rows), ((0, 0), (0, 0), (0, pad))) if pad else _to_lanes(x, rows)
    tinv = _from_lanes(_rwksolve(to_lanes(nab))[:, :, :n_inst], rows)
    eye2 = jnp.eye(2, dtype=F32)
    s_bd = jnp.einsum("bpqvk,qr->bpqvrk", s_r.reshape(bsz, C_HEADS // 2, 2, C_DH, C_DH), eye2)
    s_bd = s_bd.reshape(bsz, C_HEADS // 2, LANES, LANES)
    o_c, s_bd_new = _rwkseq(tinv, a_t, g_mat, r_t, arb, y0, v_c, b_h, k_h, gam, bonus,
                            P["c_ln_w"][l][None, :], P["c_ln_b"][l][None, :], s_bd, bsz, nc)
    s_new6 = s_bd_new.reshape(bsz, C_HEADS // 2, 2, C_DH, 2, C_DH)
    s_r_new = jnp.stack([s_new6[:, :, 0, :, 0, :], s_new6[:, :, 1, :, 1, :]], axis=2)
    s_r_new = s_r_new.reshape(bsz, C_HEADS, C_DH, C_DH)
    shift_new = _join_shift(sh_r, sh_k, sh_v, sh_lo)

    bf = lambda w: w.astype(BF16)
    y = _merge(x2d, o_a, o_b, o_c, proj, bf(P["w_out_a"][l]), bf(P["w_out_b"][l]), bf(P["w_out_c"][l]),
               bf(P["w_o"][l]))
    return y, (s_h_new, s_r_new, shift_new), v_c, (k_stack, v_stack)


def _run_trunk(x, q_off, P, lb, past):
    bsz, t_len, _ = x.shape
    depth = P["w_in"].shape[0]
    x2d = x.reshape(bsz * t_len, D_MODEL)
    outs = ([], [], [])
    v_first, kv_stack = None, None
    for l in range(depth):
        x2d, entries, v_c, kv_stack = _layer(l, x2d, bsz, t_len, q_off, P, lb, past, v_first, kv_stack)
        if l == 0:
            v_first = v_c
        for lst, e in zip(outs, entries):
            lst.append(e)
    kv_rows = [s.reshape(depth, bsz, t_len, A_HEADS, 2 * A_DQK) for s in kv_stack]
    return x2d.reshape(bsz, t_len, D_MODEL), kv_rows + [jnp.stack(lst) for lst in outs]


def kernel(x_prompt, x_sample, cache_attn_k, cache_attn_v, state_hgrn, state_rwkv, state_rwkv_shift,
           norm_g, w_in, a_qnorm_g, a_knorm_g, a_lambda, a_subln_g, b_lower, b_norm_g,
           c_shift_mu, c_w0, c_w2, c_a0, c_a2, c_k_k, c_k_a, c_r_k, c_ln_w, c_ln_b,
           c_vres_w1, c_vres_w2, c_v0, w_out_a, w_out_b, w_out_c, w_o):
    P = {"norm_g": norm_g, "w_in": w_in, "a_qnorm_g": a_qnorm_g, "a_knorm_g": a_knorm_g,
         "a_lambda": a_lambda, "a_subln_g": a_subln_g, "b_norm_g": b_norm_g,
         "c_shift_mu": c_shift_mu, "c_w0": c_w0, "c_w2": c_w2, "c_a0": c_a0, "c_a2": c_a2,
         "c_k_k": c_k_k, "c_k_a": c_k_a, "c_r_k": c_r_k, "c_ln_w": c_ln_w, "c_ln_b": c_ln_b,
         "c_vres_w1": c_vres_w1, "c_vres_w2": c_vres_w2, "c_v0": c_v0,
         "w_out_a": w_out_a, "w_out_b": w_out_b, "w_out_c": w_out_c, "w_o": w_o}
    sm = jax.nn.softmax(b_lower.astype(F32), axis=0)
    lb = jnp.cumsum(sm, axis=0) - sm[0:1]
    past_len = cache_attn_k.shape[2]
    y_p, (k_p, v_p, hg_p, rw_p, sh_p) = _run_trunk(x_prompt, 0, P, lb, None)
    y_s, (k_s, v_s, hg_s, rw_s, sh_s) = _run_trunk(
        x_sample, past_len, P, lb, (cache_attn_k, cache_attn_v, state_hgrn, state_rwkv, state_rwkv_shift))
    return (y_p, y_s, k_p, v_p, hg_p, rw_p, sh_p, k_s, v_s, hg_s, rw_s, sh_s)
```

```python
import functools
import math

import jax
import jax.numpy as jnp
from jax import lax
from jax.experimental import pallas as pl
from jax.experimental.pallas import tpu as pltpu

F32 = jnp.float32
BF16 = jnp.bfloat16
BRANCH_DTYPE = BF16

D_MODEL = 1024
CHUNK = 64
EPS = 1e-6
NEG_BIG = -1e30
LOG2_E = 1.4426950408889634
A_HEADS = 8
A_DQK = 64
A_ROT = 16
ROPE_THETA = 500000.0
B_HEADS = 8
C_HEADS = 16
C_DH = 64
C_WIDTH = 1024
C_DECAY_RANK = 64
C_A_RANK = 64
C_VRES_RANK = 32
C_GN_EPS = 64e-5
ATTN_BOUND_LIMIT = 60.0
HGRN_SAFE_SPAN = 60.0
LANES = 128
SUBLANES = 8
BF16_ROWS = 16
SEQ_SUB_CHUNKS = 8
PREP_SUB_CHUNKS = 4
GROUP_W = 1024
N_GROUPS = 15
COL_LO = N_GROUPS * GROUP_W
COL_VRES = COL_LO + LANES
N_PROJ = COL_VRES + LANES
PROJ_TN = 512
N_PROJ_PAD = -(-N_PROJ // PROJ_TN) * PROJ_TN
(G_AQ, G_AK, G_AV, G_AG, G_BQ, G_BF, G_BI, G_BG, G_CR, G_CK, G_CV, G_CG, G_MA, G_MB, G_MC) = range(N_GROUPS)
VMEM_LIMIT = 56 * 1024 * 1024


def _dot(a, b):
    return jnp.dot(a, b, preferred_element_type=F32)


def _dot_nt(a, b):
    return lax.dot_general(a, b, (((1,), (1,)), ((), ())), preferred_element_type=F32)


def _dot_tn(a, b):
    return lax.dot_general(a, b, (((0,), (0,)), ((), ())), preferred_element_type=F32)


def _split2(x):
    hi = x.astype(BF16)
    lo = (x - hi.astype(F32)).astype(BF16)
    return hi, lo


def _split3(x):
    hi = x.astype(BF16)
    r = x - hi.astype(F32)
    mid = r.astype(BF16)
    lo = (r - mid.astype(F32)).astype(BF16)
    return hi, mid, lo


def _sigmoid(x):
    return 1.0 / (1.0 + jnp.exp(-x))


def _silu(x):
    return x * _sigmoid(x)


def _seg_mask(n, seg):
    r = lax.broadcasted_iota(jnp.int32, (n, n), 0) // seg
    c = lax.broadcasted_iota(jnp.int32, (n, n), 1) // seg
    return r == c


def _seg_ones(n, seg):
    return _seg_mask(n, seg).astype(BF16)


def _segsum(x, ones_bf16):
    hi, lo = _split2(x)
    return _dot(hi, ones_bf16) + _dot(lo, ones_bf16)


def _tri_incl(n):
    r = lax.broadcasted_iota(jnp.int32, (n, n), 0)
    c = lax.broadcasted_iota(jnp.int32, (n, n), 1)
    return (c <= r).astype(BF16)


def _cumsum_rows(x, tri_bf16):
    hi, mid, lo = _split3(x)
    return _dot(tri_bf16, hi) + _dot(tri_bf16, mid) + _dot(tri_bf16, lo)


def _params(sem, vmem=None):
    return pltpu.CompilerParams(dimension_semantics=sem, vmem_limit_bytes=vmem or VMEM_LIMIT)


def _proj_kernel(x_ref, g_ref, w_ref, o_ref, h_scr):
    @pl.when(pl.program_id(1) == 0)
    def _():
        x = x_ref[...]
        ms = jnp.mean(x * x, axis=-1, keepdims=True)
        h_scr[...] = (x * lax.rsqrt(ms + EPS) * g_ref[...]).astype(BF16)

    o_ref[...] = _dot(h_scr[...], w_ref[...]).astype(o_ref.dtype)


def _proj(x2d, g, w_bf16):
    m = x2d.shape[0]
    tm = min(2048, m)
    n = w_bf16.shape[1]
    return pl.pallas_call(
        _proj_kernel,
        grid=(m // tm, n // PROJ_TN),
        in_specs=[pl.BlockSpec((tm, D_MODEL), lambda i, j: (i, 0)),
                  pl.BlockSpec((1, D_MODEL), lambda i, j: (0, 0)),
                  pl.BlockSpec((D_MODEL, PROJ_TN), lambda i, j: (0, j))],
        out_specs=pl.BlockSpec((tm, PROJ_TN), lambda i, j: (i, j)),
        out_shape=jax.ShapeDtypeStruct((m, n), BF16),
        scratch_shapes=[pltpu.VMEM((tm, D_MODEL), BF16)],
        compiler_params=_params(("parallel", "arbitrary")),
        name="proj",
    )(x2d, g, w_bf16)


def _qkprep_kernel(*refs):
    q_ref, k_ref, v_ref, qg_ref, kg_ref, c_ref, s1_ref, s2_ref = refs[:8]
    q16t_ref, k32_ref, k16_ref, v32_ref, v16t_ref = refs[-5:]
    ones = _seg_ones(LANES, A_DQK)
    cosv, sin1, sin2 = c_ref[...], s1_ref[...], s2_ref[...]

    def prep(x, gain):
        ss = _segsum(x * x, ones)
        y = x * lax.rsqrt(ss * (1.0 / A_DQK) + EPS) * gain
        return y * cosv + pltpu.roll(y, LANES - A_ROT // 2, 1) * sin1 + pltpu.roll(y, A_ROT // 2, 1) * sin2

    for c in range(GROUP_W // LANES):
        sl = slice(c * LANES, (c + 1) * LANES)
        q = prep(q_ref[:, sl].astype(F32), qg_ref[...])
        q16t_ref[0, sl, :] = (q * (A_DQK ** -0.5 * LOG2_E)).T.astype(BF16)
        k = prep(k_ref[:, sl].astype(F32), kg_ref[...])
        k32_ref[0, :, sl] = k
        k16_ref[:, sl] = k.astype(BF16)
        v = v_ref[:, sl].astype(F32)
        v32_ref[0, :, sl] = v
        v16t_ref[0, sl, :] = v.T.astype(BF16)


def _qkprep(proj, qg128, kg128, cos_t, sin1_t, sin2_t, t_len, layer, depth, kv_stack):
    m = proj.shape[0]
    tm = min(512, t_len)
    nt = t_len // tm
    row = lambda g: pl.BlockSpec((tm, GROUP_W), lambda i, g=g: (i, g))
    tab = pl.BlockSpec((tm, LANES), lambda i: (i % nt, 0))
    vec = pl.BlockSpec((1, LANES), lambda i: (0, 0))
    out = pl.BlockSpec((tm, GROUP_W), lambda i: (i, 0))
    out_t = pl.BlockSpec((1, GROUP_W, tm), lambda i: (i // nt, 0, i % nt))
    transposed = jax.ShapeDtypeStruct((m // t_len, GROUP_W, t_len), BF16)
    stack = jax.ShapeDtypeStruct((depth, m, GROUP_W), F32)
    out_stack = pl.BlockSpec((1, tm, GROUP_W), lambda i: (layer, i, 0))
    in_specs = [row(G_AQ), row(G_AK), row(G_AV), vec, vec, tab, tab, tab]
    args = [proj, proj, proj, qg128, kg128, cos_t, sin1_t, sin2_t]
    aliases = {}
    if kv_stack is not None:
        aliases = {len(args): 1, len(args) + 1: 3}
        in_specs += [pl.BlockSpec(memory_space=pl.ANY)] * 2
        args += list(kv_stack)
    return pl.pallas_call(
        _qkprep_kernel,
        grid=(m // tm,),
        in_specs=in_specs,
        out_specs=[out_t, out_stack, out, out_stack, out_t],
        out_shape=[transposed, stack, jax.ShapeDtypeStruct((m, GROUP_W), BF16), stack, transposed],
        input_output_aliases=aliases,
        compiler_params=_params(("parallel",)),
        name="qkprep",
    )(*args)


def _attn_kernel(sc_ref, qt_ref, k_ref, vt_ref, g_ref, o_ref, m1, l1, a1, m2, l2, a2,
                 *, tq, tk, q_off, nk, out_scale, bounded):
    qi = pl.program_id(2)
    q_first = q_off + qi * tq
    first_chunk_end = (q_first // CHUNK) * CHUNK + CHUNK
    last_vis = ((q_first + tq - 1) // CHUNK) * CHUNK + CHUNK - 1
    n_blocks = jnp.minimum(nk, last_vis // tk + 1)
    n_full = jnp.minimum(n_blocks, first_chunk_end // tk)

    for m, l, a in ((m1, l1, a1), (m2, l2, a2)):
        m[...] = jnp.full(m.shape, NEG_BIG, F32)
        l[...] = jnp.zeros(l.shape, F32)
        a[...] = jnp.zeros(a.shape, F32)

    qt = qt_ref[0]
    dim = lax.broadcasted_iota(jnp.int32, qt.shape, 0)
    zero = jnp.zeros_like(qt)
    q_halves = (jnp.where(dim < A_DQK, qt, zero), jnp.where(dim >= A_DQK, qt, zero))

    def scores(kj, masked, q_lo=0):
        if nk == 1:
            k0, k, vt = 0, k_ref[0], vt_ref[0]
        else:
            k0 = pl.multiple_of(kj * tk, tk)
            k = k_ref[0, pl.ds(k0, tk), :]
            vt = vt_ref[0, :, pl.ds(k0, tk)]
        vis = None
        if masked:
            k_chunk = (k0 + lax.broadcasted_iota(jnp.int32, (tk, 1), 0)) // CHUNK
            q_chunk = (q_first + q_lo + lax.broadcasted_iota(jnp.int32, (1, tq - q_lo), 1)) // CHUNK
            vis = k_chunk <= q_chunk

        def score(qh):
            s = _dot(k, qh[:, q_lo:])
            return jnp.where(vis, s, NEG_BIG) if masked else s

        return score, vt

    def key_partial_sums(pr):
        return jnp.sum(pr.reshape(tk // SUBLANES, SUBLANES, pr.shape[1]), axis=0)

    def online_step(kj, masked):
        score, vt = scores(kj, masked)
        for qh, (m, l, a) in zip(q_halves, ((m1, l1, a1), (m2, l2, a2))):
            s = score(qh)
            m_prev = m[...]
            m_new = jnp.maximum(m_prev, jnp.max(s, axis=0, keepdims=True))
            pr = jnp.exp2(s - m_new)
            alpha = jnp.exp2(m_prev - m_new)
            l[...] = alpha * l[...] + key_partial_sums(pr)
            a[...] = alpha * a[...] + _dot(vt, pr.astype(BF16))
            m[...] = m_new

    def bounded_step(kj, masked, q_lo=0):
        score, vt = scores(kj, masked, q_lo)
        for qh, (m, l, a) in zip(q_halves, ((m1, l1, a1), (m2, l2, a2))):
            pr = jnp.exp2(score(qh) - bound)
            l[:, q_lo:] += key_partial_sums(pr)
            a[:, q_lo:] += _dot(vt, pr.astype(BF16))

    def run(step, split_diagonal=False):
        def full_body(kj, carry):
            step(kj, False)
            return carry

        def masked_body(kj, carry):
            step(kj, True)
            return carry

        lax.fori_loop(0, n_full, full_body, 0)
        if split_diagonal:
            step(n_full, True)
            step(n_full + 1, True, tk)
        else:
            lax.fori_loop(n_full, n_blocks, masked_body, 0)

    lam = sc_ref[0]
    bound = sc_ref[1]
    if bounded:
        in_range = bound <= ATTN_BOUND_LIMIT

        @pl.when(in_range)
        def _():
            run(bounded_step, split_diagonal=(tq == 2 * tk and q_off % tq == 0 and nk > 1))

        @pl.when(jnp.logical_not(in_range))
        def _():
            run(online_step)
    else:
        run(online_step)

    l1_tot = jnp.sum(l1[...], axis=0, keepdims=True)
    l2_tot = jnp.sum(l2[...], axis=0, keepdims=True)
    o = (a1[...] / l1_tot - lam * (a2[...] / l2_tot)).T
    ms = jnp.mean(o * o, axis=-1, keepdims=True)
    o_ref[0] = (o * lax.rsqrt(ms + EPS) * g_ref[...] * out_scale).astype(o_ref.dtype)


def _attn(q16t, k16, v16t, scalars, subln_g, q_off, out_scale, bounded):
    b, _, tq_len = q16t.shape
    tk_len = k16.shape[1]
    tq = min(1024, tq_len)
    tk = 512 if tk_len > 2048 else tk_len
    assert tq_len % tq == 0 and tk_len % tk == 0
    nq, nk = tq_len // tq, tk_len // tk
    kern = functools.partial(_attn_kernel, tq=tq, tk=tk, q_off=q_off, nk=nk, out_scale=out_scale,
                             bounded=bounded)
    run_max = pltpu.VMEM((1, tq), F32)
    key_sum = pltpu.VMEM((SUBLANES, tq), F32)
    acc = pltpu.VMEM((LANES, tq), F32)
    return pl.pallas_call(
        kern,
        grid=(b, A_HEADS, nq),
        in_specs=[pl.BlockSpec(memory_space=pltpu.SMEM),
                  pl.BlockSpec((1, LANES, tq), lambda bi, h, qi: (bi, h, qi)),
                  pl.BlockSpec((1, tk_len, LANES), lambda bi, h, qi: (bi, 0, h)),
                  pl.BlockSpec((1, LANES, tk_len), lambda bi, h, qi: (bi, h, 0)),
                  pl.BlockSpec((1, LANES), lambda bi, h, qi: (0, 0))],
        out_specs=pl.BlockSpec((1, tq, LANES), lambda bi, h, qi: (bi, qi, h)),
        out_shape=jax.ShapeDtypeStruct((b, tq_len, GROUP_W), BRANCH_DTYPE),
        scratch_shapes=[run_max, key_sum, acc, run_max, key_sum, acc],
        compiler_params=_params(("parallel", "parallel", "parallel")),
        name="attn",
    )(scalars, q16t, k16, v16t, subln_g)


def _hgrn_kernel(q_ref, f_ref, i_ref, lb_ref, g_ref, s0_ref, o_ref, sfin_ref,
                 st_scr, cum_scr, qk_scr, kin_scr, oi_scr, *, n_sub):
    c = pl.program_id(1)

    @pl.when(c == 0)
    def _():
        st_scr[...] = s0_ref[0]

    sub_rows = [slice(j * CHUNK, (j + 1) * CHUNK) for j in range(n_sub)]
    head_slices = [slice(h * LANES, (h + 1) * LANES) for h in range(B_HEADS)]

    def per_chunk_row(x, r):
        return jnp.concatenate([jnp.broadcast_to(x[rows, :][r:r + 1], (CHUNK, x.shape[1])) for rows in sub_rows],
                               axis=0)

    z = f_ref[...].astype(F32)
    lb = lb_ref[...]
    log_f = jnp.log(lb + (1.0 - lb) * _sigmoid(z))
    k_in = (1.0 - lb) * _sigmoid(-z)
    q = _silu(q_ref[...].astype(F32))
    tri = _tri_incl(CHUNK)
    cum = jnp.concatenate([_cumsum_rows(log_f[rows, :], tri) for rows in sub_rows], axis=0)
    row = lax.broadcasted_iota(jnp.int32, (CHUNK, 1), 0)

    rel = cum - per_chunk_row(cum, CHUNK // 2 - 1)
    safe = jnp.max(jnp.abs(rel)) <= HGRN_SAFE_SPAN

    @pl.when(safe)
    def _():
        qe = (q * jnp.exp(rel)).astype(BF16)
        ke = (k_in * jnp.exp(-rel)).astype(BF16)
        causal = lax.broadcasted_iota(jnp.int32, (CHUNK, CHUNK), 1) <= lax.broadcasted_iota(
            jnp.int32, (CHUNK, CHUNK), 0)
        tiles = [(rows, sl) for rows in sub_rows for sl in head_slices]
        scores = [jnp.where(causal, _dot_nt(qe[rows, sl], ke[rows, sl]), 0.0).astype(BF16) for rows, sl in tiles]
        for (rows, sl), sc in zip(tiles, scores):
            oi_scr[rows, sl] = _dot(sc, i_ref[rows, sl])

    @pl.when(jnp.logical_not(safe))
    def _():
        cum_scr[...] = cum
        qk_scr[...] = q
        kin_scr[...] = k_in

        def chunk_body(j, carry):
            r0 = pl.multiple_of(j * CHUNK, CHUNK)
            for sl in head_slices:
                cum_h = cum_scr[pl.ds(r0, CHUNK), sl]
                kin_h = kin_scr[pl.ds(r0, CHUNK), sl]
                i_h = i_ref[pl.ds(r0, CHUNK), sl].astype(F32)

                def body(g, inner, sl=sl, cum_h=cum_h, kin_h=kin_h, i_h=i_h):
                    g8 = pl.multiple_of(g * SUBLANES, SUBLANES)
                    c_tile = cum_scr[pl.ds(r0 + g8, SUBLANES), sl]
                    q_tile = qk_scr[pl.ds(r0 + g8, SUBLANES), sl]
                    out_rows = []
                    for r in range(SUBLANES):
                        dec = jnp.exp(jnp.minimum(c_tile[r:r + 1] - cum_h, 0.0))
                        col = jnp.sum(dec * (kin_h * q_tile[r:r + 1]), axis=1, keepdims=True)
                        col = jnp.where(row <= g8 + r, col, 0.0)
                        out_rows.append(jnp.sum(col * i_h, axis=0, keepdims=True))
                    oi_scr[pl.ds(r0 + g8, SUBLANES), sl] = jnp.concatenate(out_rows, axis=0)
                    return inner

                lax.fori_loop(0, CHUNK // SUBLANES, body, 0)
            return carry

        lax.fori_loop(0, n_sub, chunk_body, 0)

    cum_last = per_chunk_row(cum, CHUNK - 1)
    q_dec = (q * jnp.exp(cum)).astype(BF16)
    k_tail = (k_in * jnp.exp(cum_last - cum)).astype(BF16)
    decay = jnp.exp(cum_last)
    update = [[_dot_tn(i_ref[rows, sl], k_tail[rows, sl]) for sl in head_slices] for rows in sub_rows]
    states = [st_scr[h] for h in range(B_HEADS)]
    for j, rows in enumerate(sub_rows):
        inter = [_dot_nt(q_dec[rows, sl], st.astype(BF16)) for sl, st in zip(head_slices, states)]
        states = [st * decay[rows, sl][:1] + upd for st, sl, upd in zip(states, head_slices, update[j])]
        for sl, x in zip(head_slices, inter):
            o = oi_scr[rows, sl] + x
            ms = jnp.mean(o * o, axis=-1, keepdims=True)
            o_ref[rows, sl] = (o * lax.rsqrt(ms + EPS) * g_ref[...]).astype(o_ref.dtype)
    for h, st in enumerate(states):
        st_scr[h] = st

    sfin_ref[0] = st_scr[...]


def _hgrn(proj, lb, norm_g128, s0_t, bsz, nc):
    m = proj.shape[0]
    n_sub = _sub_chunks(nc)
    steps = nc // n_sub
    rows = n_sub * CHUNK
    row = lambda g: pl.BlockSpec((rows, GROUP_W), lambda b, c, g=g: (b * steps + c, g))
    st_spec = pl.BlockSpec((1, B_HEADS, LANES, LANES), lambda b, c: (b, 0, 0, 0))
    scratch = pltpu.VMEM((rows, GROUP_W), F32)
    return pl.pallas_call(
        functools.partial(_hgrn_kernel, n_sub=n_sub),
        grid=(bsz, steps),
        in_specs=[row(G_BQ), row(G_BF), row(G_BI),
                  pl.BlockSpec((1, GROUP_W), lambda b, c: (0, 0)),
                  pl.BlockSpec((1, LANES), lambda b, c: (0, 0)),
                  st_spec],
        out_specs=[pl.BlockSpec((rows, GROUP_W), lambda b, c: (b * steps + c, 0)), st_spec],
        out_shape=[jax.ShapeDtypeStruct((m, GROUP_W), BRANCH_DTYPE),
                   jax.ShapeDtypeStruct((bsz, B_HEADS, LANES, LANES), F32)],
        scratch_shapes=[pltpu.VMEM((B_HEADS, LANES, LANES), F32), scratch, scratch, scratch, scratch],
        compiler_params=_params(("parallel", "arbitrary")),
        name="hgrn",
    )(proj, proj, proj, lb, norm_g128, s0_t)


def _rwkprep_kernel(*refs, has_vres, n_sub):
    (cr_ref, ck_ref, cv_ref, clo_ref, spr_ref, spk_ref, spv_ref, splo_ref,
     mur_ref, muk_ref, muv_ref, mulo_ref, w0_ref, w2h_ref, w2l_ref, a0_ref, a2h_ref, a2l_ref,
     kk_ref, ka_ref, rk_ref) = refs[:21]
    pos = 21
    if has_vres:
        vres_ref, vf_ref, v0_ref, vw2h_ref, vw2l_ref = refs[pos:pos + 5]
        pos += 5
    (at_ref, rt_ref, bh_ref, kh_ref, vc_ref, bonus_ref, gam_ref, nab_ref, arb_ref, g_ref, y0_ref,
     shr_ref, shk_ref, shv_ref, shlo_ref) = refs[pos:pos + 15]
    pr_scr, pk_scr, pv_scr, plo_scr = refs[pos + 15:]
    c = pl.program_id(1)

    @pl.when(c == 0)
    def _():
        pr_scr[...] = spr_ref[0]
        pk_scr[...] = spk_ref[0]
        pv_scr[...] = spv_ref[0]
        plo_scr[...] = splo_ref[0]

    def shifted(x_ref, prev_scr, mu_ref, last_ref):
        x = x_ref[...].astype(F32)
        row = lax.broadcasted_iota(jnp.int32, x.shape, 0)
        prev = jnp.where(row == 0, prev_scr[...], pltpu.roll(x, 1, 0))
        last = x[x.shape[0] - 1:, :]
        prev_scr[...] = last
        last_ref[0] = last
        return x + (prev - x) * mu_ref[...]

    r = shifted(cr_ref, pr_scr, mur_ref, shr_ref)
    k0 = shifted(ck_ref, pk_scr, muk_ref, shk_ref)
    v = shifted(cv_ref, pv_scr, muv_ref, shv_ref)
    lo = shifted(clo_ref, plo_scr, mulo_ref, shlo_ref)

    def lowrank(x, wh_ref, wl_ref):
        xh, xl = _split2(x)
        return _dot(xh, wh_ref[...]) + _dot(xl, wh_ref[...]) + _dot(xh, wl_ref[...])

    w_in = w0_ref[...] + lowrank(jnp.tanh(lo), w2h_ref, w2l_ref)
    nw = -w_in
    softplus = jnp.maximum(nw, 0.0) + jnp.log(1.0 + jnp.exp(-jnp.abs(nw)))
    log_decay = -jnp.exp(-softplus - 0.5)
    a_sig = _sigmoid(a0_ref[...] + lowrank(lo, a2h_ref, a2l_ref))
    if has_vres:
        v_mix = _sigmoid(v0_ref[...] + lowrank(vres_ref[...].astype(F32), vw2h_ref, vw2l_ref))
        v = v + (vf_ref[...] - v) * v_mix
    vc_ref[...] = v

    ones = _seg_ones(LANES, C_DH)
    sub_rows = [slice(j * CHUNK, (j + 1) * CHUNK) for j in range(n_sub)]
    tri = _tri_incl(CHUNK)
    cum = jnp.concatenate([_cumsum_rows(log_decay[rows, :], tri) for rows in sub_rows], axis=0)
    for j, rows in enumerate(sub_rows):
        gam_ref[j] = jnp.exp(cum[rows, :][CHUNK - 1:CHUNK])
    cum_last = jnp.concatenate([jnp.broadcast_to(cum[rows, :][CHUNK - 1:CHUNK], (CHUNK, C_WIDTH))
                                for rows in sub_rows], axis=0)
    e_prev = jnp.exp(cum - log_decay)
    e_cum = jnp.exp(cum)
    e_inv = jnp.exp(-cum)
    e_tail = jnp.exp(cum_last - cum)

    lane = lax.broadcasted_iota(jnp.int32, (CHUNK, LANES), 1)
    par1 = lane >= C_DH
    tcol = lax.broadcasted_iota(jnp.int32, (CHUNK, LANES), 0)
    scol = lane % C_DH
    strict = jnp.concatenate([scol < tcol, scol < tcol], axis=1)
    incl = jnp.concatenate([scol <= tcol, scol <= tcol], axis=1)
    low_mask = jnp.concatenate([strict, incl], axis=0)

    pairs = [slice(p * LANES, (p + 1) * LANES) for p in range(C_WIDTH // LANES)]

    def head_sums(x):
        n = x.shape[0]
        stacked = _segsum(jnp.concatenate([x[:, sl] for sl in pairs], axis=0), ones)
        return jnp.concatenate([stacked[p * n:(p + 1) * n] for p in range(len(pairs))], axis=1)

    kk = k0 * kk_ref[...]
    kk = kk / jnp.maximum(jnp.sqrt(head_sums(kk * kk)), 1e-12)
    k = k0 * (1.0 + (a_sig - 1.0) * ka_ref[...])
    b_vec = kk * a_sig
    bonus_ref[...] = head_sums(r * k * rk_ref[...]) * v
    a_t = -kk * e_prev
    r_t = r * e_cum
    a16, r16 = a_t.astype(BF16), r_t.astype(BF16)
    at_ref[...] = a16
    rt_ref[...] = r16
    bh_ref[...] = (b_vec * e_tail).astype(BF16)
    kh_ref[...] = (k * e_tail).astype(BF16)
    b16, k16, v16 = (b_vec * e_inv).astype(BF16), (k * e_inv).astype(BF16), v.astype(BF16)
    zero = jnp.zeros((CHUNK, LANES), BF16)

    def block_diag(x):
        return [jnp.where(par1, zero, x), jnp.where(par1, x, zero)]

    tiles = [(rows, sl) for rows in sub_rows for sl in pairs]
    prods = [jnp.where(low_mask,
                       _dot_nt(jnp.concatenate([a16[rows, sl], r16[rows, sl]], axis=0),
                               jnp.concatenate(block_diag(b16[rows, sl]) + block_diag(k16[rows, sl]), axis=0)),
                       0.0) for rows, sl in tiles]
    gys = [_dot(prod[:, LANES:].astype(BF16), jnp.concatenate(block_diag(v16[rows, sl]), axis=0))
           for (rows, sl), prod in zip(tiles, prods)]
    for (rows, sl), prod, gy in zip(tiles, prods, gys):
        nab_ref[rows, sl] = prod[:CHUNK, :LANES].astype(BF16)
        arb_ref[rows, sl] = prod[CHUNK:, :LANES].astype(BF16)
        g_ref[rows, sl] = gy[:CHUNK].astype(BF16)
        y0_ref[rows, sl] = gy[CHUNK:]


def _rwkprep(proj, shift_parts, params, vres, bsz, nc):
    m = proj.shape[0]
    has_vres = vres is not None
    n_sub = PREP_SUB_CHUNKS if nc % PREP_SUB_CHUNKS == 0 else 1
    steps = nc // n_sub
    rows = n_sub * CHUNK
    row = lambda g: pl.BlockSpec((rows, GROUP_W), lambda b, c, g=g: (b * steps + c, g))
    lo_spec = pl.BlockSpec((rows, LANES), lambda b, c: (b * steps + c, COL_LO // LANES))
    st = lambda w: pl.BlockSpec((1, 1, w), lambda b, c: (b, 0, 0))
    vec = lambda w: pl.BlockSpec((1, w), lambda b, c: (0, 0))
    mat = lambda: pl.BlockSpec((LANES, GROUP_W), lambda b, c: (0, 0))
    out = pl.BlockSpec((rows, GROUP_W), lambda b, c: (b * steps + c, 0))
    in_specs = [row(G_CR), row(G_CK), row(G_CV), lo_spec, st(GROUP_W), st(GROUP_W), st(GROUP_W), st(LANES),
                vec(GROUP_W), vec(GROUP_W), vec(GROUP_W), vec(LANES),
                vec(GROUP_W), mat(), mat(), vec(GROUP_W), mat(), mat(),
                vec(GROUP_W), vec(GROUP_W), vec(GROUP_W)]
    args = [proj, proj, proj, proj, *shift_parts,
            params["mu_r"], params["mu_k"], params["mu_v"], params["mu_lo"],
            params["w0"], params["w2h"], params["w2l"], params["a0"], params["a2h"], params["a2l"],
            params["k_k"], params["k_a"], params["r_k"]]
    if has_vres:
        in_specs += [pl.BlockSpec((rows, LANES), lambda b, c: (b * steps + c, COL_VRES // LANES)),
                     out, vec(GROUP_W), mat(), mat()]
        args += [proj, vres["v_first"], vres["v0"], vres["w2h"], vres["w2l"]]
    big = jax.ShapeDtypeStruct((m, GROUP_W), F32)
    half = jax.ShapeDtypeStruct((m, GROUP_W), BF16)
    out_shape = [half] * 4 + [big] * 2 + [jax.ShapeDtypeStruct((bsz * nc, 1, GROUP_W), F32)] + [
        half, half, half, big] + [
        jax.ShapeDtypeStruct((bsz, 1, GROUP_W), F32)] * 3 + [jax.ShapeDtypeStruct((bsz, 1, LANES), F32)]
    out_specs = [out] * 6 + [pl.BlockSpec((n_sub, 1, GROUP_W), lambda b, c: (b * steps + c, 0, 0))] + [out] * 4 + [
        st(GROUP_W)] * 3 + [st(LANES)]
    return pl.pallas_call(
        functools.partial(_rwkprep_kernel, has_vres=has_vres, n_sub=n_sub),
        grid=(bsz, steps),
        in_specs=in_specs,
        out_specs=out_specs,
        out_shape=out_shape,
        scratch_shapes=[pltpu.VMEM((1, GROUP_W), F32)] * 3 + [pltpu.VMEM((1, LANES), F32)],
        compiler_params=_params(("parallel", "arbitrary")),
        name="rwkprep",
    )(*args)


def _rwksolve_kernel(n_ref, out_ref, t_ref):
    col = lax.broadcasted_iota(jnp.int32, (SUBLANES, LANES), 0)
    zero_tile = jnp.zeros((SUBLANES, LANES), F32)
    for t in range(CHUNK):
        n_tiles = t // SUBLANES + 1
        acc = [jnp.where(col == t - c8 * SUBLANES, 1.0, 0.0).astype(F32) for c8 in range(n_tiles)]
        for s0 in range(0, t, BF16_ROWS):
            coef = n_ref[t, s0:s0 + BF16_ROWS, :].astype(F32)
            for s in range(s0, min(s0 + BF16_ROWS, t)):
                c_row = coef[s - s0:s - s0 + 1]
                for c8 in range(s // SUBLANES + 1):
                    acc[c8] = acc[c8] + c_row * t_ref[s, c8 * SUBLANES:(c8 + 1) * SUBLANES, :]
        t_ref[t] = jnp.concatenate(acc + [zero_tile] * (CHUNK // SUBLANES - n_tiles), axis=0)
    out_ref[...] = t_ref[...].astype(out_ref.dtype)


def _rwksolve(n_bl):
    n_inst = n_bl.shape[-1]
    spec = pl.BlockSpec((CHUNK, CHUNK, LANES), lambda i: (0, 0, i))
    return pl.pallas_call(
        _rwksolve_kernel,
        grid=(n_inst // LANES,),
        in_specs=[spec],
        out_specs=spec,
        out_shape=jax.ShapeDtypeStruct(n_bl.shape, BF16),
        scratch_shapes=[pltpu.VMEM((CHUNK, CHUNK, LANES), F32)],
        compiler_params=_params(("parallel",)),
        name="rwksolve",
    )(n_bl)


def _rwkseq_kernel(tinv_ref, at_ref, g_ref, rt_ref, arb_ref, y0_ref, v_ref, bh_ref, kh_ref, gam_ref, bonus_ref,
                   lnw_ref, lnb_ref, s0_ref, o_ref, sfin_ref, st_scr, *, n_sub):
    c = pl.program_id(1)

    @pl.when(c == 0)
    def _():
        st_scr[...] = s0_ref[0]

    ones = _seg_ones(LANES, C_DH)
    lane = lax.broadcasted_iota(jnp.int32, (CHUNK, LANES), 1)
    par1 = lane >= C_DH
    bd = _seg_mask(LANES, C_DH)
    pairs = [slice(p * LANES, (p + 1) * LANES) for p in range(C_WIDTH // LANES)]
    zero = jnp.zeros((CHUNK, LANES), BF16)

    def block_diag(x):
        return jnp.concatenate([jnp.where(par1, zero, x), jnp.where(par1, x, zero)], axis=0)

    def apply_inverse(rows):
        return [_dot(tinv_ref[rows, sl],
                     jnp.concatenate([block_diag(at_ref[rows, sl]), block_diag(g_ref[rows, sl])], axis=1))
                for sl in pairs]

    states = [st_scr[p] for p in range(len(pairs))]
    sub_rows = [slice(j * CHUNK, (j + 1) * CHUNK) for j in range(n_sub)]
    wu = apply_inverse(sub_rows[0])
    y_parts = []
    for j, rows in enumerate(sub_rows):
        st16 = [st.astype(BF16) for st in states]
        u = [_dot_nt(x[:, :LANES].astype(BF16), st) + x[:, LANES:] for x, st in zip(wu, st16)]
        y_state = [_dot_nt(rt_ref[rows, sl], st) for sl, st in zip(pairs, st16)]
        if j + 1 < n_sub:
            wu = apply_inverse(sub_rows[j + 1])
        u16 = [x.astype(BF16) for x in u]
        v16 = v_ref[rows, :].astype(BF16)
        upd = [_dot_tn(jnp.concatenate([x, v16[:, sl]], axis=0),
                       jnp.concatenate([bh_ref[rows, sl], kh_ref[rows, sl]], axis=0)) for sl, x in zip(pairs, u16)]
        y_u = [_dot(arb_ref[rows, sl], block_diag(x)) for sl, x in zip(pairs, u16)]
        states = [st * gam_ref[j, :, sl] + jnp.where(bd, x, 0.0) for st, sl, x in zip(states, pairs, upd)]
        y_parts += [a + b + y0_ref[rows, sl] for sl, a, b in zip(pairs, y_state, y_u)]
    for p, st in enumerate(states):
        st_scr[p] = st
    y = jnp.concatenate(y_parts, axis=0)
    d = y - _segsum(y, ones) * (1.0 / C_DH)
    var = _segsum(d * d, ones) * (1.0 / C_DH)
    dn = d * lax.rsqrt(var + C_GN_EPS)
    for j, rows in enumerate(sub_rows):
        for p, sl in enumerate(pairs):
            tile = dn[(j * len(pairs) + p) * CHUNK:(j * len(pairs) + p + 1) * CHUNK]
            o_ref[rows, sl] = (tile * lnw_ref[:, sl] + lnb_ref[:, sl] + bonus_ref[rows, sl]).astype(o_ref.dtype)

    sfin_ref[0] = st_scr[...]


def _sub_chunks(nc):
    return SEQ_SUB_CHUNKS if nc % SEQ_SUB_CHUNKS == 0 else 1


def _rwkseq(tinv, at, g_mat, rt, arb, y0, v, bh, kh, gam, bonus, ln_w, ln_b, s0_bd, bsz, nc):
    m = tinv.shape[0]
    n_sub = _sub_chunks(nc)
    steps = nc // n_sub
    row = pl.BlockSpec((n_sub * CHUNK, GROUP_W), lambda b, c: (b * steps + c, 0))
    vec = pl.BlockSpec((1, GROUP_W), lambda b, c: (0, 0))
    st_spec = pl.BlockSpec((1, C_WIDTH // LANES, LANES, LANES), lambda b, c: (b, 0, 0, 0))
    return pl.pallas_call(
        functools.partial(_rwkseq_kernel, n_sub=n_sub),
        grid=(bsz, steps),
        in_specs=[row] * 9 + [pl.BlockSpec((n_sub, 1, GROUP_W), lambda b, c: (b * steps + c, 0, 0)), row, vec, vec,
                  st_spec],
        out_specs=[row, st_spec],
        out_shape=[jax.ShapeDtypeStruct((m, GROUP_W), BRANCH_DTYPE),
                   jax.ShapeDtypeStruct((bsz, C_WIDTH // LANES, LANES, LANES), F32)],
        scratch_shapes=[pltpu.VMEM((C_WIDTH // LANES, LANES, LANES), F32)],
        compiler_params=_params(("parallel", "arbitrary")),
        name="rwkseq",
    )(tinv, at, g_mat, rt, arb, y0, v, bh, kh, gam, bonus, ln_w, ln_b, s0_bd)


def _merge_kernel(x_ref, oa_ref, ob_ref, oc_ref, ag_ref, bg_ref, cg_ref, ma_ref, mb_ref, mc_ref,
                  wa_ref, wb_ref, wc_ref, wo_ref, y_ref):
    def branch(o_ref, gate_ref, w_ref):
        return _dot((o_ref[...] * _silu(gate_ref[...].astype(F32))).astype(BF16), w_ref[...])

    merged = (_sigmoid(ma_ref[...].astype(F32)) * branch(oa_ref, ag_ref, wa_ref)
              + _sigmoid(mb_ref[...].astype(F32)) * branch(ob_ref, bg_ref, wb_ref)
              + _sigmoid(mc_ref[...].astype(F32)) * branch(oc_ref, cg_ref, wc_ref))
    y_ref[...] = x_ref[...] + _dot(merged.astype(BF16), wo_ref[...])


def _merge(x2d, o_a, o_b, o_c, proj, wa, wb, wc, wo):
    m = x2d.shape[0]
    tm = min(256, m)
    row = pl.BlockSpec((tm, GROUP_W), lambda i: (i, 0))
    grp = lambda g: pl.BlockSpec((tm, GROUP_W), lambda i, g=g: (i, g))
    wsp = pl.BlockSpec((GROUP_W, D_MODEL), lambda i: (0, 0))
    return pl.pallas_call(
        _merge_kernel,
        grid=(m // tm,),
        in_specs=[row, row, row, row, grp(G_AG), grp(G_BG), grp(G_CG), grp(G_MA), grp(G_MB), grp(G_MC),
                  wsp, wsp, wsp, wsp],
        out_specs=row,
        out_shape=jax.ShapeDtypeStruct((m, D_MODEL), F32),
        compiler_params=_params(("parallel",)),
        name="merge",
    )(x2d, o_a, o_b, o_c, proj, proj, proj, proj, proj, proj, wa, wb, wc, wo)


_C_OFF = 8 * GROUP_W
_CP_R = (_C_OFF, _C_OFF + C_WIDTH)
_CP_WLO = (_CP_R[1], _CP_R[1] + C_DECAY_RANK)
_CP_K = (_CP_WLO[1], _CP_WLO[1] + C_WIDTH)
_CP_V = (_CP_K[1], _CP_K[1] + C_WIDTH)
_CP_ALO = (_CP_V[1], _CP_V[1] + C_A_RANK)
_REST = _CP_ALO[1]


def _regroup_w_in(w, vres_w1):
    cols = [w[:, :_C_OFF], w[:, _CP_R[0]:_CP_R[1]], w[:, _CP_K[0]:_CP_K[1]], w[:, _CP_V[0]:_CP_V[1]],
            w[:, _REST:], w[:, _CP_WLO[0]:_CP_WLO[1]], w[:, _CP_ALO[0]:_CP_ALO[1]]]
    vres = jnp.zeros((D_MODEL, LANES), F32)
    if vres_w1 is not None:
        vres = vres.at[:, :C_VRES_RANK].set(vres_w1)
    cols.append(vres)
    cols.append(jnp.zeros((D_MODEL, N_PROJ_PAD - N_PROJ), F32))
    return jnp.concatenate(cols, axis=1).astype(BF16)


def _split_shift(s):
    o = _C_OFF
    part = lambda a: s[:, a[0] - o:a[1] - o]
    lo = jnp.concatenate([part(_CP_WLO), part(_CP_ALO)], axis=1)
    return [x[:, None, :] for x in (part(_CP_R), part(_CP_K), part(_CP_V), lo)]


def _join_shift(r, k, v, lo):
    r, k, v, lo = (x[:, 0, :] for x in (r, k, v, lo))
    return jnp.concatenate([r, lo[:, :C_DECAY_RANK], k, v, lo[:, C_DECAY_RANK:]], axis=1)


def _pad_rows(w, row0):
    out = jnp.zeros((LANES, w.shape[1]), F32).at[row0:row0 + w.shape[0]].set(w)
    hi = out.astype(BF16)
    return hi, (out - hi.astype(F32)).astype(BF16)


def _rope_tables(pos):
    half = A_ROT // 2
    inv_freq = ROPE_THETA ** (-(jnp.arange(half, dtype=F32) * (2.0 / A_ROT)))
    ang = pos.astype(F32)[:, None] * inv_freq[None, :]
    cos, sin = jnp.cos(ang), jnp.sin(ang)
    t = pos.shape[0]
    one = jnp.ones((t, A_DQK - A_ROT), F32)
    zero = jnp.zeros((t, A_DQK - A_ROT), F32)
    z8 = jnp.zeros((t, half), F32)
    c64 = jnp.concatenate([cos, cos, one], axis=1)
    s1 = jnp.concatenate([-sin, z8, zero], axis=1)
    s2 = jnp.concatenate([z8, sin, zero], axis=1)
    tile = lambda x: jnp.concatenate([x, x], axis=1)
    return tile(c64), tile(s1), tile(s2)


def _to_lanes(x, rows):
    return x.reshape(rows, CHUNK, C_HEADS, C_DH).transpose(1, 3, 0, 2).reshape(CHUNK, C_DH, rows * C_HEADS)


def _from_lanes(x, rows):
    return x.reshape(CHUNK, C_DH, rows, C_HEADS).transpose(2, 0, 3, 1).reshape(rows * CHUNK, C_WIDTH)


def _layer(l, x2d, bsz, t_len, q_off, P, lb, past, v_first, kv_stack):
    nc = t_len // CHUNK
    m = bsz * t_len
    w_in = _regroup_w_in(P["w_in"][l], P["c_vres_w1"][l - 1] if l > 0 else None)
    proj = _proj(x2d, P["norm_g"][l][None, :], w_in)

    pos = q_off + jnp.arange(t_len, dtype=jnp.int32)
    cos_t, sin1_t, sin2_t = _rope_tables(pos)
    tile2 = lambda g: jnp.concatenate([g, g])[None, :]
    q16t, k_stack, k16, v_stack, v16t = _qkprep(proj, tile2(P["a_qnorm_g"][l]), tile2(P["a_knorm_g"][l]),
                                                cos_t, sin1_t, sin2_t, t_len, l, P["w_in"].shape[0], kv_stack)
    k16 = k16.reshape(bsz, t_len, GROUP_W)
    if past is not None:
        pk, pv = past[0][l], past[1][l]
        p_len = pk.shape[1]
        k16 = jnp.concatenate([pk.reshape(bsz, p_len, GROUP_W).astype(BF16), k16], axis=1)
        pv_t = jnp.swapaxes(pv.reshape(bsz, p_len, GROUP_W).astype(BF16), 1, 2)
        v16t = jnp.concatenate([pv_t, v16t], axis=2)
    lam_init = 0.8 - 0.6 * math.exp(-0.3 * l)
    lp = P["a_lambda"][l].astype(F32)
    lam = jnp.exp(jnp.sum(lp[0] * lp[1])) - jnp.exp(jnp.sum(lp[2] * lp[3])) + lam_init
    bound = (8.0 * LOG2_E * 1.02) * jnp.max(jnp.abs(P["a_qnorm_g"][l])) * jnp.max(jnp.abs(P["a_knorm_g"][l]))
    o_a = _attn(q16t, k16, v16t, jnp.stack([lam, bound]).astype(F32), P["a_subln_g"][l][None, :], q_off,
                1.0 - lam_init, bounded=past is None)
    o_a = o_a.reshape(m, GROUP_W)

    if past is None:
        s_h = jnp.zeros((bsz, B_HEADS, LANES, LANES), F32)
    else:
        s_h = jnp.swapaxes(past[2][l].astype(F32), -1, -2)
    o_b, s_h_new = _hgrn(proj, lb[l][None, :], P["b_norm_g"][l][None, :], s_h, bsz, nc)
    s_h_new = jnp.swapaxes(s_h_new, -1, -2)

    if past is None:
        shift_prev = jnp.zeros((bsz, 3 * C_WIDTH + C_DECAY_RANK + C_A_RANK), F32)
        s_r = jnp.zeros((bsz, C_HEADS, C_DH, C_DH), F32)
    else:
        shift_prev, s_r = past[4][l], past[3][l].astype(F32)
    mu = _split_shift(P["c_shift_mu"][l][None, :])
    w2h, w2l = _pad_rows(P["c_w2"][l], 0)
    a2h, a2l = _pad_rows(P["c_a2"][l], C_DECAY_RANK)
    cparams = {"mu_r": mu[0][0], "mu_k": mu[1][0], "mu_v": mu[2][0], "mu_lo": mu[3][0],
               "w0": P["c_w0"][l][None, :], "w2h": w2h, "w2l": w2l,
               "a0": P["c_a0"][l][None, :], "a2h": a2h, "a2l": a2l,
               "k_k": P["c_k_k"][l][None, :], "k_a": P["c_k_a"][l][None, :],
               "r_k": P["c_r_k"][l].reshape(1, C_WIDTH)}
    vres = None
    if l > 0:
        vh, vl = _pad_rows(P["c_vres_w2"][l - 1], 0)
        vres = {"v_first": v_first, "v0": P["c_v0"][l - 1][None, :], "w2h": vh, "w2l": vl}
    (a_t, r_t, b_h, k_h, v_c, bonus, gam, nab, arb, g_mat, y0, sh_r, sh_k, sh_v, sh_lo) = _rwkprep(
        proj, _split_shift(shift_prev), cparams, vres, bsz, nc)
    rows = bsz * nc
    n_inst = rows * C_HEADS
    pad = (-n_inst) % LANES
    to_lanes = lambda x: jnp.pad(_to_lanes(x, rows), ((0, 0), (0, 0), (0, pad))) if pad else _to_lanes(x, rows)
    tinv = _from_lanes(_rwksolve(to_lanes(nab))[:, :, :n_inst], rows)
    eye2 = jnp.eye(2, dtype=F32)
    s_bd = jnp.einsum("bpqvk,qr->bpqvrk", s_r.reshape(bsz, C_HEADS // 2, 2, C_DH, C_DH), eye2)
    s_bd = s_bd.reshape(bsz, C_HEADS // 2, LANES, LANES)
    o_c, s_bd_new = _rwkseq(tinv, a_t, g_mat, r_t, arb, y0, v_c, b_h, k_h, gam, bonus,
                            P["c_ln_w"][l][None, :], P["c_ln_b"][l][None, :], s_bd, bsz, nc)
    s_new6 = s_bd_new.reshape(bsz, C_HEADS // 2, 2, C_DH, 2, C_DH)
    s_r_new = jnp.stack([s_new6[:, :, 0, :, 0, :], s_new6[:, :, 1, :, 1, :]], axis=2)
    s_r_new = s_r_new.reshape(bsz, C_HEADS, C_DH, C_DH)
    shift_new = _join_shift(sh_r, sh_k, sh_v, sh_lo)

    bf = lambda w: w.astype(BF16)
    y = _merge(x2d, o_a, o_b, o_c, proj, bf(P["w_out_a"][l]), bf(P["w_out_b"][l]), bf(P["w_out_c"][l]),
               bf(P["w_o"][l]))
    return y, (s_h_new, s_r_new, shift_new), v_c, (k_stack, v_stack)


def _run_trunk(x, q_off, P, lb, past):
    bsz, t_len, _ = x.shape
    depth = P["w_in"].shape[0]
    x2d = x.reshape(bsz * t_len, D_MODEL)
    outs = ([], [], [])
    v_first, kv_stack = None, None
    for l in range(depth):
        x2d, entries, v_c, kv_stack = _layer(l, x2d, bsz, t_len, q_off, P, lb, past, v_first, kv_stack)
        if l == 0:
            v_first = v_c
        for lst, e in zip(outs, entries):
            lst.append(e)
    kv_rows = [s.reshape(depth, bsz, t_len, A_HEADS, 2 * A_DQK) for s in kv_stack]
    return x2d.reshape(bsz, t_len, D_MODEL), kv_rows + [jnp.stack(lst) for lst in outs]


def kernel(x_prompt, x_sample, cache_attn_k, cache_attn_v, state_hgrn, state_rwkv, state_rwkv_shift,
           norm_g, w_in, a_qnorm_g, a_knorm_g, a_lambda, a_subln_g, b_lower, b_norm_g,
           c_shift_mu, c_w0, c_w2, c_a0, c_a2, c_k_k, c_k_a, c_r_k, c_ln_w, c_ln_b,
           c_vres_w1, c_vres_w2, c_v0, w_out_a, w_out_b, w_out_c, w_o):
    P = {"norm_g": norm_g, "w_in": w_in, "a_qnorm_g": a_qnorm_g, "a_knorm_g": a_knorm_g,
         "a_lambda": a_lambda, "a_subln_g": a_subln_g, "b_norm_g": b_norm_g,
         "c_shift_mu": c_shift_mu, "c_w0": c_w0, "c_w2": c_w2, "c_a0": c_a0, "c_a2": c_a2,
         "c_k_k": c_k_k, "c_k_a": c_k_a, "c_r_k": c_r_k, "c_ln_w": c_ln_w, "c_ln_b": c_ln_b,
         "c_vres_w1": c_vres_w1, "c_vres_w2": c_vres_w2, "c_v0": c_v0,
         "w_out_a": w_out_a, "w_out_b": w_out_b, "w_out_c": w_out_c, "w_o": w_o}
    sm = jax.nn.softmax(b_lower.astype(F32), axis=0)
    lb = jnp.cumsum(sm, axis=0) - sm[0:1]
    past_len = cache_attn_k.shape[2]
    y_p, (k_p, v_p, hg_p, rw_p, sh_p) = _run_trunk(x_prompt, 0, P, lb, None)
    y_s, (k_s, v_s, hg_s, rw_s, sh_s) = _run_trunk(
        x_sample, past_len, P, lb, (cache_attn_k, cache_attn_v, state_hgrn, state_rwkv, state_rwkv_shift))
    return (y_p, y_s, k_p, v_p, hg_p, rw_p, sh_p, k_s, v_s, hg_s, rw_s, sh_s)
```
